```python
import math
import jax, jax.numpy as jnp
from jax import lax
import numpy as np

D_MODEL = 1024
BATCH = 4
SEQ = 4096
DEPTH = 2
DEC_BATCH = 32
DEC_SEQ = 1
PAST_LEN = 8192
PAGE_SIZE = 128

HEAD_DIM = 64
D_MIX = D_MODEL
SB_HEADS = D_MIX // (4 * HEAD_DIM)
RET_HEADS = D_MIX // (4 * HEAD_DIM)
NSA_HEADS = D_MIX // (2 * HEAD_DIM)
NSA_KV = NSA_HEADS // 4
NSA_REP = NSA_HEADS // NSA_KV
D_SB = SB_HEADS * HEAD_DIM
D_RET = RET_HEADS * HEAD_DIM
D_NSA = NSA_HEADS * HEAD_DIM
D_NSA_KV = NSA_KV * HEAD_DIM
D_FF = 4 * D_MODEL
ROPE_THETA = 500000.0
ROT_DIM = HEAD_DIM // 4
RET_THETA = 10000.0
RET_CHUNK = 128
Q_BLOCK = 128
BLK_CMP = 32
BLK_SEL = 64
TOP_N = 16
WINDOW = 512
FORCE_SCORE = 1.0e4
NEG = -1.0e30
EPS = 1e-6

kernel_name = 'hymba_style_sb_retention_nsa_decoder_step'


def rms_norm(x, g):
    xf = x.astype(jnp.float32)
    y = xf * lax.rsqrt(jnp.mean(xf * xf, axis=-1, keepdims=True) + EPS)
    return (y * g.astype(jnp.float32)).astype(x.dtype)


def rope(x, pos, rot_dim, theta):
    half = rot_dim // 2
    inv = jnp.exp(-math.log(theta) * jnp.arange(half, dtype=jnp.float32) / half)
    ang = pos.astype(jnp.float32)[:, None] * inv[None, :]
    cos = jnp.cos(ang)[None, :, None, :]
    sin = jnp.sin(ang)[None, :, None, :]
    xr = x[..., :rot_dim].astype(jnp.float32)
    x1, x2 = xr[..., :half], xr[..., half:]
    rot = jnp.concatenate([x1 * cos - x2 * sin, x2 * cos + x1 * sin], axis=-1).astype(x.dtype)
    return jnp.concatenate([rot, x[..., rot_dim:]], axis=-1)


def split_sizes():
    return [D_SB] * 3 + [D_RET] * 4 + [D_NSA] + [D_NSA_KV] * 6 + [3 * NSA_HEADS]


def project(xn, pos, w_in, q_norm_g, k_norm_g):
    b, t, _ = xn.shape
    cuts = [int(c) for c in np.cumsum(split_sizes())[:-1]]
    (sb_q, sb_k, sb_v, r_q, r_k, r_v, r_g, n_q, c_k, c_v, s_k, s_v, w_k, w_v,
     n_g) = jnp.split(xn @ w_in, cuts, axis=-1)
    hd = lambda a: a.reshape(b, t, -1, HEAD_DIM)
    nsa_qk = lambda a, g: rope(rms_norm(hd(a), g), pos, ROT_DIM, ROPE_THETA)
    return {
        'sb_q': hd(sb_q), 'sb_k': hd(sb_k), 'sb_v': hd(sb_v),
        'r_q': rope(hd(r_q), pos, HEAD_DIM, RET_THETA),
        'r_k': rope(hd(r_k), pos, HEAD_DIM, RET_THETA) * HEAD_DIM ** -0.5,
        'r_v': hd(r_v), 'r_g': hd(r_g),
        'n_q': nsa_qk(n_q, q_norm_g),
        'c_k': nsa_qk(c_k, k_norm_g[0]), 'c_v': hd(c_v),
        's_k': nsa_qk(s_k, k_norm_g[1]), 's_v': hd(s_v),
        'w_k': nsa_qk(w_k, k_norm_g[2]), 'w_v': hd(w_v),
        'n_g': jax.nn.sigmoid(n_g.reshape(b, t, NSA_HEADS, 3)),
    }


def map_query_blocks(fn, q, q_pos):
    b, t = q.shape[:2]
    nb = t // Q_BLOCK
    qb = q.reshape(b, nb, Q_BLOCK, *q.shape[2:]).swapaxes(0, 1)
    pb = q_pos.reshape(nb, Q_BLOCK)
    out = lax.map(lambda a: fn(a[0], a[1]), (qb, pb))
    return out.swapaxes(0, 1).reshape(b, t, *out.shape[3:])


def stick_breaking(q, k, v, q_pos, k_pos):
    z = jnp.einsum('bthd,bshd->bhts', q, k).astype(jnp.float32) / math.sqrt(HEAD_DIM)
    mask = (k_pos[None, :] < q_pos[:, None])[None, None]
    log_1m = jnp.where(mask, jax.nn.log_sigmoid(-z), 0.0)
    after = lax.cumsum(log_1m, axis=3, reverse=True) - log_1m
    w = jnp.where(mask, jnp.exp(jax.nn.log_sigmoid(z) + after), 0.0)
    return jnp.einsum('bhts,bshd->bthd', w.astype(v.dtype), v)


def retention_chunk(state, q, k, v, log_gamma):
    c = q.shape[1]
    idx = jnp.arange(c, dtype=jnp.float32)
    diff = idx[:, None] - idx[None, :]
    decay = jnp.where(diff >= 0, jnp.exp(jnp.maximum(diff, 0.0)[None] * log_gamma[:, None, None]), 0.0)
    qf, kf, vf = q.astype(jnp.float32), k.astype(jnp.float32), v.astype(jnp.float32)
    inner = jnp.einsum('bihd,bjhd->bhij', qf, kf) * decay[None]
    o_in = jnp.einsum('bhij,bjhe->bihe', inner, vf)
    q_dec = jnp.exp((idx[:, None] + 1.0) * log_gamma[None, :])
    o_cross = jnp.einsum('bihd,bhde->bihe', qf, state) * q_dec[None, :, :, None]
    k_dec = jnp.exp((c - 1.0 - idx)[:, None] * log_gamma[None, :])
    new_state = (state * jnp.exp(c * log_gamma)[None, :, None, None]
                 + jnp.einsum('bjhd,bjhe->bhde', kf * k_dec[None, :, :, None], vf))
    return new_state, o_in + o_cross


def retention_prompt(q, k, v, log_gamma):
    b, t, h, d = q.shape
    nc = t // RET_CHUNK
    to_c = lambda a: a.reshape(b, nc, RET_CHUNK, h, d).swapaxes(0, 1)
    state0 = jnp.zeros((b, h, d, d), jnp.float32)
    st, o = lax.scan(lambda s, xs: retention_chunk(s, xs[0], xs[1], xs[2], log_gamma),
                     state0, (to_c(q), to_c(k), to_c(v)))
    return o.swapaxes(0, 1).reshape(b, t, h, d), st


def pad_seq(a, before, after):
    return jnp.pad(a, ((0, 0), (before, after), (0, 0), (0, 0)))


def compress(k, w_pos, w_phi):
    kb = k.reshape(k.shape[0], -1, BLK_CMP, *k.shape[2:])
    return jnp.einsum('bnjgd,jd,de->bnge', kb, w_pos, w_phi)


def sel_blocks(k):
    return k.reshape(k.shape[0], -1, BLK_SEL, NSA_KV, HEAD_DIM).transpose(0, 3, 1, 2, 4)


def nsa_keys(c_k, c_v, s_k, s_v, pos_k, pos_v, phi_k, phi_v):
    extra = (-c_k.shape[1]) % BLK_SEL
    pad = lambda a: pad_seq(a, 0, extra)
    return (compress(pad(c_k), pos_k, phi_k), compress(pad(c_v), pos_v, phi_v),
            sel_blocks(pad(s_k)), sel_blocks(pad(s_v)))


def nsa_cmp_slc(q, q_pos, kcmp, vcmp, ks_blk, vs_blk):
    b, tq = q.shape[:2]
    scale = 1.0 / math.sqrt(HEAD_DIM)
    qg = q.reshape(b, tq, NSA_KV, NSA_REP, HEAD_DIM)
    nc = kcmp.shape[1]
    ns = ks_blk.shape[2]
    s = jnp.einsum('btgrd,bngd->btgrn', qg, kcmp).astype(jnp.float32) * scale
    valid = ((jnp.arange(nc) + 1) * BLK_CMP - 1)[None, :] <= q_pos[:, None]
    vmask = valid[None, :, None, None, :]
    p = jnp.where(vmask, jax.nn.softmax(jnp.where(vmask, s, NEG), axis=-1), 0.0)
    o_cmp = jnp.einsum('btgrn,bngd->btgrd', p.astype(vcmp.dtype), vcmp)
    imp = p.sum(axis=3).reshape(b, tq, NSA_KV, ns, BLK_SEL // BLK_CMP).sum(axis=-1)
    blk = jnp.arange(ns)
    forced = (blk[None, :] == 0) | (blk[None, :] == (q_pos // BLK_SEL)[:, None])
    future = (blk * BLK_SEL)[None, :] > q_pos[:, None]
    imp = jnp.where(forced[None, :, None, :], FORCE_SCORE,
                    jnp.where(future[None, :, None, :], -1.0, imp))
    _, idx = lax.top_k(imp, min(TOP_N, ns))
    b_ix = jnp.arange(b)[:, None, None, None]
    g_ix = jnp.arange(NSA_KV)[None, None, :, None]
    kg = ks_blk[b_ix, g_ix, idx]
    vg = vs_blk[b_ix, g_ix, idx]
    s2 = jnp.einsum('btgrd,btgnjd->btgrnj', qg, kg).astype(jnp.float32) * scale
    kpos = idx[..., None] * BLK_SEL + jnp.arange(BLK_SEL)
    m2 = (kpos <= q_pos[None, :, None, None, None])[:, :, :, None]
    s2 = jnp.where(m2, s2, NEG)
    p2 = jax.nn.softmax(s2.reshape(*s2.shape[:4], -1), axis=-1).reshape(s2.shape)
    o_slc = jnp.einsum('btgrnj,btgnjd->btgrd', p2.astype(vg.dtype), vg)
    return o_cmp.reshape(b, tq, NSA_HEADS, HEAD_DIM), o_slc.reshape(b, tq, NSA_HEADS, HEAD_DIM)


def window_attn(q, k, v, q_pos, k_pos):
    b, tq = q.shape[:2]
    qg = q.reshape(b, tq, NSA_KV, NSA_REP, HEAD_DIM)
    s = jnp.einsum('btgrd,bsgd->btgrs', qg, k).astype(jnp.float32) / math.sqrt(HEAD_DIM)
    d = q_pos[:, None] - k_pos[None, :]
    m = ((d >= 0) & (d <= WINDOW) & (k_pos[None, :] >= 0))[None, :, None, None, :]
    p = jax.nn.softmax(jnp.where(m, s, NEG), axis=-1)
    o = jnp.einsum('btgrs,bsgd->btgrd', p.astype(v.dtype), v)
    return o.reshape(b, tq, NSA_HEADS, HEAD_DIM)


def merge_heads(pr, o_sb, o_ret, o_nsa3, w_out):
    b, t = o_sb.shape[:2]
    of = o_ret.astype(jnp.float32)
    mu = jnp.mean(of, axis=-1, keepdims=True)
    var = jnp.mean(jnp.square(of - mu), axis=-1, keepdims=True)
    o_ret = ((of - mu) * lax.rsqrt(var + EPS)).astype(o_sb.dtype) * jax.nn.silu(pr['r_g'])
    o_nsa = jnp.sum(o_nsa3 * pr['n_g'][..., None].astype(o_nsa3.dtype), axis=-2)
    cat = jnp.concatenate([o_sb.reshape(b, t, -1), o_ret.reshape(b, t, -1),
                           o_nsa.reshape(b, t, -1)], axis=-1)
    return cat @ w_out


def sq_relu_mlp(x, g, w_up, w_down):
    h = jax.nn.relu(rms_norm(x, g) @ w_up)
    return (h * h) @ w_down


def setup_inputs(seed: int = 0) -> dict:
    key = jax.random.key(seed)
    ks = jax.random.split(key, 24)
    n_pages = PAST_LEN // PAGE_SIZE
    n_pool = (5 * DEC_BATCH * n_pages + 3) // 4
    wb = min(WINDOW, PAST_LEN)
    n_in = sum(split_sizes())
    nrm = lambda k, shape, scale=1.0: scale * jax.random.normal(k, shape, jnp.float32)
    gain = lambda k, shape: 1.0 + 0.05 * jax.random.normal(k, shape, jnp.float32)
    page_table = jax.random.permutation(ks[11], n_pool)[:DEC_BATCH * n_pages]
    page_table = page_table.reshape(DEC_BATCH, n_pages).astype(jnp.int32)
    return {
        'x_prompt': nrm(ks[0], (BATCH, SEQ, D_MODEL)),
        'x_sample': nrm(ks[1], (DEC_BATCH, DEC_SEQ, D_MODEL)),
        'cache_sb_k': nrm(ks[2], (DEPTH, n_pool, PAGE_SIZE, SB_HEADS, HEAD_DIM)),
        'cache_sb_v': nrm(ks[3], (DEPTH, n_pool, PAGE_SIZE, SB_HEADS, HEAD_DIM)),
        'cache_cmp_k': nrm(ks[4], (DEPTH, n_pool, PAGE_SIZE, NSA_KV, HEAD_DIM)),
        'cache_cmp_v': nrm(ks[5], (DEPTH, n_pool, PAGE_SIZE, NSA_KV, HEAD_DIM)),
        'cache_slc_k': nrm(ks[6], (DEPTH, n_pool, PAGE_SIZE, NSA_KV, HEAD_DIM)),
        'cache_slc_v': nrm(ks[7], (DEPTH, n_pool, PAGE_SIZE, NSA_KV, HEAD_DIM)),
        'state_win_k': nrm(ks[8], (DEPTH, DEC_BATCH, wb, NSA_KV, HEAD_DIM)),
        'state_win_v': nrm(ks[9], (DEPTH, DEC_BATCH, wb, NSA_KV, HEAD_DIM)),
        'state_ret': nrm(ks[10], (DEPTH, DEC_BATCH, RET_HEADS, HEAD_DIM, HEAD_DIM), 0.5),
        'page_table': page_table,
        'norm1_g': gain(ks[12], (DEPTH, D_MODEL)),
        'w_in': nrm(ks[13], (DEPTH, D_MODEL, n_in), D_MODEL ** -0.5),
        'nsa_q_norm': gain(ks[14], (DEPTH, HEAD_DIM)),
        'nsa_k_norm': gain(ks[15], (DEPTH, 3, HEAD_DIM)),
        'cmp_pos_k': (1.0 + 0.5 * jax.random.normal(ks[16], (DEPTH, BLK_CMP, HEAD_DIM), jnp.float32)) * BLK_CMP ** -0.5,
        'cmp_pos_v': (1.0 + 0.5 * jax.random.normal(ks[17], (DEPTH, BLK_CMP, HEAD_DIM), jnp.float32)) * BLK_CMP ** -0.5,
        'cmp_w_k': nrm(ks[18], (DEPTH, HEAD_DIM, HEAD_DIM), HEAD_DIM ** -0.5),
        'cmp_w_v': nrm(ks[19], (DEPTH, HEAD_DIM, HEAD_DIM), HEAD_DIM ** -0.5),
        'w_out': nrm(ks[20], (DEPTH, D_MIX, D_MODEL), D_MIX ** -0.5),
        'norm2_g': gain(ks[21], (DEPTH, D_MODEL)),
        'w_up': nrm(ks[22], (DEPTH, D_MODEL, D_FF), D_MODEL ** -0.5),
        'w_down': nrm(ks[23], (DEPTH, D_FF, D_MODEL), D_FF ** -0.5),
    }


def reference(x_prompt, x_sample, cache_sb_k, cache_sb_v, cache_cmp_k, cache_cmp_v,
              cache_slc_k, cache_slc_v, state_win_k, state_win_v, state_ret, page_table,
              norm1_g, w_in, nsa_q_norm, nsa_k_norm, cmp_pos_k, cmp_pos_v, cmp_w_k, cmp_w_v,
              w_out, norm2_g, w_up, w_down):
    log_gamma = jnp.log1p(-jnp.exp2(-5.0 - jnp.arange(RET_HEADS, dtype=jnp.float32)))
    seq = x_prompt.shape[1]
    n_new = x_sample.shape[1]
    past_len = page_table.shape[1] * cache_sb_k.shape[2]
    wb = state_win_k.shape[2]
    pos_p = jnp.arange(seq, dtype=jnp.int32)
    pos_s = past_len + jnp.arange(n_new, dtype=jnp.int32)
    kpos_s = jnp.arange(past_len + n_new, dtype=jnp.int32)
    wpos_s = past_len - wb + jnp.arange(wb + n_new, dtype=jnp.int32)
    xp, xs = x_prompt, x_sample
    new = {}
    put = lambda name, val: new.setdefault(name, []).append(val)
    for l in range(DEPTH):
        pr = project(rms_norm(xp, norm1_g[l]), pos_p, w_in[l], nsa_q_norm[l], nsa_k_norm[l])
        o_sb = map_query_blocks(
            lambda qb, pb: stick_breaking(qb, pr['sb_k'], pr['sb_v'], pb, pos_p), pr['sb_q'], pos_p)
        o_ret, ret_st = retention_prompt(pr['r_q'], pr['r_k'], pr['r_v'], log_gamma)
        kcmp, vcmp, ks_blk, vs_blk = nsa_keys(pr['c_k'], pr['c_v'], pr['s_k'], pr['s_v'],
                                              cmp_pos_k[l], cmp_pos_v[l], cmp_w_k[l], cmp_w_v[l])
        kw_pad = pad_seq(pr['w_k'], WINDOW, 0)
        vw_pad = pad_seq(pr['w_v'], WINDOW, 0)

        def nsa_block(qb, pb):
            o_c, o_s = nsa_cmp_slc(qb, pb, kcmp, vcmp, ks_blk, vs_blk)
            start = pb[0]
            kwb = lax.dynamic_slice_in_dim(kw_pad, start, WINDOW + Q_BLOCK, axis=1)
            vwb = lax.dynamic_slice_in_dim(vw_pad, start, WINDOW + Q_BLOCK, axis=1)
            kpos = start - WINDOW + jnp.arange(WINDOW + Q_BLOCK, dtype=jnp.int32)
            o_w = window_attn(qb, kwb, vwb, pb, kpos)
            return jnp.stack([o_c, o_s, o_w], axis=-2)

        o_nsa3 = map_query_blocks(nsa_block, pr['n_q'], pos_p)
        xp = xp + merge_heads(pr, o_sb, o_ret, o_nsa3, w_out[l])
        xp = xp + sq_relu_mlp(xp, norm2_g[l], w_up[l], w_down[l])
        keep = min(WINDOW, seq)
        put('p_sb_k', pr['sb_k']); put('p_sb_v', pr['sb_v'])
        put('p_cmp_k', pr['c_k']); put('p_cmp_v', pr['c_v'])
        put('p_slc_k', pr['s_k']); put('p_slc_v', pr['s_v'])
        put('p_win_k', pr['w_k'][:, seq - keep:]); put('p_win_v', pr['w_v'][:, seq - keep:])
        put('p_ret', ret_st)

        sr = project(rms_norm(xs, norm1_g[l]), pos_s, w_in[l], nsa_q_norm[l], nsa_k_norm[l])

        def past(cache):
            rows = cache[l][page_table]
            return rows.reshape(rows.shape[0], -1, *rows.shape[3:])

        sb_k_all = jnp.concatenate([past(cache_sb_k), sr['sb_k']], axis=1)
        sb_v_all = jnp.concatenate([past(cache_sb_v), sr['sb_v']], axis=1)
        o_sb_s = stick_breaking(sr['sb_q'], sb_k_all, sb_v_all, pos_s, kpos_s)
        ret_new, o_ret_s = retention_chunk(state_ret[l].astype(jnp.float32), sr['r_q'], sr['r_k'],
                                           sr['r_v'], log_gamma)
        kcmp_s, vcmp_s, ks_blk_s, vs_blk_s = nsa_keys(
            jnp.concatenate([past(cache_cmp_k), sr['c_k']], axis=1),
            jnp.concatenate([past(cache_cmp_v), sr['c_v']], axis=1),
            jnp.concatenate([past(cache_slc_k), sr['s_k']], axis=1),
            jnp.concatenate([past(cache_slc_v), sr['s_v']], axis=1),
            cmp_pos_k[l], cmp_pos_v[l], cmp_w_k[l], cmp_w_v[l])
        o_c_s, o_s_s = nsa_cmp_slc(sr['n_q'], pos_s, kcmp_s, vcmp_s, ks_blk_s, vs_blk_s)
        kw_all = jnp.concatenate([state_win_k[l], sr['w_k']], axis=1)
        vw_all = jnp.concatenate([state_win_v[l], sr['w_v']], axis=1)
        o_w_s = window_attn(sr['n_q'], kw_all, vw_all, pos_s, wpos_s)
        o_nsa3_s = jnp.stack([o_c_s, o_s_s, o_w_s], axis=-2)
        xs = xs + merge_heads(sr, o_sb_s, o_ret_s, o_nsa3_s, w_out[l])
        xs = xs + sq_relu_mlp(xs, norm2_g[l], w_up[l], w_down[l])
        put('s_sb_k', sr['sb_k']); put('s_sb_v', sr['sb_v'])
        put('s_cmp_k', sr['c_k']); put('s_cmp_v', sr['c_v'])
        put('s_slc_k', sr['s_k']); put('s_slc_v', sr['s_v'])
        put('s_win_k', kw_all[:, n_new:]); put('s_win_v', vw_all[:, n_new:])
        put('s_ret', ret_new)
    st = lambda name: jnp.stack(new[name])
    return (xp, xs,
            st('p_sb_k'), st('p_sb_v'), st('p_cmp_k'), st('p_cmp_v'), st('p_slc_k'), st('p_slc_v'),
            st('p_win_k'), st('p_win_v'), st('p_ret'),
            st('s_sb_k'), st('s_sb_v'), st('s_cmp_k'), st('s_cmp_v'), st('s_slc_k'), st('s_slc_v'),
            st('s_win_k'), st('s_win_v'), st('s_ret'))
```

```python
import functools
import math

import numpy as np
import jax
import jax.numpy as jnp
from jax import lax
from jax.experimental import pallas as pl
from jax.experimental.pallas import tpu as pltpu

HEAD_DIM = 64
SB_HEADS = 4
RET_HEADS = 4
NSA_HEADS = 8
NSA_KV = 2
NSA_REP = NSA_HEADS // NSA_KV
D_SB = SB_HEADS * HEAD_DIM
D_RET = RET_HEADS * HEAD_DIM
D_NSA = NSA_HEADS * HEAD_DIM
D_KV = NSA_KV * HEAD_DIM
ROPE_THETA = 500000.0
ROT_DIM = HEAD_DIM // 4
RET_THETA = 10000.0
BLK_CMP = 32
BLK_SEL = 64
TOP_N = 16
WINDOW = 512
FORCE_SCORE = 1.0e4
NEG = -1.0e30
EPS = 1e-6
QK_SCALE = HEAD_DIM ** -0.5

LANES = 128
VMEM_LIMIT = 56 * 1024 * 1024

BF = jnp.bfloat16
F32 = jnp.float32
I32 = jnp.int32

_C_SBQ, _C_SBK, _C_SBV = 0, 256, 512
_C_RQ, _C_RK, _C_RV, _C_RG = 768, 1024, 1280, 1536
_C_NQ = 1792
_C_CK, _C_CV, _C_SK, _C_SV, _C_WK, _C_WV = 2304, 2432, 2560, 2688, 2816, 2944
_C_NG = 3072
_N_IN = 3096
_N_IN_PAD = 3200
_S_RQ, _S_RK, _S_NQ, _S_CK, _S_SK, _S_WK = 0, 256, 512, 1024, 1152, 1280
_N_SWAP = 1408


def _dot(a, b):
    return jnp.dot(a, b, preferred_element_type=F32)


def _dot_nt(a, b):
    return lax.dot_general(a, b, (((1,), (1,)), ((), ())), preferred_element_type=F32)


def _dot_tn(a, b):
    return lax.dot_general(a, b, (((0,), (0,)), ((), ())), preferred_element_type=F32)


def _split2(x):
    hi = x.astype(BF)
    lo = (x - hi.astype(F32)).astype(BF)
    return hi, lo


def _split3(x):
    hi = x.astype(BF)
    r = x - hi.astype(F32)
    mid = r.astype(BF)
    lo = (r - mid.astype(F32)).astype(BF)
    return hi, mid, lo


def _dot_x2(x, w):
    hi, lo = _split2(x)
    return _dot(hi, w) + _dot(lo, w)


def _seg_mean_mat(n):
    r = lax.broadcasted_iota(I32, (n, n), 0) // HEAD_DIM
    c = lax.broadcasted_iota(I32, (n, n), 1) // HEAD_DIM
    return jnp.where(r == c, 1.0 / HEAD_DIM, 0.0).astype(BF)


def _cparams(sem, vmem=VMEM_LIMIT):
    return pltpu.CompilerParams(dimension_semantics=sem, vmem_limit_bytes=vmem)


def _const_spec(shape):
    nd = len(shape)
    return pl.BlockSpec(shape, lambda *a: (0,) * nd)


def _proj_kernel(x_ref, g1_ref, wm_ref, ws_ref, cr_ref, sr_ref, cn_ref, sn_ref, gn_ref, gs_ref,
                 sbk_f, sbv_f, ck_f, cv_f, sk_f, sv_f, wk_f, wv_f,
                 sbq_b, sbk_b, sbv_b, rq_b, rk_b, rv_b, rg_f, nq_b, sk_b, sv_b, wk_b, wv_b, ng_f):
    x = x_ref[...]
    ms = jnp.mean(x * x, axis=-1, keepdims=True)
    xn = (x * lax.rsqrt(ms + EPS) * g1_ref[...]).astype(BF)

    def mm(w_ref, lo, n):
        return _dot(xn, w_ref[:, lo:lo + n])

    sbq_b[...] = (mm(wm_ref, _C_SBQ, D_SB) * QK_SCALE).astype(BF)
    k = mm(wm_ref, _C_SBK, D_SB)
    sbk_f[...] = k
    sbk_b[...] = k.astype(BF)
    v = mm(wm_ref, _C_SBV, D_SB)
    sbv_f[...] = v
    sbv_b[...] = v.astype(BF)

    cr = cr_ref[...]
    sr = sr_ref[...]
    rq_b[...] = (mm(wm_ref, _C_RQ, D_RET) * cr + mm(ws_ref, _S_RQ, D_RET) * sr).astype(BF)
    rk_b[...] = ((mm(wm_ref, _C_RK, D_RET) * cr + mm(ws_ref, _S_RK, D_RET) * sr) * QK_SCALE).astype(BF)
    rv_b[...] = mm(wm_ref, _C_RV, D_RET).astype(BF)
    rg_f[...] = mm(wm_ref, _C_RG, D_RET)

    seg = _seg_mean_mat(LANES)
    cn = cn_ref[...]
    sn = sn_ref[...]

    def normrope(cm, cs, gi):
        y = mm(wm_ref, cm, LANES)
        ysw = mm(ws_ref, cs, LANES)
        r = lax.rsqrt(_dot_x2(y * y, seg) + EPS)
        g = gn_ref[gi:gi + 1, :]
        gsw = gs_ref[gi:gi + 1, :]
        return r * (y * (g * cn) + ysw * (gsw * sn))

    for c in range(D_NSA // LANES):
        nq_b[:, c * LANES:(c + 1) * LANES] = (
            normrope(_C_NQ + c * LANES, _S_NQ + c * LANES, 0) * QK_SCALE).astype(BF)
    ck_f[...] = normrope(_C_CK, _S_CK, 1)
    cv_f[...] = mm(wm_ref, _C_CV, D_KV)
    k = normrope(_C_SK, _S_SK, 2)
    sk_f[...] = k
    sk_b[...] = k.astype(BF)
    v = mm(wm_ref, _C_SV, D_KV)
    sv_f[...] = v
    sv_b[...] = v.astype(BF)
    k = normrope(_C_WK, _S_WK, 3)
    wk_f[...] = k
    wk_b[...] = k.astype(BF)
    v = mm(wm_ref, _C_WV, D_KV)
    wv_f[...] = v
    wv_b[...] = v.astype(BF)
    ng_f[...] = jax.nn.sigmoid(mm(wm_ref, _C_NG, LANES))


def _project(x2d, g1, wm, ws, tabs, gn, gs, *, tm, tab_blocks):
    n, d = x2d.shape
    cr, sr, cn, sn = tabs
    grid = (n // tm,)
    tok = lambda w: pl.BlockSpec((tm, w), lambda i: (i, 0))
    tab = lambda w: pl.BlockSpec((tm, w), lambda i: (i % tab_blocks, 0))
    f32 = lambda w: jax.ShapeDtypeStruct((n, w), F32)
    bf = lambda w: jax.ShapeDtypeStruct((n, w), BF)
    out_w_f = [D_SB, D_SB] + [D_KV] * 6
    out_shape = ([f32(w) for w in out_w_f]
                 + [bf(D_SB)] * 3 + [bf(D_RET)] * 3 + [f32(D_RET), bf(D_NSA)] + [bf(D_KV)] * 4 + [f32(LANES)])
    out_w = out_w_f + [D_SB] * 3 + [D_RET] * 3 + [D_RET, D_NSA] + [D_KV] * 4 + [LANES]
    return pl.pallas_call(
        _proj_kernel,
        out_shape=out_shape,
        grid=grid,
        in_specs=[tok(d), _const_spec((1, d)), _const_spec(wm.shape), _const_spec(ws.shape),
                  tab(D_RET), tab(D_RET), tab(LANES), tab(LANES),
                  _const_spec(gn.shape), _const_spec(gs.shape)],
        out_specs=[tok(w) for w in out_w],
        compiler_params=_cparams(("parallel",)),
        name="proj",
    )(x2d, g1, wm, ws, cr, sr, cn, sn, gn, gs)


def _softplus(z):
    return jnp.maximum(z, 0.0) + jnp.log(1.0 + jnp.exp(-jnp.abs(z)))


def _sb_kernel(q_ref, k_ref, v_ref, o_ref, c_ref, acc_ref, *, tq, tk):
    i = pl.program_id(1)
    m = SB_HEADS * tq
    q = q_ref[...]
    lane_h = lax.broadcasted_iota(I32, (tq, D_SB), 1) // HEAD_DIM
    qs = jnp.concatenate([jnp.where(lane_h == h, q, jnp.zeros_like(q)) for h in range(SB_HEADS)], axis=0)
    q_pos = i * tq + lax.broadcasted_iota(I32, (m, tk), 0) % tq
    col = lax.broadcasted_iota(I32, (m, tk), 1)
    tri = (lax.broadcasted_iota(I32, (tk, tk), 0) > lax.broadcasted_iota(I32, (tk, tk), 1)).astype(BF)
    nt = (i * tq) // tk + 1

    c_ref[...] = jnp.zeros_like(c_ref)
    acc_ref[...] = jnp.zeros_like(acc_ref)

    def tile(j, masked):
        off = pl.multiple_of(j * tk, tk)
        z = _dot_nt(qs, k_ref[pl.ds(off, tk), :])
        l1m = -_softplus(z)
        if masked:
            mask = (off + col) < q_pos
            l1m = jnp.where(mask, l1m, 0.0)
        c = c_ref[...]
        after = c + _dot_x2(l1m, tri)
        lw = z + l1m + after
        if masked:
            lw = jnp.where(mask, lw, NEG)
        w = jnp.exp(lw)
        acc_ref[...] += _dot(w.astype(BF), v_ref[pl.ds(off, tk), :])
        c_ref[...] = c + jnp.sum(l1m, axis=1, keepdims=True)

    tile(nt - 1, True)

    def body(jj, carry):
        tile(nt - 2 - jj, False)
        return carry

    lax.fori_loop(0, nt - 1, body, 0)

    acc = acc_ref[...]
    out = jnp.zeros((tq, D_SB), F32)
    for h in range(SB_HEADS):
        out = out + jnp.where(lane_h == h, acc[h * tq:(h + 1) * tq], 0.0)
    o_ref[...] = out.astype(BF)


def _sb_prompt(q, k, v, *, b, t, tq=128, tk=256):
    n = b * t
    nq = t // tq
    return pl.pallas_call(
        functools.partial(_sb_kernel, tq=tq, tk=tk),
        out_shape=jax.ShapeDtypeStruct((n, D_SB), BF),
        grid=(b, nq),
        in_specs=[pl.BlockSpec((tq, D_SB), lambda bi, i: (bi * nq + i, 0)),
                  pl.BlockSpec((t, D_SB), lambda bi, i: (bi, 0)),
                  pl.BlockSpec((t, D_SB), lambda bi, i: (bi, 0))],
        out_specs=pl.BlockSpec((tq, D_SB), lambda bi, i: (bi * nq + i, 0)),
        scratch_shapes=[pltpu.VMEM((SB_HEADS * tq, 1), F32), pltpu.VMEM((SB_HEADS * tq, D_SB), F32)],
        compiler_params=_cparams(("parallel", "parallel")),
        name="sb_prompt",
    )(q, k, v)


def _ret_kernel(q_ref, k_ref, v_ref, g_ref, dec_ref, qd_ref, kd_ref, gc_ref, o_ref, st_ref, s_scr, *, c):
    ci = pl.program_id(1)

    @pl.when(ci == 0)
    def _():
        s_scr[...] = jnp.zeros_like(s_scr)

    q = q_ref[...]
    k = k_ref[...]
    v = v_ref[...]
    lane_h = lax.broadcasted_iota(I32, (c, D_RET), 1) // HEAD_DIM
    o = jnp.zeros((c, D_RET), F32)
    for h in range(RET_HEADS):
        kh = jnp.where(lane_h == h, k, jnp.zeros_like(k))
        vh = jnp.where(lane_h == h, v, jnp.zeros_like(v))
        s = _dot_nt(q, kh) * dec_ref[h]
        o = o + _dot(s.astype(BF), vh)
    st = s_scr[...]
    shi, slo = _split2(st)
    o = o + (_dot(q, shi) + _dot(q, slo)) * qd_ref[...]

    kd = (k.astype(F32) * kd_ref[...]).astype(BF)
    ktv = _dot_tn(kd, v)
    r = lax.broadcasted_iota(I32, (D_RET, D_RET), 0) // HEAD_DIM
    cc = lax.broadcasted_iota(I32, (D_RET, D_RET), 1) // HEAD_DIM
    new_st = st * gc_ref[...] + jnp.where(r == cc, ktv, 0.0)
    s_scr[...] = new_st
    st_ref[...] = new_st

    seg = _seg_mean_mat(D_RET)
    mu = _dot_x2(o, seg)
    d = o - mu
    var = _dot_x2(d * d, seg)
    y = d * lax.rsqrt(var + EPS)
    o_ref[...] = (y * jax.nn.silu(g_ref[...])).astype(BF)


def _ret_tables(c):
    log_gamma = np.log1p(-np.exp2(-5.0 - np.arange(RET_HEADS, dtype=np.float64)))
    idx = np.arange(c, dtype=np.float64)
    diff = idx[:, None] - idx[None, :]
    dec = np.where(diff >= 0, np.exp(np.maximum(diff, 0.0)[None] * log_gamma[:, None, None]), 0.0)
    lane_lg = np.repeat(log_gamma, HEAD_DIM)
    qd = np.exp((idx[:, None] + 1.0) * lane_lg[None, :])
    kd = np.exp((c - 1.0 - idx)[:, None] * lane_lg[None, :])
    gc = np.exp(c * lane_lg)[None, :]
    f = lambda a: jnp.asarray(a, F32)
    return f(dec), f(qd), f(kd), f(gc)


def _ret_prompt(q, k, v, g, *, b, t, c=256):
    n = b * t
    nc = t // c
    dec, qd, kd, gc = _ret_tables(c)
    tok = pl.BlockSpec((c, D_RET), lambda bi, i: (bi * nc + i, 0))
    o, st = pl.pallas_call(
        functools.partial(_ret_kernel, c=c),
        out_shape=[jax.ShapeDtypeStruct((n, D_RET), BF), jax.ShapeDtypeStruct((b, D_RET, D_RET), F32)],
        grid=(b, nc),
        in_specs=[tok, tok, tok, tok, _const_spec(dec.shape), _const_spec(qd.shape), _const_spec(kd.shape),
                  _const_spec(gc.shape)],
        out_specs=[tok, pl.BlockSpec((None, D_RET, D_RET), lambda bi, i: (bi, 0, 0))],
        scratch_shapes=[pltpu.VMEM((D_RET, D_RET), F32)],
        compiler_params=_cparams(("parallel", "arbitrary")),
        name="ret_prompt",
    )(q, k, v, g, dec, qd, kd, gc)
    return o, st


def _compress_rows(x, pos, phi_hi, phi_lo):
    r = x.shape[0]
    xs = (x.reshape(r // BLK_CMP, BLK_CMP, LANES) * pos[None]).sum(axis=1)
    hi, lo = _split2(xs)
    return _dot(hi, phi_hi) + _dot(lo, phi_hi) + _dot(hi, phi_lo)


def _compress_kernel(k_ref, v_ref, pk_ref, pv_ref, wk_ref, wv_ref, ko_ref, vo_ref):
    wk_hi, wk_lo = _split2(wk_ref[...])
    wv_hi, wv_lo = _split2(wv_ref[...])
    ko_ref[...] = _compress_rows(k_ref[...], pk_ref[...], wk_hi, wk_lo).astype(BF)
    vo_ref[...] = _compress_rows(v_ref[...], pv_ref[...], wv_hi, wv_lo).astype(BF)


def _compress_prompt(ck, cv, pk, pv, wk, wv, *, rows=1024):
    n = ck.shape[0]
    tok = pl.BlockSpec((rows, LANES), lambda i: (i, 0))
    out = pl.BlockSpec((rows // BLK_CMP, LANES), lambda i: (i, 0))
    return pl.pallas_call(
        _compress_kernel,
        out_shape=[jax.ShapeDtypeStruct((n // BLK_CMP, LANES), BF)] * 2,
        grid=(n // rows,),
        in_specs=[tok, tok, _const_spec(pk.shape), _const_spec(pv.shape), _const_spec(wk.shape),
                  _const_spec(wv.shape)],
        out_specs=[out, out],
        compiler_params=_cparams(("parallel",)),
        name="compress_prompt",
    )(ck, cv, pk, pv, wk, wv)


def _stack_heads_q(q, extra):
    tq = q.shape[0]
    lane = lax.broadcasted_iota(I32, (tq, LANES), 1)
    rows = []
    for h in range(NSA_HEADS):
        g = h // NSA_REP
        blk = q[:, (h // 2) * LANES:(h // 2 + 1) * LANES]
        src_half = h % 2
        if src_half != g:
            blk32 = pltpu.roll(blk.astype(F32), HEAD_DIM, 1).astype(BF)
        else:
            blk32 = blk
        keep = (lane // HEAD_DIM) == g
        rows.append(jnp.where(keep, blk32, jnp.zeros_like(blk32)))
    qs = jnp.concatenate(rows, axis=0)
    if extra is not None:
        qs = jnp.concatenate([qs, extra], axis=1)
    return qs


def _unstack_heads(acc, tq):
    lane = lax.broadcasted_iota(I32, (tq, LANES), 1)
    cols = []
    for c in range(NSA_HEADS // 2):
        g = (2 * c) // NSA_REP
        a = acc[(2 * c) * tq:(2 * c + 1) * tq]
        b = acc[(2 * c + 1) * tq:(2 * c + 2) * tq]
        if g == 0:
            cols.append(jnp.where(lane < HEAD_DIM, a, pltpu.roll(b, HEAD_DIM, 1)))
        else:
            cols.append(jnp.where(lane < HEAD_DIM, pltpu.roll(a, HEAD_DIM, 1), b))
    return jnp.concatenate(cols, axis=1)


def _gate_expand(gates, branch):
    r = lax.broadcasted_iota(I32, (LANES, D_NSA), 0)
    c = lax.broadcasted_iota(I32, (LANES, D_NSA), 1)
    e = (r == (c // HEAD_DIM) * 3 + branch).astype(BF)
    return _dot_x2(gates, e)


def _nsa_kernel(q_ref, gate_ref, kc_ref, vc_ref, sk_ref, sv_ref, wk_ref, wv_ref, o_ref,
                k2_scr, imp_scr, m_scr, l_scr, acc_scr, *, tq, tk, tw, t):
    i = pl.program_id(1)
    nc = t // BLK_CMP
    ns = t // BLK_SEL
    m = NSA_HEADS * tq

    @pl.when(i == 0)
    def _():
        rows = 512
        for r0 in range(0, t, rows):
            s_idx = r0 + lax.broadcasted_iota(I32, (rows, LANES), 0)
            c_idx = lax.broadcasted_iota(I32, (rows, LANES), 1)
            e = (s_idx // BLK_SEL == c_idx).astype(BF)
            k2_scr[r0:r0 + rows, :] = jnp.concatenate([sk_ref[r0:r0 + rows, :], e], axis=1)

    q = q_ref[...]
    qs = _stack_heads_q(q, None)

    kc = kc_ref[...]
    vc = vc_ref[...]
    n_idx = lax.broadcasted_iota(I32, (nc, tq), 0)
    t_idx = i * tq + lax.broadcasted_iota(I32, (nc, tq), 1)
    valid = (n_idx + 1) * BLK_CMP - 1 <= t_idx
    o_cmp = []
    imp = [jnp.zeros((nc, tq), F32) for _ in range(NSA_KV)]
    for h in range(NSA_HEADS):
        st = _dot_nt(kc, qs[h * tq:(h + 1) * tq])
        st = jnp.where(valid, st, NEG)
        mx = jnp.max(st, axis=0, keepdims=True)
        e = jnp.where(valid, jnp.exp(st - mx), 0.0)
        den = jnp.sum(e, axis=0, keepdims=True)
        p = e / jnp.where(den > 0.0, den, 1.0)
        imp[h // NSA_REP] = imp[h // NSA_REP] + p
        o_cmp.append(_dot_tn(p.astype(BF), vc))
    o_cmp = jnp.concatenate(o_cmp, axis=0)

    pair = (lax.broadcasted_iota(I32, (ns, nc), 1) // (BLK_SEL // BLK_CMP)
            == lax.broadcasted_iota(I32, (ns, nc), 0)).astype(BF)
    blk = lax.broadcasted_iota(I32, (ns, tq), 0)
    tb = i * tq + lax.broadcasted_iota(I32, (ns, tq), 1)
    forced = (blk == 0) | (blk == tb // BLK_SEL)
    future = blk * BLK_SEL > tb
    eye = (lax.broadcasted_iota(I32, (tq, tq), 0) == lax.broadcasted_iota(I32, (tq, tq), 1)).astype(BF)
    pens = []
    for g in range(NSA_KV):
        a, b, c = _split3(imp[g])
        v = _dot(pair, a) + _dot(pair, b) + _dot(pair, c)
        v = jnp.where(forced, FORCE_SCORE, jnp.where(future, -1.0, v))
        imp_scr[g] = v

        def body(j, cnt, g=g, v=v):
            vj = imp_scr[g, pl.ds(j, 1), :]
            ge = (vj >= v).astype(I32)
            gt = (vj > v).astype(I32)
            return cnt + jnp.where(blk > j, ge, gt)

        cnt = lax.fori_loop(0, ns, body, jnp.zeros((ns, tq), I32))
        sel_t = (cnt < min(TOP_N, ns)).astype(BF)
        pen = _dot_nt(eye, sel_t)
        pen = ((pen - 1.0) * (-NEG)).astype(BF)
        if ns < LANES:
            pen = jnp.concatenate([pen, jnp.zeros((tq, LANES - ns), BF)], axis=1)
        pens.append(pen)
    pen_rows = jnp.concatenate([pens[h // NSA_REP] for h in range(NSA_HEADS)], axis=0)
    qs2 = jnp.concatenate([qs, pen_rows], axis=1)

    q_pos = i * tq + lax.broadcasted_iota(I32, (m, 1), 0) % tq

    def online(scores, vals, mask):
        if mask is not None:
            scores = jnp.where(mask, scores, NEG)
        m_old = m_scr[...]
        m_new = jnp.maximum(m_old, jnp.max(scores, axis=1, keepdims=True))
        alpha = jnp.exp(m_old - m_new)
        p = jnp.exp(scores - m_new)
        l_scr[...] = alpha * l_scr[...] + jnp.sum(p, axis=1, keepdims=True)
        acc_scr[...] = alpha * acc_scr[...] + _dot(p.astype(BF), vals)
        m_scr[...] = m_new

    def reset():
        m_scr[...] = jnp.full_like(m_scr, NEG)
        l_scr[...] = jnp.zeros_like(l_scr)
        acc_scr[...] = jnp.zeros_like(acc_scr)

    def result():
        return acc_scr[...] / l_scr[...]

    reset()
    nt = (i * tq) // tk + 1
    col_k = lax.broadcasted_iota(I32, (m, tk), 1)

    def slc_body(j, carry):
        off = pl.multiple_of(j * tk, tk)
        s = _dot_nt(qs2, k2_scr[pl.ds(off, tk), :])
        online(s, sv_ref[pl.ds(off, tk), :], None)
        return carry

    lax.fori_loop(0, nt - 1, slc_body, 0)
    off = pl.multiple_of((nt - 1) * tk, tk)
    s = _dot_nt(qs2, k2_scr[pl.ds(off, tk), :])
    online(s, sv_ref[pl.ds(off, tk), :], (off + col_k) <= q_pos)
    o_slc = result()

    reset()
    col_w = lax.broadcasted_iota(I32, (m, tw), 1)
    first = jnp.maximum(i * tq - WINDOW, 0) // tw
    last = (i * tq) // tw

    def win_body(j, carry):
        off = pl.multiple_of(j * tw, tw)
        s = _dot_nt(qs, wk_ref[pl.ds(off, tw), :])
        k_pos = off + col_w
        online(s, wv_ref[pl.ds(off, tw), :], (k_pos <= q_pos) & (q_pos - k_pos <= WINDOW))
        return carry

    lax.fori_loop(first, last + 1, win_body, 0)
    o_win = result()

    gates = gate_ref[...]
    out = (_gate_expand(gates, 0) * _unstack_heads(o_cmp, tq)
           + _gate_expand(gates, 1) * _unstack_heads(o_slc, tq)
           + _gate_expand(gates, 2) * _unstack_heads(o_win, tq))
    o_ref[...] = out.astype(BF)


def _nsa_prompt(nq, gates, kc, vc, sk, sv, wk, wv, *, b, t, tq=128, tk=256, tw=128):
    n = b * t
    nqb = t // tq
    nc = t // BLK_CMP
    ns = t // BLK_SEL
    m = NSA_HEADS * tq
    tok = lambda w: pl.BlockSpec((tq, w), lambda bi, i: (bi * nqb + i, 0))
    seq = lambda rows, w: pl.BlockSpec((rows, w), lambda bi, i: (bi, 0))
    return pl.pallas_call(
        functools.partial(_nsa_kernel, tq=tq, tk=tk, tw=tw, t=t),
        out_shape=jax.ShapeDtypeStruct((n, D_NSA), BF),
        grid=(b, nqb),
        in_specs=[tok(D_NSA), tok(LANES), seq(nc, LANES), seq(nc, LANES),
                  seq(t, LANES), seq(t, LANES), seq(t, LANES), seq(t, LANES)],
        out_specs=tok(D_NSA),
        scratch_shapes=[pltpu.VMEM((t, 2 * LANES), BF), pltpu.VMEM((NSA_KV, ns, tq), F32),
                        pltpu.VMEM((m, 1), F32), pltpu.VMEM((m, 1), F32), pltpu.VMEM((m, LANES), F32)],
        compiler_params=_cparams(("parallel", "arbitrary")),
        name="nsa_prompt",
    )(nq, gates, kc, vc, sk, sv, wk, wv)


def _out_mlp_kernel(x_ref, osb_ref, oret_ref, onsa_ref, wo_ref, g2_ref, wu_ref, wd_ref, y_ref, *, ff_chunk):
    h = (x_ref[...] + _dot(osb_ref[...].astype(BF), wo_ref[0:D_SB, :])
         + _dot(oret_ref[...].astype(BF), wo_ref[D_SB:D_SB + D_RET, :])
         + _dot(onsa_ref[...].astype(BF), wo_ref[D_SB + D_RET:, :]))
    ms = jnp.mean(h * h, axis=-1, keepdims=True)
    hn = (h * lax.rsqrt(ms + EPS) * g2_ref[...]).astype(BF)
    mlp = None
    d_ff = wu_ref.shape[1]
    for c0 in range(0, d_ff, ff_chunk):
        u = jnp.maximum(_dot(hn, wu_ref[:, c0:c0 + ff_chunk]), 0.0)
        part = _dot((u * u).astype(BF), wd_ref[c0:c0 + ff_chunk, :])
        mlp = part if mlp is None else mlp + part
    y_ref[...] = h + mlp


def _out_mlp(x2d, osb, oret, onsa, wo, g2, wu, wd, *, tm, ff_chunk=1024):
    n, d = x2d.shape
    tok = lambda w: pl.BlockSpec((tm, w), lambda i: (i, 0))
    return pl.pallas_call(
        functools.partial(_out_mlp_kernel, ff_chunk=ff_chunk),
        out_shape=jax.ShapeDtypeStruct((n, d), F32),
        grid=(n // tm,),
        in_specs=[tok(d), tok(D_SB), tok(D_RET), tok(D_NSA), _const_spec(wo.shape), _const_spec((1, d)),
                  _const_spec(wu.shape), _const_spec(wd.shape)],
        out_specs=tok(d),
        compiler_params=_cparams(("parallel",)),
        name="out_mlp",
    )(x2d, osb, oret, onsa, wo, g2, wu, wd)


PAGES_PER_STEP = 8
DEC_ROWS = 16


def _page_spec(layer, width, page_of_step, rows=None, row_blk=None):
    def imap(bi, p, pt, *rest):
        return (layer, pt[bi, page_of_step(p)], 0, 0)
    return pl.BlockSpec((None, None, rows, width), imap)


def _sb_dec_kernel(pt_ref, q_ref, *refs, npp, last):
    k_refs, v_refs = refs[:npp], refs[npp:2 * npp]
    o_ref, c_scr, acc_scr = refs[2 * npp:]
    p = pl.program_id(1)
    rows = DEC_ROWS
    pg = k_refs[0].shape[0]

    @pl.when(p == 0)
    def _():
        c_scr[...] = jnp.zeros_like(c_scr)
        acc_scr[...] = jnp.zeros_like(acc_scr)

    lane_h = lax.broadcasted_iota(I32, (rows, D_SB), 1) // HEAD_DIM
    row = lax.broadcasted_iota(I32, (rows, D_SB), 0)
    q = jnp.broadcast_to(q_ref[...].astype(F32), (rows, D_SB))
    qs = jnp.where(lane_h == row, q, 0.0).astype(BF)
    tri = (lax.broadcasted_iota(I32, (pg, pg), 0) > lax.broadcasted_iota(I32, (pg, pg), 1)).astype(BF)
    c = c_scr[...]
    acc = acc_scr[...]
    for kk in range(npp):
        z = _dot_nt(qs, k_refs[kk][...].astype(BF))
        l1m = -_softplus(z)
        after = c + _dot_x2(l1m, tri)
        w = jnp.exp(z + l1m + after)
        acc = acc + _dot(w.astype(BF), v_refs[kk][...].astype(BF))
        c = c + jnp.sum(l1m, axis=1, keepdims=True)
    c_scr[...] = c
    acc_scr[...] = acc

    @pl.when(p == last)
    def _():
        o_ref[...] = jnp.sum(jnp.where(lane_h == row, acc, 0.0), axis=0, keepdims=True)


def _sb_decode(layer, page_table, q, cache_k, cache_v):
    nb, npg = page_table.shape
    npp = math.gcd(PAGES_PER_STEP, npg)
    steps = npg // npp
    pg = cache_k.shape[2]
    specs_k = [_page_spec(layer, D_SB, (lambda p, kk=kk: npg - 1 - (p * npp + kk)), rows=pg) for kk in range(npp)]
    specs_v = [_page_spec(layer, D_SB, (lambda p, kk=kk: npg - 1 - (p * npp + kk)), rows=pg) for kk in range(npp)]
    row = pl.BlockSpec((None, 1, D_SB), lambda bi, p, pt: (bi, 0, 0))
    return pl.pallas_call(
        functools.partial(_sb_dec_kernel, npp=npp, last=steps - 1),
        out_shape=jax.ShapeDtypeStruct((nb, 1, D_SB), F32),
        grid_spec=pltpu.PrefetchScalarGridSpec(
            num_scalar_prefetch=1, grid=(nb, steps),
            in_specs=[row] + specs_k + specs_v, out_specs=row,
            scratch_shapes=[pltpu.VMEM((DEC_ROWS, 1), F32), pltpu.VMEM((DEC_ROWS, D_SB), F32)]),
        compiler_params=_cparams(("parallel", "arbitrary")),
        name="sb_decode",
    )(page_table, q, *([cache_k] * npp), *([cache_v] * npp))


def _cmp_dec_kernel(pt_ref, kn_ref, vn_ref, pk_ref, pv_ref, wk_ref, wv_ref, *refs, npp, last, nblk):
    k_refs, v_refs = refs[:npp], refs[npp:2 * npp]
    ko_ref, vo_ref = refs[2 * npp:]
    p = pl.program_id(1)
    wk_hi, wk_lo = _split2(wk_ref[...])
    wv_hi, wv_lo = _split2(wv_ref[...])
    pg = k_refs[0].shape[0]
    per = npp * pg // BLK_CMP
    r0 = pl.multiple_of(p * per, per)
    ko_ref[pl.ds(r0, per), :] = _compress_rows(
        jnp.concatenate([r[...] for r in k_refs], axis=0), pk_ref[...], wk_hi, wk_lo)
    vo_ref[pl.ds(r0, per), :] = _compress_rows(
        jnp.concatenate([r[...] for r in v_refs], axis=0), pv_ref[...], wv_hi, wv_lo)

    @pl.when(p == last)
    def _():
        first = lax.broadcasted_iota(I32, (DEC_ROWS, LANES), 0) == 0
        xk = jnp.where(first, kn_ref[...] * pk_ref[0:1, :], 0.0)
        xv = jnp.where(first, vn_ref[...] * pv_ref[0:1, :], 0.0)
        a, b = _split2(xk)
        ko_ref[nblk:nblk + DEC_ROWS, :] = _dot(a, wk_hi) + _dot(b, wk_hi) + _dot(a, wk_lo)
        a, b = _split2(xv)
        vo_ref[nblk:nblk + DEC_ROWS, :] = _dot(a, wv_hi) + _dot(b, wv_hi) + _dot(a, wv_lo)


def _cmp_decode(layer, page_table, ck_new, cv_new, cache_k, cache_v, pk, pv, wk, wv):
    nb, npg = page_table.shape
    npp = math.gcd(PAGES_PER_STEP, npg)
    steps = npg // npp
    pg = cache_k.shape[2]
    nblk = npg * pg // BLK_CMP
    specs = [_page_spec(layer, D_KV, (lambda p, kk=kk: p * npp + kk), rows=pg) for kk in range(npp)]
    row = pl.BlockSpec((None, 1, D_KV), lambda bi, p, pt: (bi, 0, 0))
    cst = lambda a: pl.BlockSpec(a.shape, lambda bi, p, pt: (0,) * a.ndim)
    out = pl.BlockSpec((None, nblk + DEC_ROWS, LANES), lambda bi, p, pt: (bi, 0, 0))
    return pl.pallas_call(
        functools.partial(_cmp_dec_kernel, npp=npp, last=steps - 1, nblk=nblk),
        out_shape=[jax.ShapeDtypeStruct((nb, nblk + DEC_ROWS, LANES), F32)] * 2,
        grid_spec=pltpu.PrefetchScalarGridSpec(
            num_scalar_prefetch=1, grid=(nb, steps),
            in_specs=[row, row, cst(pk), cst(pv), cst(wk), cst(wv)] + specs + specs,
            out_specs=[out, out]),
        compiler_params=_cparams(("parallel", "arbitrary")),
        name="cmp_decode",
    )(page_table, ck_new, cv_new, pk, pv, wk, wv, *([cache_k] * npp), *([cache_v] * npp))


def _stack_heads_q_row(q):
    qf = q.astype(F32)
    row = lax.broadcasted_iota(I32, (DEC_ROWS, LANES), 0)
    lane_g = lax.broadcasted_iota(I32, (DEC_ROWS, LANES), 1) // HEAD_DIM
    out = jnp.zeros((DEC_ROWS, LANES), F32)
    for h in range(NSA_HEADS):
        g = h // NSA_REP
        blk = jnp.broadcast_to(qf[:, (h // 2) * LANES:(h // 2 + 1) * LANES], (DEC_ROWS, LANES))
        if h % 2 != g:
            blk = pltpu.roll(blk, HEAD_DIM, 1)
        out = jnp.where((row == h) & (lane_g == g), blk, out)
    return out.astype(BF)


def _unstack_heads_row(acc):
    lane = lax.broadcasted_iota(I32, (1, LANES), 1)
    accr = pltpu.roll(acc, HEAD_DIM, 1)
    cols = []
    for c in range(NSA_HEADS // 2):
        g = (2 * c) // NSA_REP
        if g == 0:
            cols.append(jnp.where(lane < HEAD_DIM, acc[2 * c:2 * c + 1], accr[2 * c + 1:2 * c + 2]))
        else:
            cols.append(jnp.where(lane < HEAD_DIM, accr[2 * c:2 * c + 1], acc[2 * c + 1:2 * c + 2]))
    return jnp.concatenate(cols, axis=1)


def _sel_dec_kernel(q_ref, kc_ref, vc_ref, ocmp_ref, idx_ref, imp_scr, *, q_pos, nblk, ns_pad):
    nrow = kc_ref.shape[0]
    qs = _stack_heads_q_row(q_ref[...])
    qs = jnp.concatenate([qs, jnp.zeros((LANES - DEC_ROWS, LANES), BF)], axis=0)
    kc = kc_ref[...].astype(BF)
    st = _dot_nt(kc, qs)
    n_idx = lax.broadcasted_iota(I32, (nrow, LANES), 0)
    valid = (n_idx + 1) * BLK_CMP - 1 <= q_pos
    st = jnp.where(valid, st, NEG)
    mx = jnp.max(st, axis=0, keepdims=True)
    e = jnp.where(valid, jnp.exp(st - mx), 0.0)
    den = jnp.sum(e, axis=0, keepdims=True)
    p = e / jnp.where(den > 0.0, den, 1.0)
    ocmp_ref[...] = _dot_tn(p.astype(BF), vc_ref[...].astype(BF))[0:NSA_HEADS]

    grp = (lax.broadcasted_iota(I32, (LANES, LANES), 0) // NSA_REP
           == lax.broadcasted_iota(I32, (LANES, LANES), 1)).astype(BF)
    a, b, c = _split3(p)
    impc = _dot(a, grp) + _dot(b, grp) + _dot(c, grp)
    pair = (lax.broadcasted_iota(I32, (ns_pad, nrow), 1) // (BLK_SEL // BLK_CMP)
            == lax.broadcasted_iota(I32, (ns_pad, nrow), 0)).astype(BF)
    a, b, c = _split3(impc)
    v = _dot(pair, a) + _dot(pair, b) + _dot(pair, c)
    ns = (nblk * BLK_CMP + 1 + BLK_SEL - 1) // BLK_SEL
    blk = lax.broadcasted_iota(I32, (ns_pad, LANES), 0)
    forced = (blk == 0) | (blk == q_pos // BLK_SEL)
    future = blk * BLK_SEL > q_pos
    v = jnp.where(forced, FORCE_SCORE, jnp.where(future, -1.0, v))
    v = jnp.where(blk < ns, v, -2.0)
    imp_scr[...] = v

    def body(j, cnt):
        vj = imp_scr[pl.ds(j, 1), :]
        ge = (vj >= v).astype(I32)
        gt = (vj > v).astype(I32)
        return cnt + jnp.where(blk > j, ge, gt)

    cnt = lax.fori_loop(0, ns_pad, body, jnp.zeros((ns_pad, LANES), I32))
    blk_f = blk.astype(F32)
    rows = [jnp.sum(jnp.where(cnt == r, blk_f, 0.0), axis=0, keepdims=True) for r in range(TOP_N)]
    idx_ref[...] = jnp.concatenate(rows, axis=0).astype(I32)


def _sel_decode(nq, kc, vc, *, q_pos, nblk):
    nb = nq.shape[0]
    nrow = kc.shape[1]
    ns_pad = ((nrow // 2 + 7) // 8) * 8
    return pl.pallas_call(
        functools.partial(_sel_dec_kernel, q_pos=q_pos, nblk=nblk, ns_pad=ns_pad),
        out_shape=[jax.ShapeDtypeStruct((nb, NSA_HEADS, LANES), F32), jax.ShapeDtypeStruct((nb, TOP_N, LANES), I32)],
        grid=(nb,),
        in_specs=[pl.BlockSpec((None, 1, D_NSA), lambda bi: (bi, 0, 0)),
                  pl.BlockSpec((None, nrow, LANES), lambda bi: (bi, 0, 0)),
                  pl.BlockSpec((None, nrow, LANES), lambda bi: (bi, 0, 0))],
        out_specs=[pl.BlockSpec((None, NSA_HEADS, LANES), lambda bi: (bi, 0, 0)),
                   pl.BlockSpec((None, TOP_N, LANES), lambda bi: (bi, 0, 0))],
        scratch_shapes=[pltpu.VMEM((ns_pad, LANES), F32)],
        compiler_params=_cparams(("parallel",)),
        name="sel_decode",
    )(nq, kc, vc)


def _nsa_dec_kernel(pt_ref, idx_ref, q_ref, gate_ref, ocmp_ref, skn_ref, svn_ref, wkn_ref, wvn_ref,
                    wk_ref, wv_ref, ck_hbm, cv_hbm, o_ref, wko_ref, wvo_ref, kbuf, vbuf, sem,
                    *, layer, ncache, per):
    nsel = NSA_KV * TOP_N
    bi = pl.program_id(0)

    def block_copies(j):
        blk = jnp.minimum(idx_ref[bi, j], ncache - 1)
        page = pt_ref[bi, blk // per]
        r0 = pl.multiple_of((blk % per) * BLK_SEL, BLK_SEL)
        return (pltpu.make_async_copy(ck_hbm.at[layer, page, pl.ds(r0, BLK_SEL), :], kbuf.at[j], sem.at[0, j]),
                pltpu.make_async_copy(cv_hbm.at[layer, page, pl.ds(r0, BLK_SEL), :], vbuf.at[j], sem.at[1, j]))

    for j in range(nsel):
        for cp in block_copies(j):
            cp.start()

    qs = _stack_heads_q_row(q_ref[...])
    qf = qs.astype(F32)
    row_g = lax.broadcasted_iota(I32, (DEC_ROWS, LANES), 0) // NSA_REP

    def attend(s, v, s_new, v_new):
        mx = jnp.maximum(s_new, jnp.max(s, axis=1, keepdims=True))
        p_new = jnp.exp(s_new - mx)
        p = jnp.exp(s - mx)
        den = p_new + jnp.sum(p, axis=1, keepdims=True)
        return (p_new * v_new + _dot(p.astype(BF), v)) / den

    bf_row = lambda r: r[...].astype(BF).astype(F32)
    s_wn = jnp.sum(qf * bf_row(wkn_ref), axis=1, keepdims=True)
    o_win = attend(_dot_nt(qs, wk_ref[...].astype(BF)), wv_ref[...].astype(BF), s_wn, bf_row(wvn_ref))

    wb = wk_ref.shape[0]
    wko_ref[0:wb - 1, :] = wk_ref[1:wb, :]
    wko_ref[wb - 1:wb, :] = wkn_ref[...]
    wvo_ref[0:wb - 1, :] = wv_ref[1:wb, :]
    wvo_ref[wb - 1:wb, :] = wvn_ref[...]

    for j in range(nsel):
        for cp in block_copies(j):
            cp.wait()

    s_new = jnp.sum(qf * bf_row(skn_ref), axis=1, keepdims=True)
    nkeys = TOP_N * BLK_SEL
    col_blk = lax.broadcasted_iota(I32, (DEC_ROWS, nkeys), 1) // BLK_SEL
    o_slc = jnp.zeros((DEC_ROWS, LANES), F32)
    for g in range(NSA_KV):
        pen = jnp.zeros((DEC_ROWS, nkeys), F32)
        for r in range(TOP_N):
            pen = jnp.where(col_blk == r, jnp.where(idx_ref[bi, g * TOP_N + r] < ncache, 0.0, NEG), pen)
        kcat = kbuf[g * TOP_N:(g + 1) * TOP_N].reshape(nkeys, D_KV).astype(BF)
        vcat = vbuf[g * TOP_N:(g + 1) * TOP_N].reshape(nkeys, D_KV).astype(BF)
        o_g = attend(_dot_nt(qs, kcat) + pen, vcat, s_new, bf_row(svn_ref))
        o_slc = jnp.where(row_g == g, o_g, o_slc)

    gates = jnp.broadcast_to(gate_ref[...], (DEC_ROWS, LANES))
    out = (_gate_expand(gates, 0)[0:1] * _unstack_heads_row(ocmp_ref[...])
           + _gate_expand(gates, 1)[0:1] * _unstack_heads_row(o_slc)
           + _gate_expand(gates, 2)[0:1] * _unstack_heads_row(o_win))
    o_ref[...] = out


def _nsa_decode(layer, page_table, idx, nq, gates, ocmp, sk_new, sv_new, wk_new, wv_new, win_k, win_v,
                cache_k, cache_v):
    nb, npg = page_table.shape
    pg = cache_k.shape[2]
    per = pg // BLK_SEL
    ncache = npg * per
    wb = win_k.shape[2]
    nsel = NSA_KV * TOP_N

    row = lambda w: pl.BlockSpec((None, 1, w), lambda bi, pt, ix: (bi, 0, 0))
    win = pl.BlockSpec((None, None, wb, D_KV), lambda bi, pt, ix: (layer, bi, 0, 0))
    wout = pl.BlockSpec((None, wb, D_KV), lambda bi, pt, ix: (bi, 0, 0))
    hbm = pl.BlockSpec(memory_space=pl.ANY)
    return pl.pallas_call(
        functools.partial(_nsa_dec_kernel, layer=layer, ncache=ncache, per=per),
        out_shape=[jax.ShapeDtypeStruct((nb, 1, D_NSA), F32), jax.ShapeDtypeStruct((nb, wb, D_KV), F32),
                   jax.ShapeDtypeStruct((nb, wb, D_KV), F32)],
        grid_spec=pltpu.PrefetchScalarGridSpec(
            num_scalar_prefetch=2, grid=(nb,),
            in_specs=[row(D_NSA), row(LANES), pl.BlockSpec((None, NSA_HEADS, LANES), lambda bi, pt, ix: (bi, 0, 0)),
                      row(D_KV), row(D_KV), row(D_KV), row(D_KV), win, win, hbm, hbm],
            out_specs=[row(D_NSA), wout, wout],
            scratch_shapes=[pltpu.VMEM((nsel, BLK_SEL, D_KV), F32), pltpu.VMEM((nsel, BLK_SEL, D_KV), F32),
                            pltpu.SemaphoreType.DMA((2, nsel))]),
        compiler_params=_cparams(("arbitrary",)),
        name="nsa_decode",
    )(page_table, idx, nq, gates, ocmp, sk_new, sv_new, wk_new, wv_new, win_k, win_v, cache_k, cache_v)


def _ret_dec_kernel(q_ref, k_ref, kcol_ref, v_ref, g_ref, gcol_ref, grow_ref, st_ref, o_ref, sto_ref):
    rows = DEC_ROWS
    st = st_ref[...]
    q = jnp.broadcast_to(q_ref[...], (rows, D_RET))
    v = jnp.broadcast_to(v_ref[...], (rows, D_RET))
    lane_h = lax.broadcasted_iota(I32, (rows, D_RET), 1) // HEAD_DIM
    row = lax.broadcasted_iota(I32, (rows, D_RET), 0)
    own = lane_h == row
    qs = jnp.where(own, q, 0.0).astype(BF)
    spread = (lax.broadcasted_iota(I32, (HEAD_DIM, D_RET), 1) % HEAD_DIM
              == lax.broadcasted_iota(I32, (HEAD_DIM, D_RET), 0)).astype(BF)
    shi, slo = _split2(st)
    cross = _dot(qs, shi) + _dot(qs, slo)
    o_cross = jnp.sum(jnp.where(own, _dot_x2(cross, spread), 0.0), axis=0, keepdims=True) * grow_ref[...]
    seg = _seg_mean_mat(D_RET)
    qk = q * jnp.broadcast_to(k_ref[...], (rows, D_RET))
    o = o_cross + (_dot_x2(qk, seg) * float(HEAD_DIM)) * v

    v4 = _dot_nt(jnp.where(own, v, 0.0).astype(BF), spread)
    pick = (lax.broadcasted_iota(I32, (D_RET, rows), 0) // HEAD_DIM
            == lax.broadcasted_iota(I32, (D_RET, rows), 1)).astype(BF)
    vexp = _dot(pick, v4.astype(BF))
    sto_ref[...] = st * gcol_ref[...] + kcol_ref[...] * vexp

    mu = _dot_x2(o, seg)
    d = o - mu
    var = _dot_x2(d * d, seg)
    y = d * lax.rsqrt(var + EPS)
    o_ref[...] = y[0:1] * jax.nn.silu(g_ref[...])


def _ret_decode(layer, q, k, v, g, state):
    nb = q.shape[0]
    log_gamma = np.log1p(-np.exp2(-5.0 - np.arange(RET_HEADS, dtype=np.float64)))
    gam = np.repeat(np.exp(log_gamma), HEAD_DIM)
    gcol = jnp.asarray(gam[:, None], F32)
    grow = jnp.asarray(gam[None, :], F32)
    kcol = k.reshape(nb, D_RET, 1)
    row = pl.BlockSpec((None, 1, D_RET), lambda bi: (bi, 0, 0))
    return pl.pallas_call(
        _ret_dec_kernel,
        out_shape=[jax.ShapeDtypeStruct((nb, 1, D_RET), F32), jax.ShapeDtypeStruct((nb, D_RET, HEAD_DIM), F32)],
        grid=(nb,),
        in_specs=[row, row, pl.BlockSpec((None, D_RET, 1), lambda bi: (bi, 0, 0)), row, row,
                  _const_spec(gcol.shape), _const_spec(grow.shape),
                  pl.BlockSpec((None, None, D_RET, HEAD_DIM), lambda bi: (layer, bi, 0, 0))],
        out_specs=[row, pl.BlockSpec((None, D_RET, HEAD_DIM), lambda bi: (bi, 0, 0))],
        compiler_params=_cparams(("parallel",)),
        name="ret_decode",
    )(q, k, kcol, v, g, gcol, grow, state)


def _swap_perm(rot_dim):
    half = rot_dim // 2
    p = np.arange(HEAD_DIM)
    p[:half] = np.arange(half, rot_dim)
    p[half:rot_dim] = np.arange(half)
    return p


def _swap_cols(start, width, rot_dim):
    p = _swap_perm(rot_dim)
    return np.concatenate([start + h * HEAD_DIM + p for h in range(width // HEAD_DIM)])


def _rope_tables(pos, rot_dim, theta, width):
    half = rot_dim // 2
    inv = jnp.exp(-math.log(theta) * jnp.arange(half, dtype=F32) / half)
    ang = pos.astype(F32)[:, None] * inv[None, :]
    cos, sin = jnp.cos(ang), jnp.sin(ang)
    ones = jnp.ones((pos.shape[0], HEAD_DIM - rot_dim), F32)
    c = jnp.concatenate([cos, cos, ones], axis=1)
    s = jnp.concatenate([-sin, sin, 0.0 * ones], axis=1)
    reps = width // HEAD_DIM
    return jnp.tile(c, (1, reps)), jnp.tile(s, (1, reps))


def _layer_params(l, norm1_g, w_in, nsa_q_norm, nsa_k_norm, cmp_pos_k, cmp_pos_v, cmp_w_k, cmp_w_v, w_out,
                  norm2_g, w_up, w_down):
    w = w_in[l]
    d = w.shape[0]
    wm = jnp.pad(w, ((0, 0), (0, _N_IN_PAD - _N_IN))).astype(BF)
    cols = np.concatenate([
        _swap_cols(_C_RQ, D_RET, HEAD_DIM), _swap_cols(_C_RK, D_RET, HEAD_DIM),
        _swap_cols(_C_NQ, D_NSA, ROT_DIM), _swap_cols(_C_CK, D_KV, ROT_DIM),
        _swap_cols(_C_SK, D_KV, ROT_DIM), _swap_cols(_C_WK, D_KV, ROT_DIM)])
    ws = w[:, cols].astype(BF)
    gains = jnp.concatenate([nsa_q_norm[l][None], nsa_k_norm[l]], axis=0)
    gn = jnp.tile(gains, (1, LANES // HEAD_DIM))
    gs = jnp.tile(gains[:, _swap_perm(ROT_DIM)], (1, LANES // HEAD_DIM))
    eye2 = jnp.eye(NSA_KV, dtype=F32)
    return dict(
        g1=norm1_g[l][None], wm=wm, ws=ws, gn=gn, gs=gs,
        pk=jnp.tile(cmp_pos_k[l], (1, NSA_KV)), pv=jnp.tile(cmp_pos_v[l], (1, NSA_KV)),
        phik=jnp.kron(eye2, cmp_w_k[l]), phiv=jnp.kron(eye2, cmp_w_v[l]),
        wo=w_out[l].astype(BF), g2=norm2_g[l][None], wu=w_up[l].astype(BF), wd=w_down[l].astype(BF))


def _prompt_layer(xp2d, prm, tabs, *, b, t, tm):
    (sbk_f, sbv_f, ck_f, cv_f, sk_f, sv_f, wk_f, wv_f,
     sbq_b, sbk_b, sbv_b, rq_b, rk_b, rv_b, rg_f, nq_b, sk_b, sv_b, wk_b, wv_b, ng_f) = _project(
        xp2d, prm['g1'], prm['wm'], prm['ws'], tabs, prm['gn'], prm['gs'], tm=tm, tab_blocks=t // tm)
    o_sb = _sb_prompt(sbq_b, sbk_b, sbv_b, b=b, t=t)
    o_ret, ret_st = _ret_prompt(rq_b, rk_b, rv_b, rg_f, b=b, t=t)
    kc, vc = _compress_prompt(ck_f, cv_f, prm['pk'], prm['pv'], prm['phik'], prm['phiv'])
    o_nsa = _nsa_prompt(nq_b, ng_f, kc, vc, sk_b, sv_b, wk_b, wv_b, b=b, t=t)
    y = _out_mlp(xp2d, o_sb, o_ret, o_nsa, prm['wo'], prm['g2'], prm['wu'], prm['wd'], tm=tm)
    keep = min(WINDOW, t)
    r4 = lambda a, h: a.reshape(b, t, h, HEAD_DIM)
    st = ret_st.reshape(b, RET_HEADS, HEAD_DIM, RET_HEADS, HEAD_DIM)
    st = jnp.stack([st[:, h, :, h, :] for h in range(RET_HEADS)], axis=1)
    caches = dict(
        p_sb_k=r4(sbk_f, SB_HEADS), p_sb_v=r4(sbv_f, SB_HEADS),
        p_cmp_k=r4(ck_f, NSA_KV), p_cmp_v=r4(cv_f, NSA_KV),
        p_slc_k=r4(sk_f, NSA_KV), p_slc_v=r4(sv_f, NSA_KV),
        p_win_k=r4(wk_f, NSA_KV)[:, t - keep:], p_win_v=r4(wv_f, NSA_KV)[:, t - keep:],
        p_ret=st)
    return y, caches


def _sample_layer(l, xs2d, prm, tabs, caches, states, page_table, *, past_len):
    nb = xs2d.shape[0]
    (sbk_f, sbv_f, ck_f, cv_f, sk_f, sv_f, wk_f, wv_f,
     sbq_b, sbk_b, sbv_b, rq_b, rk_b, rv_b, rg_f, nq_b, sk_b, sv_b, wk_b, wv_b, ng_f) = _project(
        xs2d, prm['g1'], prm['wm'], prm['ws'], tabs, prm['gn'], prm['gs'], tm=nb, tab_blocks=1)
    row = lambda a: a.astype(F32).reshape(nb, 1, a.shape[-1])
    c_sb_k, c_sb_v, c_cmp_k, c_cmp_v, c_slc_k, c_slc_v = caches
    win_k, win_v, st_ret = states
    pg = c_cmp_k.shape[2]
    nblk = page_table.shape[1] * pg // BLK_CMP
    o_sb = _sb_decode(l, page_table, row(sbq_b), c_sb_k, c_sb_v)
    o_ret, ret_new = _ret_decode(l, row(rq_b), row(rk_b), row(rv_b), row(rg_f), st_ret)
    kc, vc = _cmp_decode(l, page_table, row(ck_f), row(cv_f), c_cmp_k, c_cmp_v,
                         prm['pk'], prm['pv'], prm['phik'], prm['phiv'])
    ocmp, idx = _sel_decode(row(nq_b), kc, vc, q_pos=past_len, nblk=nblk)
    idx = jnp.transpose(idx[:, :, :NSA_KV], (0, 2, 1)).reshape(nb, NSA_KV * TOP_N)
    o_nsa, win_k_new, win_v_new = _nsa_decode(
        l, page_table, idx, row(nq_b), row(ng_f), ocmp, row(sk_f), row(sv_f), row(wk_f), row(wv_f),
        win_k, win_v, c_slc_k, c_slc_v)
    y = _out_mlp(xs2d, o_sb.reshape(nb, D_SB), o_ret.reshape(nb, D_RET), o_nsa.reshape(nb, D_NSA),
                 prm['wo'], prm['g2'], prm['wu'], prm['wd'], tm=nb)
    r4 = lambda a, h: a.reshape(nb, 1, h, HEAD_DIM)
    wb = win_k.shape[2]
    out = dict(
        s_sb_k=r4(sbk_f, SB_HEADS), s_sb_v=r4(sbv_f, SB_HEADS),
        s_cmp_k=r4(ck_f, NSA_KV), s_cmp_v=r4(cv_f, NSA_KV),
        s_slc_k=r4(sk_f, NSA_KV), s_slc_v=r4(sv_f, NSA_KV),
        s_win_k=win_k_new.reshape(nb, wb, NSA_KV, HEAD_DIM), s_win_v=win_v_new.reshape(nb, wb, NSA_KV, HEAD_DIM),
        s_ret=ret_new.reshape(nb, RET_HEADS, HEAD_DIM, HEAD_DIM))
    return y, out


def kernel(x_prompt, x_sample, cache_sb_k, cache_sb_v, cache_cmp_k, cache_cmp_v, cache_slc_k, cache_slc_v,
           state_win_k, state_win_v, state_ret, page_table, norm1_g, w_in, nsa_q_norm, nsa_k_norm, cmp_pos_k,
           cmp_pos_v, cmp_w_k, cmp_w_v, w_out, norm2_g, w_up, w_down):
    b, t, d = x_prompt.shape
    depth = w_in.shape[0]
    tm = min(512, t)
    pos_p = jnp.arange(t, dtype=I32)
    tabs_p = (*_rope_tables(pos_p, HEAD_DIM, RET_THETA, D_RET), *_rope_tables(pos_p, ROT_DIM, ROPE_THETA, LANES))
    xp = x_prompt.reshape(b * t, d)

    nb, n_new, _ = x_sample.shape
    assert n_new == 1, "sample group kernels handle one new token per sample"
    pg = cache_sb_k.shape[2]
    past_len = page_table.shape[1] * pg
    wb = state_win_k.shape[2]
    assert wb <= WINDOW and wb <= past_len
    pos_s = jnp.full((nb,), past_len, dtype=I32)
    tabs_s = (*_rope_tables(pos_s, HEAD_DIM, RET_THETA, D_RET), *_rope_tables(pos_s, ROT_DIM, ROPE_THETA, LANES))
    xs = x_sample.reshape(nb, d)
    flat = lambda c: c.reshape(*c.shape[:3], c.shape[3] * c.shape[4])
    caches = tuple(flat(c) for c in (cache_sb_k, cache_sb_v, cache_cmp_k, cache_cmp_v, cache_slc_k, cache_slc_v))
    states = (flat(state_win_k), flat(state_win_v),
              state_ret.reshape(depth, nb, RET_HEADS * HEAD_DIM, HEAD_DIM))

    new = {}
    for l in range(depth):
        prm = _layer_params(l, norm1_g, w_in, nsa_q_norm, nsa_k_norm, cmp_pos_k, cmp_pos_v, cmp_w_k, cmp_w_v,
                            w_out, norm2_g, w_up, w_down)
        xp, p_new = _prompt_layer(xp, prm, tabs_p, b=b, t=t, tm=tm)
        xs, s_new = _sample_layer(l, xs, prm, tabs_s, caches, states, page_table, past_len=past_len)
        for name, val in {**p_new, **s_new}.items():
            new.setdefault(name, []).append(val)
    st = lambda name: jnp.stack(new[name])
    return (xp.reshape(b, t, d), xs.reshape(nb, 1, d)) + tuple(st(nm) for nm in (
        'p_sb_k', 'p_sb_v', 'p_cmp_k', 'p_cmp_v', 'p_slc_k', 'p_slc_v', 'p_win_k', 'p_win_v', 'p_ret',
        's_sb_k', 's_sb_v', 's_cmp_k', 's_cmp_v', 's_slc_k', 's_slc_v', 's_win_k', 's_win_v', 's_ret'))
```

```python
import functools
import math

import numpy as np
import jax
import jax.numpy as jnp
from jax import lax
from jax.experimental import pallas as pl
from jax.experimental.pallas import tpu as pltpu

HEAD_DIM = 64
SB_HEADS = 4
RET_HEADS = 4
NSA_HEADS = 8
NSA_KV = 2
NSA_REP = NSA_HEADS // NSA_KV
D_SB = SB_HEADS * HEAD_DIM
D_RET = RET_HEADS * HEAD_DIM
D_NSA = NSA_HEADS * HEAD_DIM
D_KV = NSA_KV * HEAD_DIM
ROPE_THETA = 500000.0
ROT_DIM = HEAD_DIM // 4
RET_THETA = 10000.0
BLK_CMP = 32
BLK_SEL = 64
TOP_N = 16
WINDOW = 512
FORCE_SCORE = 1.0e4
NEG = -1.0e30
EPS = 1e-6
QK_SCALE = HEAD_DIM ** -0.5

LANES = 128
VMEM_LIMIT = 56 * 1024 * 1024

BF = jnp.bfloat16
F32 = jnp.float32
I32 = jnp.int32

_C_SBQ, _C_SBK, _C_SBV = 0, 256, 512
_C_RQ, _C_RK, _C_RV, _C_RG = 768, 1024, 1280, 1536
_C_NQ = 1792
_C_CK, _C_CV, _C_SK, _C_SV, _C_WK, _C_WV = 2304, 2432, 2560, 2688, 2816, 2944
_C_NG = 3072
_N_IN = 3096
_N_IN_PAD = 3200
_S_RQ, _S_RK, _S_NQ, _S_CK, _S_SK, _S_WK = 0, 256, 512, 1024, 1152, 1280
_N_SWAP = 1408


def _dot(a, b):
    return jnp.dot(a, b, preferred_element_type=F32)


def _dot_nt(a, b):
    return lax.dot_general(a, b, (((1,), (1,)), ((), ())), preferred_element_type=F32)


def _dot_tn(a, b):
    return lax.dot_general(a, b, (((0,), (0,)), ((), ())), preferred_element_type=F32)


def _split2(x):
    hi = x.astype(BF)
    lo = (x - hi.astype(F32)).astype(BF)
    return hi, lo


def _split3(x):
    hi = x.astype(BF)
    r = x - hi.astype(F32)
    mid = r.astype(BF)
    lo = (r - mid.astype(F32)).astype(BF)
    return hi, mid, lo


def _dot_x2(x, w):
    hi, lo = _split2(x)
    return _dot(hi, w) + _dot(lo, w)


def _seg_mean_mat(n):
    r = lax.broadcasted_iota(I32, (n, n), 0) // HEAD_DIM
    c = lax.broadcasted_iota(I32, (n, n), 1) // HEAD_DIM
    return jnp.where(r == c, 1.0 / HEAD_DIM, 0.0).astype(BF)


def _cparams(sem, vmem=VMEM_LIMIT):
    return pltpu.CompilerParams(dimension_semantics=sem, vmem_limit_bytes=vmem)


def _const_spec(shape):
    nd = len(shape)
    return pl.BlockSpec(shape, lambda *a: (0,) * nd)


def _proj_kernel(x_ref, g1_ref, wm_ref, ws_ref, cr_ref, sr_ref, cn_ref, sn_ref, gn_ref, gs_ref,
                 sbk_f, sbv_f, ck_f, cv_f, sk_f, sv_f, wk_f, wv_f,
                 sbq_b, sbk_b, sbv_b, rq_b, rk_b, rv_b, rg_f, nq_b, sk_b, sv_b, wk_b, wv_b, ng_f):
    x = x_ref[...]
    ms = jnp.mean(x * x, axis=-1, keepdims=True)
    xn = (x * lax.rsqrt(ms + EPS) * g1_ref[...]).astype(BF)

    def mm(w_ref, lo, n):
        return _dot(xn, w_ref[:, lo:lo + n])

    sbq_b[...] = (mm(wm_ref, _C_SBQ, D_SB) * QK_SCALE).astype(BF)
    k = mm(wm_ref, _C_SBK, D_SB)
    sbk_f[...] = k
    sbk_b[...] = k.astype(BF)
    v = mm(wm_ref, _C_SBV, D_SB)
    sbv_f[...] = v
    sbv_b[...] = v.astype(BF)

    cr = cr_ref[...]
    sr = sr_ref[...]
    rq_b[...] = (mm(wm_ref, _C_RQ, D_RET) * cr + mm(ws_ref, _S_RQ, D_RET) * sr).astype(BF)
    rk_b[...] = ((mm(wm_ref, _C_RK, D_RET) * cr + mm(ws_ref, _S_RK, D_RET) * sr) * QK_SCALE).astype(BF)
    rv_b[...] = mm(wm_ref, _C_RV, D_RET).astype(BF)
    rg_f[...] = mm(wm_ref, _C_RG, D_RET)

    seg = _seg_mean_mat(LANES)
    cn = cn_ref[...]
    sn = sn_ref[...]

    def normrope(cm, cs, gi):
        y = mm(wm_ref, cm, LANES)
        ysw = mm(ws_ref, cs, LANES)
        r = lax.rsqrt(_dot_x2(y * y, seg) + EPS)
        g = gn_ref[gi:gi + 1, :]
        gsw = gs_ref[gi:gi + 1, :]
        return r * (y * (g * cn) + ysw * (gsw * sn))

    for c in range(D_NSA // LANES):
        nq_b[:, c * LANES:(c + 1) * LANES] = (
            normrope(_C_NQ + c * LANES, _S_NQ + c * LANES, 0) * QK_SCALE).astype(BF)
    ck_f[...] = normrope(_C_CK, _S_CK, 1)
    cv_f[...] = mm(wm_ref, _C_CV, D_KV)
    k = normrope(_C_SK, _S_SK, 2)
    sk_f[...] = k
    sk_b[...] = k.astype(BF)
    v = mm(wm_ref, _C_SV, D_KV)
    sv_f[...] = v
    sv_b[...] = v.astype(BF)
    k = normrope(_C_WK, _S_WK, 3)
    wk_f[...] = k
    wk_b[...] = k.astype(BF)
    v = mm(wm_ref, _C_WV, D_KV)
    wv_f[...] = v
    wv_b[...] = v.astype(BF)
    ng_f[...] = jax.nn.sigmoid(mm(wm_ref, _C_NG, LANES))


def _project(x2d, g1, wm, ws, tabs, gn, gs, *, tm, tab_blocks):
    n, d = x2d.shape
    cr, sr, cn, sn = tabs
    grid = (n // tm,)
    tok = lambda w: pl.BlockSpec((tm, w), lambda i: (i, 0))
    tab = lambda w: pl.BlockSpec((tm, w), lambda i: (i % tab_blocks, 0))
    f32 = lambda w: jax.ShapeDtypeStruct((n, w), F32)
    bf = lambda w: jax.ShapeDtypeStruct((n, w), BF)
    out_w_f = [D_SB, D_SB] + [D_KV] * 6
    out_shape = ([f32(w) for w in out_w_f]
                 + [bf(D_SB)] * 3 + [bf(D_RET)] * 3 + [f32(D_RET), bf(D_NSA)] + [bf(D_KV)] * 4 + [f32(LANES)])
    out_w = out_w_f + [D_SB] * 3 + [D_RET] * 3 + [D_RET, D_NSA] + [D_KV] * 4 + [LANES]
    return pl.pallas_call(
        _proj_kernel,
        out_shape=out_shape,
        grid=grid,
        in_specs=[tok(d), _const_spec((1, d)), _const_spec(wm.shape), _const_spec(ws.shape),
                  tab(D_RET), tab(D_RET), tab(LANES), tab(LANES),
                  _const_spec(gn.shape), _const_spec(gs.shape)],
        out_specs=[tok(w) for w in out_w],
        compiler_params=_cparams(("parallel",)),
        name="proj",
    )(x2d, g1, wm, ws, cr, sr, cn, sn, gn, gs)


SB_DEAD = -120.0


def _softplus(z):
    return jnp.maximum(z, 0.0) + jnp.log(1.0 + jnp.exp(-jnp.abs(z)))


def _sb_kernel(q_ref, k_ref, v_ref, o_ref, c_ref, acc_ref, *, tq, tk):
    i = pl.program_id(1)
    m = SB_HEADS * tq
    q = q_ref[...]
    lane_h = lax.broadcasted_iota(I32, (tq, D_SB), 1) // HEAD_DIM
    qs = jnp.concatenate([jnp.where(lane_h == h, q, jnp.zeros_like(q)) for h in range(SB_HEADS)], axis=0)
    q_pos = i * tq + lax.broadcasted_iota(I32, (m, tk), 0) % tq
    col = lax.broadcasted_iota(I32, (m, tk), 1)
    tri = (lax.broadcasted_iota(I32, (tk, tk), 0) > lax.broadcasted_iota(I32, (tk, tk), 1)).astype(BF)
    nt = (i * tq) // tk + 1

    c_ref[...] = jnp.zeros_like(c_ref)
    acc_ref[...] = jnp.zeros_like(acc_ref)

    def tile(j, masked):
        off = pl.multiple_of(j * tk, tk)
        z = _dot_nt(qs, k_ref[pl.ds(off, tk), :])
        l1m = -_softplus(z)
        if masked:
            mask = (off + col) < q_pos
            l1m = jnp.where(mask, l1m, 0.0)
        c = c_ref[...]
        after = c + _dot_x2(l1m, tri)
        lw = z + l1m + after
        if masked:
            lw = jnp.where(mask, lw, NEG)
        w = jnp.exp(lw)
        acc_ref[...] += _dot(w.astype(BF), v_ref[pl.ds(off, tk), :])
        c_ref[...] = c + jnp.sum(l1m, axis=1, keepdims=True)

    tile(nt - 1, True)

    def alive():
        return jnp.max(c_ref[...]) > SB_DEAD

    def cond(state):
        return (state[0] < nt - 1) & state[1]

    def body(state):
        tile(nt - 2 - state[0], False)
        return state[0] + 1, alive()

    lax.while_loop(cond, body, (jnp.int32(0), alive()))

    acc = acc_ref[...]
    out = jnp.zeros((tq, D_SB), F32)
    for h in range(SB_HEADS):
        out = out + jnp.where(lane_h == h, acc[h * tq:(h + 1) * tq], 0.0)
    o_ref[...] = out.astype(BF)


def _sb_prompt(q, k, v, *, b, t, tq=128, tk=256):
    n = b * t
    nq = t // tq
    return pl.pallas_call(
        functools.partial(_sb_kernel, tq=tq, tk=tk),
        out_shape=jax.ShapeDtypeStruct((n, D_SB), BF),
        grid=(b, nq),
        in_specs=[pl.BlockSpec((tq, D_SB), lambda bi, i: (bi * nq + i, 0)),
                  pl.BlockSpec((t, D_SB), lambda bi, i: (bi, 0)),
                  pl.BlockSpec((t, D_SB), lambda bi, i: (bi, 0))],
        out_specs=pl.BlockSpec((tq, D_SB), lambda bi, i: (bi * nq + i, 0)),
        scratch_shapes=[pltpu.VMEM((SB_HEADS * tq, 1), F32), pltpu.VMEM((SB_HEADS * tq, D_SB), F32)],
        compiler_params=_cparams(("parallel", "parallel")),
        name="sb_prompt",
    )(q, k, v)


def _ret_kernel(q_ref, k_ref, v_ref, g_ref, dec_ref, qd_ref, kd_ref, gc_ref, o_ref, st_ref, s_scr, *, c):
    ci = pl.program_id(1)

    @pl.when(ci == 0)
    def _():
        s_scr[...] = jnp.zeros_like(s_scr)

    q = q_ref[...]
    k = k_ref[...]
    v = v_ref[...]
    lane_h = lax.broadcasted_iota(I32, (c, D_RET), 1) // HEAD_DIM
    o = jnp.zeros((c, D_RET), F32)
    for h in range(RET_HEADS):
        kh = jnp.where(lane_h == h, k, jnp.zeros_like(k))
        vh = jnp.where(lane_h == h, v, jnp.zeros_like(v))
        s = _dot_nt(q, kh) * dec_ref[h]
        o = o + _dot(s.astype(BF), vh)
    st = s_scr[...]
    shi, slo = _split2(st)
    o = o + (_dot(q, shi) + _dot(q, slo)) * qd_ref[...]

    kd = (k.astype(F32) * kd_ref[...]).astype(BF)
    ktv = _dot_tn(kd, v)
    r = lax.broadcasted_iota(I32, (D_RET, D_RET), 0) // HEAD_DIM
    cc = lax.broadcasted_iota(I32, (D_RET, D_RET), 1) // HEAD_DIM
    new_st = st * gc_ref[...] + jnp.where(r == cc, ktv, 0.0)
    s_scr[...] = new_st
    st_ref[...] = new_st

    seg = _seg_mean_mat(D_RET)
    mu = _dot_x2(o, seg)
    d = o - mu
    var = _dot_x2(d * d, seg)
    y = d * lax.rsqrt(var + EPS)
    o_ref[...] = (y * jax.nn.silu(g_ref[...])).astype(BF)


def _ret_tables(c):
    log_gamma = np.log1p(-np.exp2(-5.0 - np.arange(RET_HEADS, dtype=np.float64)))
    idx = np.arange(c, dtype=np.float64)
    diff = idx[:, None] - idx[None, :]
    dec = np.where(diff >= 0, np.exp(np.maximum(diff, 0.0)[None] * log_gamma[:, None, None]), 0.0)
    lane_lg = np.repeat(log_gamma, HEAD_DIM)
    qd = np.exp((idx[:, None] + 1.0) * lane_lg[None, :])
    kd = np.exp((c - 1.0 - idx)[:, None] * lane_lg[None, :])
    gc = np.exp(c * lane_lg)[None, :]
    f = lambda a: jnp.asarray(a, F32)
    return f(dec), f(qd), f(kd), f(gc)


def _ret_prompt(q, k, v, g, *, b, t, c=256):
    n = b * t
    nc = t // c
    dec, qd, kd, gc = _ret_tables(c)
    tok = pl.BlockSpec((c, D_RET), lambda bi, i: (bi * nc + i, 0))
    o, st = pl.pallas_call(
        functools.partial(_ret_kernel, c=c),
        out_shape=[jax.ShapeDtypeStruct((n, D_RET), BF), jax.ShapeDtypeStruct((b, D_RET, D_RET), F32)],
        grid=(b, nc),
        in_specs=[tok, tok, tok, tok, _const_spec(dec.shape), _const_spec(qd.shape), _const_spec(kd.shape),
                  _const_spec(gc.shape)],
        out_specs=[tok, pl.BlockSpec((None, D_RET, D_RET), lambda bi, i: (bi, 0, 0))],
        scratch_shapes=[pltpu.VMEM((D_RET, D_RET), F32)],
        compiler_params=_cparams(("parallel", "arbitrary")),
        name="ret_prompt",
    )(q, k, v, g, dec, qd, kd, gc)
    return o, st


def _compress_rows(x, pos, phi_hi, phi_lo):
    r = x.shape[0]
    xs = (x.reshape(r // BLK_CMP, BLK_CMP, LANES) * pos[None]).sum(axis=1)
    hi, lo = _split2(xs)
    return _dot(hi, phi_hi) + _dot(lo, phi_hi) + _dot(hi, phi_lo)


def _compress_kernel(k_ref, v_ref, pk_ref, pv_ref, wk_ref, wv_ref, ko_ref, vo_ref):
    wk_hi, wk_lo = _split2(wk_ref[...])
    wv_hi, wv_lo = _split2(wv_ref[...])
    ko_ref[...] = _compress_rows(k_ref[...], pk_ref[...], wk_hi, wk_lo).astype(BF)
    vo_ref[...] = _compress_rows(v_ref[...], pv_ref[...], wv_hi, wv_lo).astype(BF)


def _compress_prompt(ck, cv, pk, pv, wk, wv, *, rows=1024):
    n = ck.shape[0]
    tok = pl.BlockSpec((rows, LANES), lambda i: (i, 0))
    out = pl.BlockSpec((rows // BLK_CMP, LANES), lambda i: (i, 0))
    return pl.pallas_call(
        _compress_kernel,
        out_shape=[jax.ShapeDtypeStruct((n // BLK_CMP, LANES), BF)] * 2,
        grid=(n // rows,),
        in_specs=[tok, tok, _const_spec(pk.shape), _const_spec(pv.shape), _const_spec(wk.shape),
                  _const_spec(wv.shape)],
        out_specs=[out, out],
        compiler_params=_cparams(("parallel",)),
        name="compress_prompt",
    )(ck, cv, pk, pv, wk, wv)


def _stack_heads_q(q, extra):
    tq = q.shape[0]
    lane = lax.broadcasted_iota(I32, (tq, LANES), 1)
    rows = []
    for h in range(NSA_HEADS):
        g = h // NSA_REP
        blk = q[:, (h // 2) * LANES:(h // 2 + 1) * LANES]
        src_half = h % 2
        if src_half != g:
            blk32 = pltpu.roll(blk.astype(F32), HEAD_DIM, 1).astype(BF)
        else:
            blk32 = blk
        keep = (lane // HEAD_DIM) == g
        rows.append(jnp.where(keep, blk32, jnp.zeros_like(blk32)))
    qs = jnp.concatenate(rows, axis=0)
    if extra is not None:
        qs = jnp.concatenate([qs, extra], axis=1)
    return qs


def _unstack_heads(acc, tq):
    lane = lax.broadcasted_iota(I32, (tq, LANES), 1)
    cols = []
    for c in range(NSA_HEADS // 2):
        g = (2 * c) // NSA_REP
        a = acc[(2 * c) * tq:(2 * c + 1) * tq]
        b = acc[(2 * c + 1) * tq:(2 * c + 2) * tq]
        if g == 0:
            cols.append(jnp.where(lane < HEAD_DIM, a, pltpu.roll(b, HEAD_DIM, 1)))
        else:
            cols.append(jnp.where(lane < HEAD_DIM, pltpu.roll(a, HEAD_DIM, 1), b))
    return jnp.concatenate(cols, axis=1)


def _gate_expand(gates, branch):
    r = lax.broadcasted_iota(I32, (LANES, D_NSA), 0)
    c = lax.broadcasted_iota(I32, (LANES, D_NSA), 1)
    e = (r == (c // HEAD_DIM) * 3 + branch).astype(BF)
    return _dot_x2(gates, e)


def _nsa_kernel(q_ref, gate_ref, kc_ref, vc_ref, sk_ref, sv_ref, wk_ref, wv_ref, o_ref,
                k2_scr, svt_scr, wvt_scr, imp_scr, m_scr, l_scr, acc_scr, *, tq, tk, tw, t):
    i = pl.program_id(1)
    nc = t // BLK_CMP
    ns = t // BLK_SEL
    m = NSA_HEADS * tq

    @pl.when(i == 0)
    def _():
        rows = 512
        for r0 in range(0, t, rows):
            s_idx = r0 + lax.broadcasted_iota(I32, (rows, LANES), 0)
            c_idx = lax.broadcasted_iota(I32, (rows, LANES), 1)
            e = (s_idx // BLK_SEL == c_idx).astype(BF)
            k2_scr[r0:r0 + rows, :] = jnp.concatenate([sk_ref[r0:r0 + rows, :], e], axis=1)
        for c in range(t // tk):
            svt_scr[c] = jnp.transpose(sv_ref[c * tk:(c + 1) * tk, :].astype(F32)).astype(BF)
        for c in range(t // tw):
            wvt_scr[c] = jnp.transpose(wv_ref[c * tw:(c + 1) * tw, :].astype(F32)).astype(BF)

    q = q_ref[...]
    qs = _stack_heads_q(q, None)

    kc = kc_ref[...]
    vct = jnp.transpose(vc_ref[...].astype(F32)).astype(BF)
    n_idx = lax.broadcasted_iota(I32, (nc, m), 0)
    t_idx = i * tq + lax.broadcasted_iota(I32, (nc, m), 1) % tq
    valid = (n_idx + 1) * BLK_CMP - 1 <= t_idx
    st = jnp.where(valid, _dot_nt(kc, qs), NEG)
    mx = jnp.max(st, axis=0, keepdims=True)
    e = jnp.where(valid, jnp.exp(st - mx), 0.0)
    den = jnp.sum(e, axis=0, keepdims=True)
    p = e / jnp.where(den > 0.0, den, 1.0)
    o_cmp = _dot(vct, p.astype(BF))
    imp = []
    for g in range(NSA_KV):
        acc_g = p[:, g * NSA_REP * tq:(g * NSA_REP + 1) * tq]
        for h in range(g * NSA_REP + 1, (g + 1) * NSA_REP):
            acc_g = acc_g + p[:, h * tq:(h + 1) * tq]
        imp.append(acc_g)

    pair = (lax.broadcasted_iota(I32, (ns, nc), 1) // (BLK_SEL // BLK_CMP)
            == lax.broadcasted_iota(I32, (ns, nc), 0)).astype(BF)
    blk = lax.broadcasted_iota(I32, (ns, tq), 0)
    tb = i * tq + lax.broadcasted_iota(I32, (ns, tq), 1)
    forced = (blk == 0) | (blk == tb // BLK_SEL)
    future = blk * BLK_SEL > tb
    eye = (lax.broadcasted_iota(I32, (tq, tq), 0) == lax.broadcasted_iota(I32, (tq, tq), 1)).astype(BF)
    pens = []
    for g in range(NSA_KV):
        a, b, c = _split3(imp[g])
        v = _dot(pair, a) + _dot(pair, b) + _dot(pair, c)
        v = jnp.where(forced, FORCE_SCORE, jnp.where(future, -1.0, v))
        imp_scr[g] = v

        def body(j, cnt, g=g, v=v):
            vj = imp_scr[g, pl.ds(j, 1), :]
            ge = (vj >= v).astype(I32)
            gt = (vj > v).astype(I32)
            return cnt + jnp.where(blk > j, ge, gt)

        cnt = lax.fori_loop(0, ns, body, jnp.zeros((ns, tq), I32), unroll=8)
        sel_t = (cnt < min(TOP_N, ns)).astype(BF)
        pen = _dot_nt(eye, sel_t)
        pen = ((pen - 1.0) * (-NEG)).astype(BF)
        if ns < LANES:
            pen = jnp.concatenate([pen, jnp.zeros((tq, LANES - ns), BF)], axis=1)
        pens.append(pen)
    pen_rows = jnp.concatenate([pens[h // NSA_REP] for h in range(NSA_HEADS)], axis=0)
    qs2 = jnp.concatenate([qs, pen_rows], axis=1)

    q_pos = i * tq + lax.broadcasted_iota(I32, (1, m), 1) % tq

    def online(scores, vals_t, mask):
        if mask is not None:
            scores = jnp.where(mask, scores, NEG)
        m_old = m_scr[...]
        m_new = jnp.maximum(m_old, jnp.max(scores, axis=0, keepdims=True))
        alpha = jnp.exp(m_old - m_new)
        p = jnp.exp(scores - m_new)
        l_scr[...] = alpha * l_scr[...] + jnp.sum(p, axis=0, keepdims=True)
        acc_scr[...] = alpha * acc_scr[...] + _dot(vals_t, p.astype(BF))
        m_scr[...] = m_new

    def reset():
        m_scr[...] = jnp.full_like(m_scr, NEG)
        l_scr[...] = jnp.zeros_like(l_scr)
        acc_scr[...] = jnp.zeros_like(acc_scr)

    def result():
        return acc_scr[...] / l_scr[...]

    reset()
    nt = (i * tq) // tk + 1
    row_k = lax.broadcasted_iota(I32, (tk, 1), 0)

    def slc_body(j, carry):
        off = pl.multiple_of(j * tk, tk)
        online(_dot_nt(k2_scr[pl.ds(off, tk), :], qs2), svt_scr[j], None)
        return carry

    lax.fori_loop(0, nt - 1, slc_body, 0)
    off = pl.multiple_of((nt - 1) * tk, tk)
    online(_dot_nt(k2_scr[pl.ds(off, tk), :], qs2), svt_scr[nt - 1], (off + row_k) <= q_pos)
    o_slc = result()

    row_w = lax.broadcasted_iota(I32, (tw, 1), 0)
    n_band = (WINDOW + tq + tw - 1) // tw
    first = jnp.maximum(i * tq - WINDOW, 0) // tw
    s_w = []
    for c in range(n_band):
        off = pl.multiple_of((first + c) * tw, tw)
        k_pos = off + row_w
        s = _dot_nt(wk_ref[pl.ds(off, tw), :], qs)
        s_w.append(jnp.where((k_pos <= q_pos) & (q_pos - k_pos <= WINDOW), s, NEG))
    m_w = s_w[0].max(axis=0, keepdims=True)
    for s in s_w[1:]:
        m_w = jnp.maximum(m_w, s.max(axis=0, keepdims=True))
    l_w = jnp.zeros((1, m), F32)
    o_win = jnp.zeros((LANES, m), F32)
    for c, s in enumerate(s_w):
        p = jnp.exp(s - m_w)
        l_w = l_w + p.sum(axis=0, keepdims=True)
        o_win = o_win + _dot(wvt_scr[first + c], p.astype(BF))
    o_win = o_win / l_w

    gt = jnp.transpose(gate_ref[...])

    def gate_row(branch):
        return jnp.concatenate([gt[h * 3 + branch:h * 3 + branch + 1, :] for h in range(NSA_HEADS)], axis=1)

    mix = gate_row(0) * o_cmp + gate_row(1) * o_slc + gate_row(2) * o_win
    stacked = jnp.concatenate([jnp.transpose(mix[:, h * tq:(h + 1) * tq]) for h in range(NSA_HEADS)], axis=0)
    o_ref[...] = _unstack_heads(stacked, tq).astype(BF)


def _nsa_prompt(nq, gates, kc, vc, sk, sv, wk, wv, *, b, t, tq=128, tk=512, tw=128):
    assert t % tk == 0 and (WINDOW + tq + tw - 1) // tw <= t // tw
    n = b * t
    nqb = t // tq
    nc = t // BLK_CMP
    ns = t // BLK_SEL
    m = NSA_HEADS * tq
    tok = lambda w: pl.BlockSpec((tq, w), lambda bi, i: (bi * nqb + i, 0))
    seq = lambda rows, w: pl.BlockSpec((rows, w), lambda bi, i: (bi, 0))
    return pl.pallas_call(
        functools.partial(_nsa_kernel, tq=tq, tk=tk, tw=tw, t=t),
        out_shape=jax.ShapeDtypeStruct((n, D_NSA), BF),
        grid=(b, nqb),
        in_specs=[tok(D_NSA), tok(LANES), seq(nc, LANES), seq(nc, LANES),
                  seq(t, LANES), seq(t, LANES), seq(t, LANES), seq(t, LANES)],
        out_specs=tok(D_NSA),
        scratch_shapes=[pltpu.VMEM((t, 2 * LANES), BF), pltpu.VMEM((t // tk, LANES, tk), BF),
                        pltpu.VMEM((t // tw, LANES, tw), BF), pltpu.VMEM((NSA_KV, ns, tq), F32),
                        pltpu.VMEM((1, m), F32), pltpu.VMEM((1, m), F32), pltpu.VMEM((LANES, m), F32)],
        compiler_params=_cparams(("parallel", "arbitrary")),
        name="nsa_prompt",
    )(nq, gates, kc, vc, sk, sv, wk, wv)


def _out_mlp_kernel(x_ref, osb_ref, oret_ref, onsa_ref, wo_ref, g2_ref, wu_ref, wd_ref, y_ref, *, ff_chunk):
    h = (x_ref[...] + _dot(osb_ref[...].astype(BF), wo_ref[0:D_SB, :])
         + _dot(oret_ref[...].astype(BF), wo_ref[D_SB:D_SB + D_RET, :])
         + _dot(onsa_ref[...].astype(BF), wo_ref[D_SB + D_RET:, :]))
    ms = jnp.mean(h * h, axis=-1, keepdims=True)
    hn = (h * lax.rsqrt(ms + EPS) * g2_ref[...]).astype(BF)
    mlp = None
    d_ff = wu_ref.shape[1]
    for c0 in range(0, d_ff, ff_chunk):
        u = jnp.maximum(_dot(hn, wu_ref[:, c0:c0 + ff_chunk]), 0.0)
        part = _dot((u * u).astype(BF), wd_ref[c0:c0 + ff_chunk, :])
        mlp = part if mlp is None else mlp + part
    y_ref[...] = h + mlp


def _out_mlp(x2d, osb, oret, onsa, wo, g2, wu, wd, *, tm, ff_chunk=1024):
    n, d = x2d.shape
    tok = lambda w: pl.BlockSpec((tm, w), lambda i: (i, 0))
    return pl.pallas_call(
        functools.partial(_out_mlp_kernel, ff_chunk=ff_chunk),
        out_shape=jax.ShapeDtypeStruct((n, d), F32),
        grid=(n // tm,),
        in_specs=[tok(d), tok(D_SB), tok(D_RET), tok(D_NSA), _const_spec(wo.shape), _const_spec((1, d)),
                  _const_spec(wu.shape), _const_spec(wd.shape)],
        out_specs=tok(d),
        compiler_params=_cparams(("parallel",)),
        name="out_mlp",
    )(x2d, osb, oret, onsa, wo, g2, wu, wd)


PAGES_PER_STEP = 8
DEC_ROWS = 16


def _page_spec(layer, width, page_of_step, rows=None, row_blk=None):
    def imap(bi, p, pt, *rest):
        return (layer, pt[bi, page_of_step(p)], 0, 0)
    return pl.BlockSpec((None, None, rows, width), imap)


def _sb_dec_kernel(pt_ref, q_ref, *refs, npp, last):
    k_refs, v_refs = refs[:npp], refs[npp:2 * npp]
    o_ref, c_scr, acc_scr = refs[2 * npp:]
    p = pl.program_id(1)
    rows = DEC_ROWS
    pg = k_refs[0].shape[0]

    @pl.when(p == 0)
    def _():
        c_scr[...] = jnp.zeros_like(c_scr)
        acc_scr[...] = jnp.zeros_like(acc_scr)

    lane_h = lax.broadcasted_iota(I32, (rows, D_SB), 1) // HEAD_DIM
    row = lax.broadcasted_iota(I32, (rows, D_SB), 0)
    q = jnp.broadcast_to(q_ref[...].astype(F32), (rows, D_SB))
    qs = jnp.where(lane_h == row, q, 0.0).astype(BF)
    tri = (lax.broadcasted_iota(I32, (pg, pg), 0) > lax.broadcasted_iota(I32, (pg, pg), 1)).astype(BF)
    c = c_scr[...]
    acc = acc_scr[...]
    for kk in range(npp):
        z = _dot_nt(qs, k_refs[kk][...].astype(BF))
        l1m = -_softplus(z)
        after = c + _dot_x2(l1m, tri)
        w = jnp.exp(z + l1m + after)
        acc = acc + _dot(w.astype(BF), v_refs[kk][...].astype(BF))
        c = c + jnp.sum(l1m, axis=1, keepdims=True)
    c_scr[...] = c
    acc_scr[...] = acc

    @pl.when(p == last)
    def _():
        o_ref[...] = jnp.sum(jnp.where(lane_h == row, acc, 0.0), axis=0, keepdims=True)


def _sb_decode(layer, page_table, q, cache_k, cache_v):
    nb, npg = page_table.shape
    npp = math.gcd(PAGES_PER_STEP, npg)
    steps = npg // npp
    pg = cache_k.shape[2]
    specs_k = [_page_spec(layer, D_SB, (lambda p, kk=kk: npg - 1 - (p * npp + kk)), rows=pg) for kk in range(npp)]
    specs_v = [_page_spec(layer, D_SB, (lambda p, kk=kk: npg - 1 - (p * npp + kk)), rows=pg) for kk in range(npp)]
    row = pl.BlockSpec((None, 1, D_SB), lambda bi, p, pt: (bi, 0, 0))
    return pl.pallas_call(
        functools.partial(_sb_dec_kernel, npp=npp, last=steps - 1),
        out_shape=jax.ShapeDtypeStruct((nb, 1, D_SB), F32),
        grid_spec=pltpu.PrefetchScalarGridSpec(
            num_scalar_prefetch=1, grid=(nb, steps),
            in_specs=[row] + specs_k + specs_v, out_specs=row,
            scratch_shapes=[pltpu.VMEM((DEC_ROWS, 1), F32), pltpu.VMEM((DEC_ROWS, D_SB), F32)]),
        compiler_params=_cparams(("parallel", "arbitrary")),
        name="sb_decode",
    )(page_table, q, *([cache_k] * npp), *([cache_v] * npp))


def _cmp_dec_kernel(pt_ref, kn_ref, vn_ref, pk_ref, pv_ref, wk_ref, wv_ref, *refs, npp, last, nblk):
    k_refs, v_refs = refs[:npp], refs[npp:2 * npp]
    ko_ref, vo_ref = refs[2 * npp:]
    p = pl.program_id(1)
    wk_hi, wk_lo = _split2(wk_ref[...])
    wv_hi, wv_lo = _split2(wv_ref[...])
    pg = k_refs[0].shape[0]
    per = npp * pg // BLK_CMP
    r0 = pl.multiple_of(p * per, per)
    ko_ref[pl.ds(r0, per), :] = _compress_rows(
        jnp.concatenate([r[...] for r in k_refs], axis=0), pk_ref[...], wk_hi, wk_lo)
    vo_ref[pl.ds(r0, per), :] = _compress_rows(
        jnp.concatenate([r[...] for r in v_refs], axis=0), pv_ref[...], wv_hi, wv_lo)

    @pl.when(p == last)
    def _():
        first = lax.broadcasted_iota(I32, (DEC_ROWS, LANES), 0) == 0
        xk = jnp.where(first, kn_ref[...] * pk_ref[0:1, :], 0.0)
        xv = jnp.where(first, vn_ref[...] * pv_ref[0:1, :], 0.0)
        a, b = _split2(xk)
        ko_ref[nblk:nblk + DEC_ROWS, :] = _dot(a, wk_hi) + _dot(b, wk_hi) + _dot(a, wk_lo)
        a, b = _split2(xv)
        vo_ref[nblk:nblk + DEC_ROWS, :] = _dot(a, wv_hi) + _dot(b, wv_hi) + _dot(a, wv_lo)


def _cmp_decode(layer, page_table, ck_new, cv_new, cache_k, cache_v, pk, pv, wk, wv):
    nb, npg = page_table.shape
    npp = math.gcd(PAGES_PER_STEP, npg)
    steps = npg // npp
    pg = cache_k.shape[2]
    nblk = npg * pg // BLK_CMP
    specs = [_page_spec(layer, D_KV, (lambda p, kk=kk: p * npp + kk), rows=pg) for kk in range(npp)]
    row = pl.BlockSpec((None, 1, D_KV), lambda bi, p, pt: (bi, 0, 0))
    cst = lambda a: pl.BlockSpec(a.shape, lambda bi, p, pt: (0,) * a.ndim)
    out = pl.BlockSpec((None, nblk + DEC_ROWS, LANES), lambda bi, p, pt: (bi, 0, 0))
    return pl.pallas_call(
        functools.partial(_cmp_dec_kernel, npp=npp, last=steps - 1, nblk=nblk),
        out_shape=[jax.ShapeDtypeStruct((nb, nblk + DEC_ROWS, LANES), F32)] * 2,
        grid_spec=pltpu.PrefetchScalarGridSpec(
            num_scalar_prefetch=1, grid=(nb, steps),
            in_specs=[row, row, cst(pk), cst(pv), cst(wk), cst(wv)] + specs + specs,
            out_specs=[out, out]),
        compiler_params=_cparams(("parallel", "arbitrary")),
        name="cmp_decode",
    )(page_table, ck_new, cv_new, pk, pv, wk, wv, *([cache_k] * npp), *([cache_v] * npp))


def _stack_heads_q_row(q):
    qf = q.astype(F32)
    row = lax.broadcasted_iota(I32, (DEC_ROWS, LANES), 0)
    lane_g = lax.broadcasted_iota(I32, (DEC_ROWS, LANES), 1) // HEAD_DIM
    out = jnp.zeros((DEC_ROWS, LANES), F32)
    for h in range(NSA_HEADS):
        g = h // NSA_REP
        blk = jnp.broadcast_to(qf[:, (h // 2) * LANES:(h // 2 + 1) * LANES], (DEC_ROWS, LANES))
        if h % 2 != g:
            blk = pltpu.roll(blk, HEAD_DIM, 1)
        out = jnp.where((row == h) & (lane_g == g), blk, out)
    return out.astype(BF)


def _unstack_heads_row(acc):
    lane = lax.broadcasted_iota(I32, (1, LANES), 1)
    accr = pltpu.roll(acc, HEAD_DIM, 1)
    cols = []
    for c in range(NSA_HEADS // 2):
        g = (2 * c) // NSA_REP
        if g == 0:
            cols.append(jnp.where(lane < HEAD_DIM, acc[2 * c:2 * c + 1], accr[2 * c + 1:2 * c + 2]))
        else:
            cols.append(jnp.where(lane < HEAD_DIM, accr[2 * c:2 * c + 1], acc[2 * c + 1:2 * c + 2]))
    return jnp.concatenate(cols, axis=1)


def _sel_dec_kernel(q_ref, kc_ref, vc_ref, ocmp_ref, idx_ref, imp_scr, *, q_pos, nblk, ns_pad):
    nrow = kc_ref.shape[0]
    qs = _stack_heads_q_row(q_ref[...])
    qs = jnp.concatenate([qs, jnp.zeros((LANES - DEC_ROWS, LANES), BF)], axis=0)
    kc = kc_ref[...].astype(BF)
    st = _dot_nt(kc, qs)
    n_idx = lax.broadcasted_iota(I32, (nrow, LANES), 0)
    valid = (n_idx + 1) * BLK_CMP - 1 <= q_pos
    st = jnp.where(valid, st, NEG)
    mx = jnp.max(st, axis=0, keepdims=True)
    e = jnp.where(valid, jnp.exp(st - mx), 0.0)
    den = jnp.sum(e, axis=0, keepdims=True)
    p = e / jnp.where(den > 0.0, den, 1.0)
    ocmp_ref[...] = _dot_tn(p.astype(BF), vc_ref[...].astype(BF))[0:NSA_HEADS]

    grp = (lax.broadcasted_iota(I32, (LANES, LANES), 0) // NSA_REP
           == lax.broadcasted_iota(I32, (LANES, LANES), 1)).astype(BF)
    a, b, c = _split3(p)
    impc = _dot(a, grp) + _dot(b, grp) + _dot(c, grp)
    pair = (lax.broadcasted_iota(I32, (ns_pad, nrow), 1) // (BLK_SEL // BLK_CMP)
            == lax.broadcasted_iota(I32, (ns_pad, nrow), 0)).astype(BF)
    a, b, c = _split3(impc)
    v = _dot(pair, a) + _dot(pair, b) + _dot(pair, c)
    ns = (nblk * BLK_CMP + 1 + BLK_SEL - 1) // BLK_SEL
    blk = lax.broadcasted_iota(I32, (ns_pad, LANES), 0)
    forced = (blk == 0) | (blk == q_pos // BLK_SEL)
    future = blk * BLK_SEL > q_pos
    v = jnp.where(forced, FORCE_SCORE, jnp.where(future, -1.0, v))
    v = jnp.where(blk < ns, v, -2.0)
    imp_scr[...] = v

    def body(j, cnt):
        vj = imp_scr[pl.ds(j, 1), :]
        ge = (vj >= v).astype(I32)
        gt = (vj > v).astype(I32)
        return cnt + jnp.where(blk > j, ge, gt)

    cnt = lax.fori_loop(0, ns_pad, body, jnp.zeros((ns_pad, LANES), I32))
    blk_f = blk.astype(F32)
    rows = [jnp.sum(jnp.where(cnt == r, blk_f, 0.0), axis=0, keepdims=True) for r in range(TOP_N)]
    idx_ref[...] = jnp.concatenate(rows, axis=0).astype(I32)


def _sel_decode(nq, kc, vc, *, q_pos, nblk):
    nb = nq.shape[0]
    nrow = kc.shape[1]
    ns_pad = ((nrow // 2 + 7) // 8) * 8
    return pl.pallas_call(
        functools.partial(_sel_dec_kernel, q_pos=q_pos, nblk=nblk, ns_pad=ns_pad),
        out_shape=[jax.ShapeDtypeStruct((nb, NSA_HEADS, LANES), F32), jax.ShapeDtypeStruct((nb, TOP_N, LANES), I32)],
        grid=(nb,),
        in_specs=[pl.BlockSpec((None, 1, D_NSA), lambda bi: (bi, 0, 0)),
                  pl.BlockSpec((None, nrow, LANES), lambda bi: (bi, 0, 0)),
                  pl.BlockSpec((None, nrow, LANES), lambda bi: (bi, 0, 0))],
        out_specs=[pl.BlockSpec((None, NSA_HEADS, LANES), lambda bi: (bi, 0, 0)),
                   pl.BlockSpec((None, TOP_N, LANES), lambda bi: (bi, 0, 0))],
        scratch_shapes=[pltpu.VMEM((ns_pad, LANES), F32)],
        compiler_params=_cparams(("parallel",)),
        name="sel_decode",
    )(nq, kc, vc)


def _nsa_dec_kernel(pt_ref, idx_ref, q_ref, gate_ref, ocmp_ref, skn_ref, svn_ref, wkn_ref, wvn_ref,
                    wk_ref, wv_ref, ck_hbm, cv_hbm, o_ref, wko_ref, wvo_ref, kbuf, vbuf, sem,
                    *, layer, ncache, per):
    nsel = NSA_KV * TOP_N
    bi = pl.program_id(0)

    def block_copies(j):
        blk = jnp.minimum(idx_ref[bi, j], ncache - 1)
        page = pt_ref[bi, blk // per]
        r0 = pl.multiple_of((blk % per) * BLK_SEL, BLK_SEL)
        return (pltpu.make_async_copy(ck_hbm.at[layer, page, pl.ds(r0, BLK_SEL), :], kbuf.at[j], sem.at[0, j]),
                pltpu.make_async_copy(cv_hbm.at[layer, page, pl.ds(r0, BLK_SEL), :], vbuf.at[j], sem.at[1, j]))

    for j in range(nsel):
        for cp in block_copies(j):
            cp.start()

    qs = _stack_heads_q_row(q_ref[...])
    qf = qs.astype(F32)
    row_g = lax.broadcasted_iota(I32, (DEC_ROWS, LANES), 0) // NSA_REP

    def attend(s, v, s_new, v_new):
        mx = jnp.maximum(s_new, jnp.max(s, axis=1, keepdims=True))
        p_new = jnp.exp(s_new - mx)
        p = jnp.exp(s - mx)
        den = p_new + jnp.sum(p, axis=1, keepdims=True)
        return (p_new * v_new + _dot(p.astype(BF), v)) / den

    bf_row = lambda r: r[...].astype(BF).astype(F32)
    s_wn = jnp.sum(qf * bf_row(wkn_ref), axis=1, keepdims=True)
    o_win = attend(_dot_nt(qs, wk_ref[...].astype(BF)), wv_ref[...].astype(BF), s_wn, bf_row(wvn_ref))

    wb = wk_ref.shape[0]
    wko_ref[0:wb - 1, :] = wk_ref[1:wb, :]
    wko_ref[wb - 1:wb, :] = wkn_ref[...]
    wvo_ref[0:wb - 1, :] = wv_ref[1:wb, :]
    wvo_ref[wb - 1:wb, :] = wvn_ref[...]

    for j in range(nsel):
        for cp in block_copies(j):
            cp.wait()

    s_new = jnp.sum(qf * bf_row(skn_ref), axis=1, keepdims=True)
    nkeys = TOP_N * BLK_SEL
    col_blk = lax.broadcasted_iota(I32, (DEC_ROWS, nkeys), 1) // BLK_SEL
    o_slc = jnp.zeros((DEC_ROWS, LANES), F32)
    for g in range(NSA_KV):
        pen = jnp.zeros((DEC_ROWS, nkeys), F32)
        for r in range(TOP_N):
            pen = jnp.where(col_blk == r, jnp.where(idx_ref[bi, g * TOP_N + r] < ncache, 0.0, NEG), pen)
        kcat = kbuf[g * TOP_N:(g + 1) * TOP_N].reshape(nkeys, D_KV).astype(BF)
        vcat = vbuf[g * TOP_N:(g + 1) * TOP_N].reshape(nkeys, D_KV).astype(BF)
        o_g = attend(_dot_nt(qs, kcat) + pen, vcat, s_new, bf_row(svn_ref))
        o_slc = jnp.where(row_g == g, o_g, o_slc)

    gates = jnp.broadcast_to(gate_ref[...], (DEC_ROWS, LANES))
    out = (_gate_expand(gates, 0)[0:1] * _unstack_heads_row(ocmp_ref[...])
           + _gate_expand(gates, 1)[0:1] * _unstack_heads_row(o_slc)
           + _gate_expand(gates, 2)[0:1] * _unstack_heads_row(o_win))
    o_ref[...] = out


def _nsa_decode(layer, page_table, idx, nq, gates, ocmp, sk_new, sv_new, wk_new, wv_new, win_k, win_v,
                cache_k, cache_v):
    nb, npg = page_table.shape
    pg = cache_k.shape[2]
    per = pg // BLK_SEL
    ncache = npg * per
    wb = win_k.shape[2]
    nsel = NSA_KV * TOP_N

    row = lambda w: pl.BlockSpec((None, 1, w), lambda bi, pt, ix: (bi, 0, 0))
    win = pl.BlockSpec((None, None, wb, D_KV), lambda bi, pt, ix: (layer, bi, 0, 0))
    wout = pl.BlockSpec((None, wb, D_KV), lambda bi, pt, ix: (bi, 0, 0))
    hbm = pl.BlockSpec(memory_space=pl.ANY)
    return pl.pallas_call(
        functools.partial(_nsa_dec_kernel, layer=layer, ncache=ncache, per=per),
        out_shape=[jax.ShapeDtypeStruct((nb, 1, D_NSA), F32), jax.ShapeDtypeStruct((nb, wb, D_KV), F32),
                   jax.ShapeDtypeStruct((nb, wb, D_KV), F32)],
        grid_spec=pltpu.PrefetchScalarGridSpec(
            num_scalar_prefetch=2, grid=(nb,),
            in_specs=[row(D_NSA), row(LANES), pl.BlockSpec((None, NSA_HEADS, LANES), lambda bi, pt, ix: (bi, 0, 0)),
                      row(D_KV), row(D_KV), row(D_KV), row(D_KV), win, win, hbm, hbm],
            out_specs=[row(D_NSA), wout, wout],
            scratch_shapes=[pltpu.VMEM((nsel, BLK_SEL, D_KV), F32), pltpu.VMEM((nsel, BLK_SEL, D_KV), F32),
                            pltpu.SemaphoreType.DMA((2, nsel))]),
        compiler_params=_cparams(("arbitrary",)),
        name="nsa_decode",
    )(page_table, idx, nq, gates, ocmp, sk_new, sv_new, wk_new, wv_new, win_k, win_v, cache_k, cache_v)


def _ret_dec_kernel(q_ref, k_ref, kcol_ref, v_ref, g_ref, gcol_ref, grow_ref, st_ref, o_ref, sto_ref):
    rows = DEC_ROWS
    st = st_ref[...]
    q = jnp.broadcast_to(q_ref[...], (rows, D_RET))
    v = jnp.broadcast_to(v_ref[...], (rows, D_RET))
    lane_h = lax.broadcasted_iota(I32, (rows, D_RET), 1) // HEAD_DIM
    row = lax.broadcasted_iota(I32, (rows, D_RET), 0)
    own = lane_h == row
    qs = jnp.where(own, q, 0.0).astype(BF)
    spread = (lax.broadcasted_iota(I32, (HEAD_DIM, D_RET), 1) % HEAD_DIM
              == lax.broadcasted_iota(I32, (HEAD_DIM, D_RET), 0)).astype(BF)
    shi, slo = _split2(st)
    cross = _dot(qs, shi) + _dot(qs, slo)
    o_cross = jnp.sum(jnp.where(own, _dot_x2(cross, spread), 0.0), axis=0, keepdims=True) * grow_ref[...]
    seg = _seg_mean_mat(D_RET)
    qk = q * jnp.broadcast_to(k_ref[...], (rows, D_RET))
    o = o_cross + (_dot_x2(qk, seg) * float(HEAD_DIM)) * v

    v4 = _dot_nt(jnp.where(own, v, 0.0).astype(BF), spread)
    pick = (lax.broadcasted_iota(I32, (D_RET, rows), 0) // HEAD_DIM
            == lax.broadcasted_iota(I32, (D_RET, rows), 1)).astype(BF)
    vexp = _dot(pick, v4.astype(BF))
    sto_ref[...] = st * gcol_ref[...] + kcol_ref[...] * vexp

    mu = _dot_x2(o, seg)
    d = o - mu
    var = _dot_x2(d * d, seg)
    y = d * lax.rsqrt(var + EPS)
    o_ref[...] = y[0:1] * jax.nn.silu(g_ref[...])


def _ret_decode(layer, q, k, v, g, state):
    nb = q.shape[0]
    log_gamma = np.log1p(-np.exp2(-5.0 - np.arange(RET_HEADS, dtype=np.float64)))
    gam = np.repeat(np.exp(log_gamma), HEAD_DIM)
    gcol = jnp.asarray(gam[:, None], F32)
    grow = jnp.asarray(gam[None, :], F32)
    kcol = k.reshape(nb, D_RET, 1)
    row = pl.BlockSpec((None, 1, D_RET), lambda bi: (bi, 0, 0))
    return pl.pallas_call(
        _ret_dec_kernel,
        out_shape=[jax.ShapeDtypeStruct((nb, 1, D_RET), F32), jax.ShapeDtypeStruct((nb, D_RET, HEAD_DIM), F32)],
        grid=(nb,),
        in_specs=[row, row, pl.BlockSpec((None, D_RET, 1), lambda bi: (bi, 0, 0)), row, row,
                  _const_spec(gcol.shape), _const_spec(grow.shape),
                  pl.BlockSpec((None, None, D_RET, HEAD_DIM), lambda bi: (layer, bi, 0, 0))],
        out_specs=[row, pl.BlockSpec((None, D_RET, HEAD_DIM), lambda bi: (bi, 0, 0))],
        compiler_params=_cparams(("parallel",)),
        name="ret_decode",
    )(q, k, kcol, v, g, gcol, grow, state)


def _swap_perm(rot_dim):
    half = rot_dim // 2
    p = np.arange(HEAD_DIM)
    p[:half] = np.arange(half, rot_dim)
    p[half:rot_dim] = np.arange(half)
    return p


def _swap_cols(start, width, rot_dim):
    p = _swap_perm(rot_dim)
    return np.concatenate([start + h * HEAD_DIM + p for h in range(width // HEAD_DIM)])


def _rope_tables(pos, rot_dim, theta, width):
    half = rot_dim // 2
    inv = jnp.exp(-math.log(theta) * jnp.arange(half, dtype=F32) / half)
    ang = pos.astype(F32)[:, None] * inv[None, :]
    cos, sin = jnp.cos(ang), jnp.sin(ang)
    ones = jnp.ones((pos.shape[0], HEAD_DIM - rot_dim), F32)
    c = jnp.concatenate([cos, cos, ones], axis=1)
    s = jnp.concatenate([-sin, sin, 0.0 * ones], axis=1)
    reps = width // HEAD_DIM
    return jnp.tile(c, (1, reps)), jnp.tile(s, (1, reps))


def _layer_params(l, norm1_g, w_in, nsa_q_norm, nsa_k_norm, cmp_pos_k, cmp_pos_v, cmp_w_k, cmp_w_v, w_out,
                  norm2_g, w_up, w_down):
    w = w_in[l]
    d = w.shape[0]
    wm = jnp.pad(w, ((0, 0), (0, _N_IN_PAD - _N_IN))).astype(BF)
    cols = np.concatenate([
        _swap_cols(_C_RQ, D_RET, HEAD_DIM), _swap_cols(_C_RK, D_RET, HEAD_DIM),
        _swap_cols(_C_NQ, D_NSA, ROT_DIM), _swap_cols(_C_CK, D_KV, ROT_DIM),
        _swap_cols(_C_SK, D_KV, ROT_DIM), _swap_cols(_C_WK, D_KV, ROT_DIM)])
    ws = w[:, cols].astype(BF)
    gains = jnp.concatenate([nsa_q_norm[l][None], nsa_k_norm[l]], axis=0)
    gn = jnp.tile(gains, (1, LANES // HEAD_DIM))
    gs = jnp.tile(gains[:, _swap_perm(ROT_DIM)], (1, LANES // HEAD_DIM))
    eye2 = jnp.eye(NSA_KV, dtype=F32)
    return dict(
        g1=norm1_g[l][None], wm=wm, ws=ws, gn=gn, gs=gs,
        pk=jnp.tile(cmp_pos_k[l], (1, NSA_KV)), pv=jnp.tile(cmp_pos_v[l], (1, NSA_KV)),
        phik=jnp.kron(eye2, cmp_w_k[l]), phiv=jnp.kron(eye2, cmp_w_v[l]),
        wo=w_out[l].astype(BF), g2=norm2_g[l][None], wu=w_up[l].astype(BF), wd=w_down[l].astype(BF))


def _prompt_layer(xp2d, prm, tabs, *, b, t, tm):
    (sbk_f, sbv_f, ck_f, cv_f, sk_f, sv_f, wk_f, wv_f,
     sbq_b, sbk_b, sbv_b, rq_b, rk_b, rv_b, rg_f, nq_b, sk_b, sv_b, wk_b, wv_b, ng_f) = _project(
        xp2d, prm['g1'], prm['wm'], prm['ws'], tabs, prm['gn'], prm['gs'], tm=tm, tab_blocks=t // tm)
    o_sb = _sb_prompt(sbq_b, sbk_b, sbv_b, b=b, t=t)
    o_ret, ret_st = _ret_prompt(rq_b, rk_b, rv_b, rg_f, b=b, t=t)
    kc, vc = _compress_prompt(ck_f, cv_f, prm['pk'], prm['pv'], prm['phik'], prm['phiv'])
    o_nsa = _nsa_prompt(nq_b, ng_f, kc, vc, sk_b, sv_b, wk_b, wv_b, b=b, t=t)
    y = _out_mlp(xp2d, o_sb, o_ret, o_nsa, prm['wo'], prm['g2'], prm['wu'], prm['wd'], tm=tm)
    keep = min(WINDOW, t)
    r4 = lambda a, h: a.reshape(b, t, h, HEAD_DIM)
    st = ret_st.reshape(b, RET_HEADS, HEAD_DIM, RET_HEADS, HEAD_DIM)
    st = jnp.stack([st[:, h, :, h, :] for h in range(RET_HEADS)], axis=1)
    caches = dict(
        p_sb_k=r4(sbk_f, SB_HEADS), p_sb_v=r4(sbv_f, SB_HEADS),
        p_cmp_k=r4(ck_f, NSA_KV), p_cmp_v=r4(cv_f, NSA_KV),
        p_slc_k=r4(sk_f, NSA_KV), p_slc_v=r4(sv_f, NSA_KV),
        p_win_k=r4(wk_f, NSA_KV)[:, t - keep:], p_win_v=r4(wv_f, NSA_KV)[:, t - keep:],
        p_ret=st)
    return y, caches


def _sample_layer(l, xs2d, prm, tabs, caches, states, page_table, *, past_len):
    nb = xs2d.shape[0]
    (sbk_f, sbv_f, ck_f, cv_f, sk_f, sv_f, wk_f, wv_f,
     sbq_b, sbk_b, sbv_b, rq_b, rk_b, rv_b, rg_f, nq_b, sk_b, sv_b, wk_b, wv_b, ng_f) = _project(
        xs2d, prm['g1'], prm['wm'], prm['ws'], tabs, prm['gn'], prm['gs'], tm=nb, tab_blocks=1)
    row = lambda a: a.astype(F32).reshape(nb, 1, a.shape[-1])
    c_sb_k, c_sb_v, c_cmp_k, c_cmp_v, c_slc_k, c_slc_v = caches
    win_k, win_v, st_ret = states
    pg = c_cmp_k.shape[2]
    nblk = page_table.shape[1] * pg // BLK_CMP
    o_sb = _sb_decode(l, page_table, row(sbq_b), c_sb_k, c_sb_v)
    o_ret, ret_new = _ret_decode(l, row(rq_b), row(rk_b), row(rv_b), row(rg_f), st_ret)
    kc, vc = _cmp_decode(l, page_table, row(ck_f), row(cv_f), c_cmp_k, c_cmp_v,
                         prm['pk'], prm['pv'], prm['phik'], prm['phiv'])
    ocmp, idx = _sel_decode(row(nq_b), kc, vc, q_pos=past_len, nblk=nblk)
    idx = jnp.transpose(idx[:, :, :NSA_KV], (0, 2, 1)).reshape(nb, NSA_KV * TOP_N)
    o_nsa, win_k_new, win_v_new = _nsa_decode(
        l, page_table, idx, row(nq_b), row(ng_f), ocmp, row(sk_f), row(sv_f), row(wk_f), row(wv_f),
        win_k, win_v, c_slc_k, c_slc_v)
    y = _out_mlp(xs2d, o_sb.reshape(nb, D_SB), o_ret.reshape(nb, D_RET), o_nsa.reshape(nb, D_NSA),
                 prm['wo'], prm['g2'], prm['wu'], prm['wd'], tm=nb)
    r4 = lambda a, h: a.reshape(nb, 1, h, HEAD_DIM)
    wb = win_k.shape[2]
    out = dict(
        s_sb_k=r4(sbk_f, SB_HEADS), s_sb_v=r4(sbv_f, SB_HEADS),
        s_cmp_k=r4(ck_f, NSA_KV), s_cmp_v=r4(cv_f, NSA_KV),
        s_slc_k=r4(sk_f, NSA_KV), s_slc_v=r4(sv_f, NSA_KV),
        s_win_k=win_k_new.reshape(nb, wb, NSA_KV, HEAD_DIM), s_win_v=win_v_new.reshape(nb, wb, NSA_KV, HEAD_DIM),
        s_ret=ret_new.reshape(nb, RET_HEADS, HEAD_DIM, HEAD_DIM))
    return y, out


def kernel(x_prompt, x_sample, cache_sb_k, cache_sb_v, cache_cmp_k, cache_cmp_v, cache_slc_k, cache_slc_v,
           state_win_k, state_win_v, state_ret, page_table, norm1_g, w_in, nsa_q_norm, nsa_k_norm, cmp_pos_k,
           cmp_pos_v, cmp_w_k, cmp_w_v, w_out, norm2_g, w_up, w_down):
    b, t, d = x_prompt.shape
    depth = w_in.shape[0]
    tm = min(512, t)
    pos_p = jnp.arange(t, dtype=I32)
    tabs_p = (*_rope_tables(pos_p, HEAD_DIM, RET_THETA, D_RET), *_rope_tables(pos_p, ROT_DIM, ROPE_THETA, LANES))
    xp = x_prompt.reshape(b * t, d)

    nb, n_new, _ = x_sample.shape
    assert n_new == 1, "sample group kernels handle one new token per sample"
    pg = cache_sb_k.shape[2]
    past_len = page_table.shape[1] * pg
    wb = state_win_k.shape[2]
    assert wb <= WINDOW and wb <= past_len
    pos_s = jnp.full((nb,), past_len, dtype=I32)
    tabs_s = (*_rope_tables(pos_s, HEAD_DIM, RET_THETA, D_RET), *_rope_tables(pos_s, ROT_DIM, ROPE_THETA, LANES))
    xs = x_sample.reshape(nb, d)
    flat = lambda c: c.reshape(*c.shape[:3], c.shape[3] * c.shape[4])
    caches = tuple(flat(c) for c in (cache_sb_k, cache_sb_v, cache_cmp_k, cache_cmp_v, cache_slc_k, cache_slc_v))
    states = (flat(state_win_k), flat(state_win_v),
              state_ret.reshape(depth, nb, RET_HEADS * HEAD_DIM, HEAD_DIM))

    new = {}
    for l in range(depth):
        prm = _layer_params(l, norm1_g, w_in, nsa_q_norm, nsa_k_norm, cmp_pos_k, cmp_pos_v, cmp_w_k, cmp_w_v,
                            w_out, norm2_g, w_up, w_down)
        xp, p_new = _prompt_layer(xp, prm, tabs_p, b=b, t=t, tm=tm)
        xs, s_new = _sample_layer(l, xs, prm, tabs_s, caches, states, page_table, past_len=past_len)
        for name, val in {**p_new, **s_new}.items():
            new.setdefault(name, []).append(val)
    st = lambda name: jnp.stack(new[name])
    return (xp.reshape(b, t, d), xs.reshape(nb, 1, d)) + tuple(st(nm) for nm in (
        'p_sb_k', 'p_sb_v', 'p_cmp_k', 'p_cmp_v', 'p_slc_k', 'p_slc_v', 'p_win_k', 'p_win_v', 'p_ret',
        's_sb_k', 's_sb_v', 's_cmp_k', 's_cmp_v', 's_slc_k', 's_slc_v', 's_win_k', 's_win_v', 's_ret'))
```

```python
import functools
import math

import numpy as np
import jax
import jax.numpy as jnp
from jax import lax
from jax.experimental import pallas as pl
from jax.experimental.pallas import tpu as pltpu

HEAD_DIM = 64
SB_HEADS = 4
RET_HEADS = 4
NSA_HEADS = 8
NSA_KV = 2
NSA_REP = NSA_HEADS // NSA_KV
D_SB = SB_HEADS * HEAD_DIM
D_RET = RET_HEADS * HEAD_DIM
D_NSA = NSA_HEADS * HEAD_DIM
D_KV = NSA_KV * HEAD_DIM
ROPE_THETA = 500000.0
ROT_DIM = HEAD_DIM // 4
RET_THETA = 10000.0
BLK_CMP = 32
BLK_SEL = 64
TOP_N = 16
WINDOW = 512
FORCE_SCORE = 1.0e4
NEG = -1.0e30
EPS = 1e-6
QK_SCALE = HEAD_DIM ** -0.5

LANES = 128
VMEM_LIMIT = 56 * 1024 * 1024

BF = jnp.bfloat16
F32 = jnp.float32
I32 = jnp.int32

_C_SBQ, _C_SBK, _C_SBV = 0, 256, 512
_C_RQ, _C_RK, _C_RV, _C_RG = 768, 1024, 1280, 1536
_C_NQ = 1792
_C_CK, _C_CV, _C_SK, _C_SV, _C_WK, _C_WV = 2304, 2432, 2560, 2688, 2816, 2944
_C_NG = 3072
_N_IN = 3096
_N_IN_PAD = 3200
_S_RQ, _S_RK, _S_NQ, _S_CK, _S_SK, _S_WK = 0, 256, 512, 1024, 1152, 1280
_N_SWAP = 1408


def _dot(a, b):
    return jnp.dot(a, b, preferred_element_type=F32)


def _dot_nt(a, b):
    return lax.dot_general(a, b, (((1,), (1,)), ((), ())), preferred_element_type=F32)


def _dot_tn(a, b):
    return lax.dot_general(a, b, (((0,), (0,)), ((), ())), preferred_element_type=F32)


def _split2(x):
    hi = x.astype(BF)
    lo = (x - hi.astype(F32)).astype(BF)
    return hi, lo


def _split3(x):
    hi = x.astype(BF)
    r = x - hi.astype(F32)
    mid = r.astype(BF)
    lo = (r - mid.astype(F32)).astype(BF)
    return hi, mid, lo


def _dot_x2(x, w):
    hi, lo = _split2(x)
    return _dot(hi, w) + _dot(lo, w)


def _seg_mean_mat(n):
    r = lax.broadcasted_iota(I32, (n, n), 0) // HEAD_DIM
    c = lax.broadcasted_iota(I32, (n, n), 1) // HEAD_DIM
    return jnp.where(r == c, 1.0 / HEAD_DIM, 0.0).astype(BF)


def _cparams(sem, vmem=VMEM_LIMIT):
    return pltpu.CompilerParams(dimension_semantics=sem, vmem_limit_bytes=vmem)


def _const_spec(shape):
    nd = len(shape)
    return pl.BlockSpec(shape, lambda *a: (0,) * nd)


def _proj_kernel(x_ref, g1_ref, wm_ref, ws_ref, cr_ref, sr_ref, cn_ref, sn_ref, gn_ref, gs_ref,
                 sbk_f, sbv_f, ck_f, cv_f, sk_f, sv_f, wk_f, wv_f,
                 sbq_b, sbk_b, sbv_b, rq_b, rk_b, rv_b, rg_f, nq_b, sk_b, sv_b, wk_b, wv_b, ng_f, *, t_out):
    x = x_ref[...]
    ms = jnp.mean(x * x, axis=-1, keepdims=True)
    xn = (x * lax.rsqrt(ms + EPS) * g1_ref[...]).astype(BF)

    def put(ref, val):
        ref[...] = jnp.transpose(val) if t_out else val

    def mm(w_ref, lo, n):
        return _dot(xn, w_ref[:, lo:lo + n])

    sbq_b[...] = (mm(wm_ref, _C_SBQ, D_SB) * QK_SCALE).astype(BF)
    k = mm(wm_ref, _C_SBK, D_SB)
    put(sbk_f, k)
    sbk_b[...] = k.astype(BF)
    v = mm(wm_ref, _C_SBV, D_SB)
    put(sbv_f, v)
    sbv_b[...] = v.astype(BF)

    cr = cr_ref[...]
    sr = sr_ref[...]
    rq_b[...] = (mm(wm_ref, _C_RQ, D_RET) * cr + mm(ws_ref, _S_RQ, D_RET) * sr).astype(BF)
    rk_b[...] = ((mm(wm_ref, _C_RK, D_RET) * cr + mm(ws_ref, _S_RK, D_RET) * sr) * QK_SCALE).astype(BF)
    rv_b[...] = mm(wm_ref, _C_RV, D_RET).astype(BF)
    rg_f[...] = mm(wm_ref, _C_RG, D_RET)

    seg = _seg_mean_mat(LANES)
    cn = cn_ref[...]
    sn = sn_ref[...]

    def normrope(cm, cs, gi):
        y = mm(wm_ref, cm, LANES)
        ysw = mm(ws_ref, cs, LANES)
        r = lax.rsqrt(_dot_x2(y * y, seg) + EPS)
        g = gn_ref[gi:gi + 1, :]
        gsw = gs_ref[gi:gi + 1, :]
        return r * (y * (g * cn) + ysw * (gsw * sn))

    for c in range(D_NSA // LANES):
        nq_b[:, c * LANES:(c + 1) * LANES] = (
            normrope(_C_NQ + c * LANES, _S_NQ + c * LANES, 0) * QK_SCALE).astype(BF)
    put(ck_f, normrope(_C_CK, _S_CK, 1))
    put(cv_f, mm(wm_ref, _C_CV, D_KV))
    k = normrope(_C_SK, _S_SK, 2)
    put(sk_f, k)
    sk_b[...] = k.astype(BF)
    v = mm(wm_ref, _C_SV, D_KV)
    put(sv_f, v)
    sv_b[...] = v.astype(BF)
    k = normrope(_C_WK, _S_WK, 3)
    put(wk_f, k)
    wk_b[...] = k.astype(BF)
    v = mm(wm_ref, _C_WV, D_KV)
    put(wv_f, v)
    wv_b[...] = v.astype(BF)
    ng_f[...] = jax.nn.sigmoid(mm(wm_ref, _C_NG, LANES))


def _project(x2d, g1, wm, ws, tabs, gn, gs, *, tm, tab_blocks, seqs=None):
    n, d = x2d.shape
    cr, sr, cn, sn = tabs
    grid = (n // tm,)
    tok = lambda w: pl.BlockSpec((tm, w), lambda i: (i, 0))
    tab = lambda w: pl.BlockSpec((tm, w), lambda i: (i % tab_blocks, 0))
    f32 = lambda w: jax.ShapeDtypeStruct((n, w), F32)
    bf = lambda w: jax.ShapeDtypeStruct((n, w), BF)
    out_w_f = [D_SB, D_SB] + [D_KV] * 6
    if seqs is None:
        cache_shapes = [f32(w) for w in out_w_f]
        cache_specs = [tok(w) for w in out_w_f]
    else:
        t = n // seqs
        per = t // tm
        cache_shapes = [jax.ShapeDtypeStruct((seqs, w, t), F32) for w in out_w_f]
        cache_specs = [pl.BlockSpec((None, w, tm), lambda i: (i // per, 0, i % per)) for w in out_w_f]
    out_shape = (cache_shapes
                 + [bf(D_SB)] * 3 + [bf(D_RET)] * 3 + [f32(D_RET), bf(D_NSA)] + [bf(D_KV)] * 4 + [f32(LANES)])
    out_w = [D_SB] * 3 + [D_RET] * 3 + [D_RET, D_NSA] + [D_KV] * 4 + [LANES]
    return pl.pallas_call(
        functools.partial(_proj_kernel, t_out=seqs is not None),
        out_shape=out_shape,
        grid=grid,
        in_specs=[tok(d), _const_spec((1, d)), _const_spec(wm.shape), _const_spec(ws.shape),
                  tab(D_RET), tab(D_RET), tab(LANES), tab(LANES),
                  _const_spec(gn.shape), _const_spec(gs.shape)],
        out_specs=cache_specs + [tok(w) for w in out_w],
        compiler_params=_cparams(("parallel",)),
        name="proj",
    )(x2d, g1, wm, ws, cr, sr, cn, sn, gn, gs)


SB_DEAD = -120.0


def _softplus(z):
    return jnp.maximum(z, 0.0) + jnp.log(1.0 + jnp.exp(-jnp.abs(z)))


def _sb_kernel(q_ref, k_ref, v_ref, o_ref, c_ref, acc_ref, *, tq, tk):
    i = pl.program_id(1)
    m = SB_HEADS * tq
    q = q_ref[...]
    lane_h = lax.broadcasted_iota(I32, (tq, D_SB), 1) // HEAD_DIM
    qs = jnp.concatenate([jnp.where(lane_h == h, q, jnp.zeros_like(q)) for h in range(SB_HEADS)], axis=0)
    q_pos = i * tq + lax.broadcasted_iota(I32, (m, tk), 0) % tq
    col = lax.broadcasted_iota(I32, (m, tk), 1)
    tri = (lax.broadcasted_iota(I32, (tk, tk), 0) > lax.broadcasted_iota(I32, (tk, tk), 1)).astype(BF)
    nt = (i * tq) // tk + 1

    c_ref[...] = jnp.zeros_like(c_ref)
    acc_ref[...] = jnp.zeros_like(acc_ref)

    def tile(j, masked):
        off = pl.multiple_of(j * tk, tk)
        z = _dot_nt(qs, k_ref[pl.ds(off, tk), :])
        l1m = -_softplus(z)
        if masked:
            mask = (off + col) < q_pos
            l1m = jnp.where(mask, l1m, 0.0)
        c = c_ref[...]
        after = c + _dot_x2(l1m, tri)
        lw = z + l1m + after
        if masked:
            lw = jnp.where(mask, lw, NEG)
        w = jnp.exp(lw)
        acc_ref[...] += _dot(w.astype(BF), v_ref[pl.ds(off, tk), :])
        c_ref[...] = c + jnp.sum(l1m, axis=1, keepdims=True)

    tile(nt - 1, True)

    def alive():
        return jnp.max(c_ref[...]) > SB_DEAD

    def cond(state):
        return (state[0] < nt - 1) & state[1]

    def body(state):
        tile(nt - 2 - state[0], False)
        return state[0] + 1, alive()

    lax.while_loop(cond, body, (jnp.int32(0), alive()))

    acc = acc_ref[...]
    out = jnp.zeros((tq, D_SB), F32)
    for h in range(SB_HEADS):
        out = out + jnp.where(lane_h == h, acc[h * tq:(h + 1) * tq], 0.0)
    o_ref[...] = out.astype(BF)


def _sb_prompt(q, k, v, *, b, t, tq=128, tk=256):
    n = b * t
    nq = t // tq
    return pl.pallas_call(
        functools.partial(_sb_kernel, tq=tq, tk=tk),
        out_shape=jax.ShapeDtypeStruct((n, D_SB), BF),
        grid=(b, nq),
        in_specs=[pl.BlockSpec((tq, D_SB), lambda bi, i: (bi * nq + i, 0)),
                  pl.BlockSpec((t, D_SB), lambda bi, i: (bi, 0)),
                  pl.BlockSpec((t, D_SB), lambda bi, i: (bi, 0))],
        out_specs=pl.BlockSpec((tq, D_SB), lambda bi, i: (bi * nq + i, 0)),
        scratch_shapes=[pltpu.VMEM((SB_HEADS * tq, 1), F32), pltpu.VMEM((SB_HEADS * tq, D_SB), F32)],
        compiler_params=_cparams(("parallel", "parallel")),
        name="sb_prompt",
    )(q, k, v)


def _ret_kernel(q_ref, k_ref, v_ref, g_ref, dec_ref, qd_ref, kd_ref, gc_ref, o_ref, st_ref, s_scr, *, c):
    ci = pl.program_id(1)

    @pl.when(ci == 0)
    def _():
        s_scr[...] = jnp.zeros_like(s_scr)

    q = q_ref[...]
    k = k_ref[...]
    v = v_ref[...]
    lane_h = lax.broadcasted_iota(I32, (c, D_RET), 1) // HEAD_DIM
    o = jnp.zeros((c, D_RET), F32)
    for h in range(RET_HEADS):
        kh = jnp.where(lane_h == h, k, jnp.zeros_like(k))
        vh = jnp.where(lane_h == h, v, jnp.zeros_like(v))
        s = _dot_nt(q, kh) * dec_ref[h]
        o = o + _dot(s.astype(BF), vh)
    st = s_scr[...]
    shi, slo = _split2(st)
    o = o + (_dot(q, shi) + _dot(q, slo)) * qd_ref[...]

    kd = (k.astype(F32) * kd_ref[...]).astype(BF)
    ktv = _dot_tn(kd, v)
    r = lax.broadcasted_iota(I32, (D_RET, D_RET), 0) // HEAD_DIM
    cc = lax.broadcasted_iota(I32, (D_RET, D_RET), 1) // HEAD_DIM
    new_st = st * gc_ref[...] + jnp.where(r == cc, ktv, 0.0)
    s_scr[...] = new_st
    st_ref[...] = new_st

    seg = _seg_mean_mat(D_RET)
    mu = _dot_x2(o, seg)
    d = o - mu
    var = _dot_x2(d * d, seg)
    y = d * lax.rsqrt(var + EPS)
    o_ref[...] = (y * jax.nn.silu(g_ref[...])).astype(BF)


def _ret_tables(c):
    log_gamma = np.log1p(-np.exp2(-5.0 - np.arange(RET_HEADS, dtype=np.float64)))
    idx = np.arange(c, dtype=np.float64)
    diff = idx[:, None] - idx[None, :]
    dec = np.where(diff >= 0, np.exp(np.maximum(diff, 0.0)[None] * log_gamma[:, None, None]), 0.0)
    lane_lg = np.repeat(log_gamma, HEAD_DIM)
    qd = np.exp((idx[:, None] + 1.0) * lane_lg[None, :])
    kd = np.exp((c - 1.0 - idx)[:, None] * lane_lg[None, :])
    gc = np.exp(c * lane_lg)[None, :]
    f = lambda a: jnp.asarray(a, F32)
    return f(dec), f(qd), f(kd), f(gc)


def _ret_prompt(q, k, v, g, *, b, t, c=256):
    n = b * t
    nc = t // c
    dec, qd, kd, gc = _ret_tables(c)
    tok = pl.BlockSpec((c, D_RET), lambda bi, i: (bi * nc + i, 0))
    o, st = pl.pallas_call(
        functools.partial(_ret_kernel, c=c),
        out_shape=[jax.ShapeDtypeStruct((n, D_RET), BF), jax.ShapeDtypeStruct((b, D_RET, D_RET), F32)],
        grid=(b, nc),
        in_specs=[tok, tok, tok, tok, _const_spec(dec.shape), _const_spec(qd.shape), _const_spec(kd.shape),
                  _const_spec(gc.shape)],
        out_specs=[tok, pl.BlockSpec((None, D_RET, D_RET), lambda bi, i: (bi, 0, 0))],
        scratch_shapes=[pltpu.VMEM((D_RET, D_RET), F32)],
        compiler_params=_cparams(("parallel", "arbitrary")),
        name="ret_prompt",
    )(q, k, v, g, dec, qd, kd, gc)
    return o, st


CMP_SPAN = 1024


def _compress_t(xt, post, phi_hi, phi_lo):
    n = xt.shape[1]
    sel = (lax.broadcasted_iota(I32, (LANES, n), 1) // BLK_CMP
           == lax.broadcasted_iota(I32, (LANES, n), 0)).astype(BF)
    x_hi, x_lo = _split2(xt * post)
    xs = _dot_nt(sel, x_hi) + _dot_nt(sel, x_lo)
    hi, lo = _split2(xs)
    out = _dot(hi, phi_hi) + _dot(lo, phi_hi) + _dot(hi, phi_lo)
    return out[0:n // BLK_CMP]


def _compress_kernel(k_ref, v_ref, pk_ref, pv_ref, wk_ref, wv_ref, ko_ref, vo_ref):
    wk_hi, wk_lo = _split2(wk_ref[...])
    wv_hi, wv_lo = _split2(wv_ref[...])
    ko_ref[...] = _compress_t(k_ref[...], pk_ref[...], wk_hi, wk_lo).astype(BF)
    vo_ref[...] = _compress_t(v_ref[...], pv_ref[...], wv_hi, wv_lo).astype(BF)


def _compress_prompt(ckt, cvt, pkt, pvt, wk, wv):
    b, _, t = ckt.shape
    steps = t // CMP_SPAN
    tok = pl.BlockSpec((None, LANES, CMP_SPAN), lambda bi, i: (bi, 0, i))
    out = pl.BlockSpec((CMP_SPAN // BLK_CMP, LANES), lambda bi, i: (bi * steps + i, 0))
    return pl.pallas_call(
        _compress_kernel,
        out_shape=[jax.ShapeDtypeStruct((b * t // BLK_CMP, LANES), BF)] * 2,
        grid=(b, steps),
        in_specs=[tok, tok, _const_spec(pkt.shape), _const_spec(pvt.shape), _const_spec(wk.shape),
                  _const_spec(wv.shape)],
        out_specs=[out, out],
        compiler_params=_cparams(("parallel", "parallel")),
        name="compress_prompt",
    )(ckt, cvt, pkt, pvt, wk, wv)


def _stack_heads_q(q, extra):
    tq = q.shape[0]
    lane = lax.broadcasted_iota(I32, (tq, LANES), 1)
    rows = []
    for h in range(NSA_HEADS):
        g = h // NSA_REP
        blk = q[:, (h // 2) * LANES:(h // 2 + 1) * LANES]
        src_half = h % 2
        if src_half != g:
            blk32 = pltpu.roll(blk.astype(F32), HEAD_DIM, 1).astype(BF)
        else:
            blk32 = blk
        keep = (lane // HEAD_DIM) == g
        rows.append(jnp.where(keep, blk32, jnp.zeros_like(blk32)))
    qs = jnp.concatenate(rows, axis=0)
    if extra is not None:
        qs = jnp.concatenate([qs, extra], axis=1)
    return qs


def _unstack_heads(acc, tq):
    lane = lax.broadcasted_iota(I32, (tq, LANES), 1)
    cols = []
    for c in range(NSA_HEADS // 2):
        g = (2 * c) // NSA_REP
        a = acc[(2 * c) * tq:(2 * c + 1) * tq]
        b = acc[(2 * c + 1) * tq:(2 * c + 2) * tq]
        if g == 0:
            cols.append(jnp.where(lane < HEAD_DIM, a, pltpu.roll(b, HEAD_DIM, 1)))
        else:
            cols.append(jnp.where(lane < HEAD_DIM, pltpu.roll(a, HEAD_DIM, 1), b))
    return jnp.concatenate(cols, axis=1)


def _gate_expand(gates, branch):
    r = lax.broadcasted_iota(I32, (LANES, D_NSA), 0)
    c = lax.broadcasted_iota(I32, (LANES, D_NSA), 1)
    e = (r == (c // HEAD_DIM) * 3 + branch).astype(BF)
    return _dot_x2(gates, e)


def _nsa_kernel(q_ref, gate_ref, kc_ref, vc_ref, sk_ref, sv_ref, wk_ref, wv_ref, o_ref,
                k2_scr, svt_scr, wvt_scr, imp_scr, m_scr, l_scr, acc_scr, *, tq, tk, tw, t):
    i = pl.program_id(1)
    nc = t // BLK_CMP
    ns = t // BLK_SEL
    m = NSA_HEADS * tq

    @pl.when(i == 0)
    def _():
        rows = 512
        for r0 in range(0, t, rows):
            s_idx = r0 + lax.broadcasted_iota(I32, (rows, LANES), 0)
            c_idx = lax.broadcasted_iota(I32, (rows, LANES), 1)
            e = (s_idx // BLK_SEL == c_idx).astype(BF)
            k2_scr[r0:r0 + rows, :] = jnp.concatenate([sk_ref[r0:r0 + rows, :], e], axis=1)
        for c in range(t // tk):
            svt_scr[c] = jnp.transpose(sv_ref[c * tk:(c + 1) * tk, :].astype(F32)).astype(BF)
        for c in range(t // tw):
            wvt_scr[c] = jnp.transpose(wv_ref[c * tw:(c + 1) * tw, :].astype(F32)).astype(BF)

    q = q_ref[...]
    qs = _stack_heads_q(q, None)

    kc = kc_ref[...]
    vct = jnp.transpose(vc_ref[...].astype(F32)).astype(BF)
    n_idx = lax.broadcasted_iota(I32, (nc, m), 0)
    t_idx = i * tq + lax.broadcasted_iota(I32, (nc, m), 1) % tq
    valid = (n_idx + 1) * BLK_CMP - 1 <= t_idx
    st = jnp.where(valid, _dot_nt(kc, qs), NEG)
    mx = jnp.max(st, axis=0, keepdims=True)
    e = jnp.where(valid, jnp.exp(st - mx), 0.0)
    den = jnp.sum(e, axis=0, keepdims=True)
    p = e / jnp.where(den > 0.0, den, 1.0)
    o_cmp = _dot(vct, p.astype(BF))
    imp = []
    for g in range(NSA_KV):
        acc_g = p[:, g * NSA_REP * tq:(g * NSA_REP + 1) * tq]
        for h in range(g * NSA_REP + 1, (g + 1) * NSA_REP):
            acc_g = acc_g + p[:, h * tq:(h + 1) * tq]
        imp.append(acc_g)

    pair = (lax.broadcasted_iota(I32, (ns, nc), 1) // (BLK_SEL // BLK_CMP)
            == lax.broadcasted_iota(I32, (ns, nc), 0)).astype(BF)
    blk = lax.broadcasted_iota(I32, (ns, tq), 0)
    tb = i * tq + lax.broadcasted_iota(I32, (ns, tq), 1)
    forced = (blk == 0) | (blk == tb // BLK_SEL)
    future = blk * BLK_SEL > tb
    eye = (lax.broadcasted_iota(I32, (tq, tq), 0) == lax.broadcasted_iota(I32, (tq, tq), 1)).astype(BF)
    pens = []
    for g in range(NSA_KV):
        a, b, c = _split3(imp[g])
        v = _dot(pair, a) + _dot(pair, b) + _dot(pair, c)
        v = jnp.where(forced, FORCE_SCORE, jnp.where(future, -1.0, v))
        imp_scr[g] = v

        def body(j, cnt, g=g, v=v):
            vj = imp_scr[g, pl.ds(j, 1), :]
            ge = (vj >= v).astype(I32)
            gt = (vj > v).astype(I32)
            return cnt + jnp.where(blk > j, ge, gt)

        cnt = lax.fori_loop(0, ns, body, jnp.zeros((ns, tq), I32), unroll=8)
        sel_t = (cnt < min(TOP_N, ns)).astype(BF)
        pen = _dot_nt(eye, sel_t)
        pen = ((pen - 1.0) * (-NEG)).astype(BF)
        if ns < LANES:
            pen = jnp.concatenate([pen, jnp.zeros((tq, LANES - ns), BF)], axis=1)
        pens.append(pen)
    pen_rows = jnp.concatenate([pens[h // NSA_REP] for h in range(NSA_HEADS)], axis=0)
    qs2 = jnp.concatenate([qs, pen_rows], axis=1)

    q_pos = i * tq + lax.broadcasted_iota(I32, (1, m), 1) % tq

    def online(scores, vals_t, mask):
        if mask is not None:
            scores = jnp.where(mask, scores, NEG)
        m_old = m_scr[...]
        m_new = jnp.maximum(m_old, jnp.max(scores, axis=0, keepdims=True))
        alpha = jnp.exp(m_old - m_new)
        p = jnp.exp(scores - m_new)
        l_scr[...] = alpha * l_scr[...] + jnp.sum(p, axis=0, keepdims=True)
        acc_scr[...] = alpha * acc_scr[...] + _dot(vals_t, p.astype(BF))
        m_scr[...] = m_new

    def reset():
        m_scr[...] = jnp.full_like(m_scr, NEG)
        l_scr[...] = jnp.zeros_like(l_scr)
        acc_scr[...] = jnp.zeros_like(acc_scr)

    def result():
        return acc_scr[...] / l_scr[...]

    reset()
    nt = (i * tq) // tk + 1
    row_k = lax.broadcasted_iota(I32, (tk, 1), 0)

    def slc_body(j, carry):
        off = pl.multiple_of(j * tk, tk)
        online(_dot_nt(k2_scr[pl.ds(off, tk), :], qs2), svt_scr[j], None)
        return carry

    lax.fori_loop(0, nt - 1, slc_body, 0)
    off = pl.multiple_of((nt - 1) * tk, tk)
    online(_dot_nt(k2_scr[pl.ds(off, tk), :], qs2), svt_scr[nt - 1], (off + row_k) <= q_pos)
    o_slc = result()

    row_w = lax.broadcasted_iota(I32, (tw, 1), 0)
    n_band = (WINDOW + tq + tw - 1) // tw
    first = jnp.maximum(i * tq - WINDOW, 0) // tw
    s_w = []
    for c in range(n_band):
        off = pl.multiple_of((first + c) * tw, tw)
        k_pos = off + row_w
        s = _dot_nt(wk_ref[pl.ds(off, tw), :], qs)
        s_w.append(jnp.where((k_pos <= q_pos) & (q_pos - k_pos <= WINDOW), s, NEG))
    m_w = s_w[0].max(axis=0, keepdims=True)
    for s in s_w[1:]:
        m_w = jnp.maximum(m_w, s.max(axis=0, keepdims=True))
    l_w = jnp.zeros((1, m), F32)
    o_win = jnp.zeros((LANES, m), F32)
    for c, s in enumerate(s_w):
        p = jnp.exp(s - m_w)
        l_w = l_w + p.sum(axis=0, keepdims=True)
        o_win = o_win + _dot(wvt_scr[first + c], p.astype(BF))
    o_win = o_win / l_w

    gt = jnp.transpose(gate_ref[...])

    def gate_row(branch):
        return jnp.concatenate([gt[h * 3 + branch:h * 3 + branch + 1, :] for h in range(NSA_HEADS)], axis=1)

    mix = gate_row(0) * o_cmp + gate_row(1) * o_slc + gate_row(2) * o_win
    stacked = jnp.concatenate([jnp.transpose(mix[:, h * tq:(h + 1) * tq]) for h in range(NSA_HEADS)], axis=0)
    o_ref[...] = _unstack_heads(stacked, tq).astype(BF)


def _nsa_prompt(nq, gates, kc, vc, sk, sv, wk, wv, *, b, t, tq=128, tk=512, tw=128):
    assert t % tk == 0 and (WINDOW + tq + tw - 1) // tw <= t // tw
    n = b * t
    nqb = t // tq
    nc = t // BLK_CMP
    ns = t // BLK_SEL
    m = NSA_HEADS * tq
    tok = lambda w: pl.BlockSpec((tq, w), lambda bi, i: (bi * nqb + i, 0))
    seq = lambda rows, w: pl.BlockSpec((rows, w), lambda bi, i: (bi, 0))
    return pl.pallas_call(
        functools.partial(_nsa_kernel, tq=tq, tk=tk, tw=tw, t=t),
        out_shape=jax.ShapeDtypeStruct((n, D_NSA), BF),
        grid=(b, nqb),
        in_specs=[tok(D_NSA), tok(LANES), seq(nc, LANES), seq(nc, LANES),
                  seq(t, LANES), seq(t, LANES), seq(t, LANES), seq(t, LANES)],
        out_specs=tok(D_NSA),
        scratch_shapes=[pltpu.VMEM((t, 2 * LANES), BF), pltpu.VMEM((t // tk, LANES, tk), BF),
                        pltpu.VMEM((t // tw, LANES, tw), BF), pltpu.VMEM((NSA_KV, ns, tq), F32),
                        pltpu.VMEM((1, m), F32), pltpu.VMEM((1, m), F32), pltpu.VMEM((LANES, m), F32)],
        compiler_params=_cparams(("parallel", "arbitrary")),
        name="nsa_prompt",
    )(nq, gates, kc, vc, sk, sv, wk, wv)


def _out_mlp_kernel(x_ref, osb_ref, oret_ref, onsa_ref, wo_ref, g2_ref, wu_ref, wd_ref, y_ref, *, ff_chunk):
    h = (x_ref[...] + _dot(osb_ref[...].astype(BF), wo_ref[0:D_SB, :])
         + _dot(oret_ref[...].astype(BF), wo_ref[D_SB:D_SB + D_RET, :])
         + _dot(onsa_ref[...].astype(BF), wo_ref[D_SB + D_RET:, :]))
    ms = jnp.mean(h * h, axis=-1, keepdims=True)
    hn = (h * lax.rsqrt(ms + EPS) * g2_ref[...]).astype(BF)
    mlp = None
    d_ff = wu_ref.shape[1]
    for c0 in range(0, d_ff, ff_chunk):
        u = jnp.maximum(_dot(hn, wu_ref[:, c0:c0 + ff_chunk]), 0.0)
        part = _dot((u * u).astype(BF), wd_ref[c0:c0 + ff_chunk, :])
        mlp = part if mlp is None else mlp + part
    y_ref[...] = h + mlp


def _out_mlp(x2d, osb, oret, onsa, wo, g2, wu, wd, *, tm, ff_chunk=1024):
    n, d = x2d.shape
    tok = lambda w: pl.BlockSpec((tm, w), lambda i: (i, 0))
    return pl.pallas_call(
        functools.partial(_out_mlp_kernel, ff_chunk=ff_chunk),
        out_shape=jax.ShapeDtypeStruct((n, d), F32),
        grid=(n // tm,),
        in_specs=[tok(d), tok(D_SB), tok(D_RET), tok(D_NSA), _const_spec(wo.shape), _const_spec((1, d)),
                  _const_spec(wu.shape), _const_spec(wd.shape)],
        out_specs=tok(d),
        compiler_params=_cparams(("parallel",)),
        name="out_mlp",
    )(x2d, osb, oret, onsa, wo, g2, wu, wd)


PAGES_PER_STEP = 8
DEC_ROWS = 16


def _pages_t(c):
    dp, npool, pg, h, dd = c.shape
    return jnp.transpose(c, (0, 1, 3, 4, 2)).reshape(dp, npool, h * dd, pg)


def _page_spec(layer, width, page_of_step, pg):
    def imap(bi, p, pt, *rest):
        return (layer, pt[bi, page_of_step(p)], 0, 0)
    return pl.BlockSpec((None, None, width, pg), imap)


def _sb_dec_kernel(pt_ref, q_ref, *refs, npp, last):
    k_refs, v_refs = refs[:npp], refs[npp:2 * npp]
    o_ref, c_scr, acc_scr, alive_scr = refs[2 * npp:]
    p = pl.program_id(1)
    rows = DEC_ROWS
    pg = k_refs[0].shape[1]

    @pl.when(p == 0)
    def _():
        c_scr[...] = jnp.zeros_like(c_scr)
        acc_scr[...] = jnp.zeros_like(acc_scr)
        alive_scr[0] = 1

    lane_h = lax.broadcasted_iota(I32, (rows, D_SB), 1) // HEAD_DIM
    row = lax.broadcasted_iota(I32, (rows, D_SB), 0)

    @pl.when(alive_scr[0] == 1)
    def _():
        q = jnp.broadcast_to(q_ref[...].astype(F32), (rows, D_SB))
        qs = jnp.where(lane_h == row, q, 0.0).astype(BF)
        tri = (lax.broadcasted_iota(I32, (pg, pg), 0) > lax.broadcasted_iota(I32, (pg, pg), 1)).astype(BF)
        c = c_scr[...]
        acc = acc_scr[...]
        for kk in range(npp):
            z = _dot(qs, k_refs[kk][...].astype(BF))
            l1m = -_softplus(z)
            after = c + _dot_x2(l1m, tri)
            w = jnp.exp(z + l1m + after)
            acc = acc + _dot_nt(w.astype(BF), v_refs[kk][...].astype(BF))
            c = c + jnp.sum(l1m, axis=1, keepdims=True)
        c_scr[...] = c
        acc_scr[...] = acc
        head_rows = lax.broadcasted_iota(I32, (rows, 1), 0) < SB_HEADS
        alive_scr[0] = (jnp.max(jnp.where(head_rows, c, NEG)) > SB_DEAD).astype(I32)

    @pl.when(p == last)
    def _():
        o_ref[...] = jnp.sum(jnp.where(lane_h == row, acc_scr[...], 0.0), axis=0, keepdims=True)


def _sb_decode(layer, page_table, q, cache_k, cache_v):
    nb, npg = page_table.shape
    npp = math.gcd(PAGES_PER_STEP, npg)
    steps = npg // npp
    pg = cache_k.shape[3]
    specs_k = [_page_spec(layer, D_SB, (lambda p, kk=kk: npg - 1 - (p * npp + kk)), pg) for kk in range(npp)]
    specs_v = [_page_spec(layer, D_SB, (lambda p, kk=kk: npg - 1 - (p * npp + kk)), pg) for kk in range(npp)]
    row = pl.BlockSpec((None, 1, D_SB), lambda bi, p, pt: (bi, 0, 0))
    return pl.pallas_call(
        functools.partial(_sb_dec_kernel, npp=npp, last=steps - 1),
        out_shape=jax.ShapeDtypeStruct((nb, 1, D_SB), F32),
        grid_spec=pltpu.PrefetchScalarGridSpec(
            num_scalar_prefetch=1, grid=(nb, steps),
            in_specs=[row] + specs_k + specs_v, out_specs=row,
            scratch_shapes=[pltpu.VMEM((DEC_ROWS, 1), F32), pltpu.VMEM((DEC_ROWS, D_SB), F32),
                            pltpu.SMEM((1,), I32)]),
        compiler_params=_cparams(("parallel", "arbitrary")),
        name="sb_decode",
    )(page_table, q, *([cache_k] * npp), *([cache_v] * npp))


def _cmp_dec_kernel(pt_ref, kn_ref, vn_ref, pk_ref, pv_ref, pkt_ref, pvt_ref, wk_ref, wv_ref, *refs,
                    npp, last, nblk):
    k_refs, v_refs = refs[:npp], refs[npp:2 * npp]
    ko_ref, vo_ref = refs[2 * npp:]
    p = pl.program_id(1)
    wk_hi, wk_lo = _split2(wk_ref[...])
    wv_hi, wv_lo = _split2(wv_ref[...])
    pg = k_refs[0].shape[1]
    per = npp * pg // BLK_CMP
    r0 = pl.multiple_of(p * per, per)
    ko_ref[pl.ds(r0, per), :] = _compress_t(
        jnp.concatenate([r[...] for r in k_refs], axis=1), pkt_ref[...], wk_hi, wk_lo)
    vo_ref[pl.ds(r0, per), :] = _compress_t(
        jnp.concatenate([r[...] for r in v_refs], axis=1), pvt_ref[...], wv_hi, wv_lo)

    @pl.when(p == last)
    def _():
        first = lax.broadcasted_iota(I32, (DEC_ROWS, LANES), 0) == 0
        xk = jnp.where(first, kn_ref[...] * pk_ref[0:1, :], 0.0)
        xv = jnp.where(first, vn_ref[...] * pv_ref[0:1, :], 0.0)
        a, b = _split2(xk)
        ko_ref[nblk:nblk + DEC_ROWS, :] = _dot(a, wk_hi) + _dot(b, wk_hi) + _dot(a, wk_lo)
        a, b = _split2(xv)
        vo_ref[nblk:nblk + DEC_ROWS, :] = _dot(a, wv_hi) + _dot(b, wv_hi) + _dot(a, wv_lo)


def _cmp_decode(layer, page_table, ck_new, cv_new, cache_k, cache_v, pk, pv, pkt, pvt, wk, wv):
    nb, npg = page_table.shape
    pg = cache_k.shape[3]
    npp = CMP_SPAN // pg
    assert npg % npp == 0
    steps = npg // npp
    nblk = npg * pg // BLK_CMP
    specs = [_page_spec(layer, D_KV, (lambda p, kk=kk: p * npp + kk), pg) for kk in range(npp)]
    row = pl.BlockSpec((None, 1, D_KV), lambda bi, p, pt: (bi, 0, 0))
    cst = lambda a: pl.BlockSpec(a.shape, lambda bi, p, pt: (0,) * a.ndim)
    out = pl.BlockSpec((None, nblk + DEC_ROWS, LANES), lambda bi, p, pt: (bi, 0, 0))
    return pl.pallas_call(
        functools.partial(_cmp_dec_kernel, npp=npp, last=steps - 1, nblk=nblk),
        out_shape=[jax.ShapeDtypeStruct((nb, nblk + DEC_ROWS, LANES), F32)] * 2,
        grid_spec=pltpu.PrefetchScalarGridSpec(
            num_scalar_prefetch=1, grid=(nb, steps),
            in_specs=[row, row, cst(pk), cst(pv), cst(pkt), cst(pvt), cst(wk), cst(wv)] + specs + specs,
            out_specs=[out, out]),
        compiler_params=_cparams(("parallel", "arbitrary")),
        name="cmp_decode",
    )(page_table, ck_new, cv_new, pk, pv, pkt, pvt, wk, wv, *([cache_k] * npp), *([cache_v] * npp))


def _stack_heads_q_row(q):
    qf = q.astype(F32)
    row = lax.broadcasted_iota(I32, (DEC_ROWS, LANES), 0)
    lane_g = lax.broadcasted_iota(I32, (DEC_ROWS, LANES), 1) // HEAD_DIM
    out = jnp.zeros((DEC_ROWS, LANES), F32)
    for h in range(NSA_HEADS):
        g = h // NSA_REP
        blk = jnp.broadcast_to(qf[:, (h // 2) * LANES:(h // 2 + 1) * LANES], (DEC_ROWS, LANES))
        if h % 2 != g:
            blk = pltpu.roll(blk, HEAD_DIM, 1)
        out = jnp.where((row == h) & (lane_g == g), blk, out)
    return out.astype(BF)


def _unstack_heads_row(acc):
    lane = lax.broadcasted_iota(I32, (1, LANES), 1)
    accr = pltpu.roll(acc, HEAD_DIM, 1)
    cols = []
    for c in range(NSA_HEADS // 2):
        g = (2 * c) // NSA_REP
        if g == 0:
            cols.append(jnp.where(lane < HEAD_DIM, acc[2 * c:2 * c + 1], accr[2 * c + 1:2 * c + 2]))
        else:
            cols.append(jnp.where(lane < HEAD_DIM, accr[2 * c:2 * c + 1], acc[2 * c + 1:2 * c + 2]))
    return jnp.concatenate(cols, axis=1)


def _sel_dec_kernel(q_ref, kc_ref, vc_ref, ocmp_ref, idx_ref, imp_scr, *, q_pos, nblk, ns_pad):
    nrow = kc_ref.shape[0]
    qs = _stack_heads_q_row(q_ref[...])
    qs = jnp.concatenate([qs, jnp.zeros((LANES - DEC_ROWS, LANES), BF)], axis=0)
    kc = kc_ref[...].astype(BF)
    st = _dot_nt(kc, qs)
    n_idx = lax.broadcasted_iota(I32, (nrow, LANES), 0)
    valid = (n_idx + 1) * BLK_CMP - 1 <= q_pos
    st = jnp.where(valid, st, NEG)
    mx = jnp.max(st, axis=0, keepdims=True)
    e = jnp.where(valid, jnp.exp(st - mx), 0.0)
    den = jnp.sum(e, axis=0, keepdims=True)
    p = e / jnp.where(den > 0.0, den, 1.0)
    ocmp_ref[...] = _dot_tn(p.astype(BF), vc_ref[...].astype(BF))[0:NSA_HEADS]

    grp = (lax.broadcasted_iota(I32, (LANES, LANES), 0) // NSA_REP
           == lax.broadcasted_iota(I32, (LANES, LANES), 1)).astype(BF)
    a, b, c = _split3(p)
    impc = _dot(a, grp) + _dot(b, grp) + _dot(c, grp)
    pair = (lax.broadcasted_iota(I32, (ns_pad, nrow), 1) // (BLK_SEL // BLK_CMP)
            == lax.broadcasted_iota(I32, (ns_pad, nrow), 0)).astype(BF)
    a, b, c = _split3(impc)
    v = _dot(pair, a) + _dot(pair, b) + _dot(pair, c)
    ns = (nblk * BLK_CMP + 1 + BLK_SEL - 1) // BLK_SEL
    blk = lax.broadcasted_iota(I32, (ns_pad, LANES), 0)
    forced = (blk == 0) | (blk == q_pos // BLK_SEL)
    future = blk * BLK_SEL > q_pos
    v = jnp.where(forced, FORCE_SCORE, jnp.where(future, -1.0, v))
    v = jnp.where(blk < ns, v, -2.0)
    imp_scr[...] = v

    def body(j, cnt):
        vj = imp_scr[pl.ds(j, 1), :]
        ge = (vj >= v).astype(I32)
        gt = (vj > v).astype(I32)
        return cnt + jnp.where(blk > j, ge, gt)

    cnt = lax.fori_loop(0, ns_pad, body, jnp.zeros((ns_pad, LANES), I32))
    blk_f = blk.astype(F32)
    rows = [jnp.sum(jnp.where(cnt == r, blk_f, 0.0), axis=0, keepdims=True) for r in range(TOP_N)]
    idx_ref[...] = jnp.concatenate(rows, axis=0).astype(I32)


def _sel_decode(nq, kc, vc, *, q_pos, nblk):
    nb = nq.shape[0]
    nrow = kc.shape[1]
    ns_pad = ((nrow // 2 + 7) // 8) * 8
    return pl.pallas_call(
        functools.partial(_sel_dec_kernel, q_pos=q_pos, nblk=nblk, ns_pad=ns_pad),
        out_shape=[jax.ShapeDtypeStruct((nb, NSA_HEADS, LANES), F32), jax.ShapeDtypeStruct((nb, TOP_N, LANES), I32)],
        grid=(nb,),
        in_specs=[pl.BlockSpec((None, 1, D_NSA), lambda bi: (bi, 0, 0)),
                  pl.BlockSpec((None, nrow, LANES), lambda bi: (bi, 0, 0)),
                  pl.BlockSpec((None, nrow, LANES), lambda bi: (bi, 0, 0))],
        out_specs=[pl.BlockSpec((None, NSA_HEADS, LANES), lambda bi: (bi, 0, 0)),
                   pl.BlockSpec((None, TOP_N, LANES), lambda bi: (bi, 0, 0))],
        scratch_shapes=[pltpu.VMEM((ns_pad, LANES), F32)],
        compiler_params=_cparams(("parallel",)),
        name="sel_decode",
    )(nq, kc, vc)


def _nsa_dec_kernel(pt_ref, idx_ref, q_ref, gate_ref, ocmp_ref, skn_ref, svn_ref, wkn_ref, wvn_ref,
                    wkc_ref, wvc_ref, wk_ref, wv_ref, ck_hbm, cv_hbm, o_ref, wko_ref, wvo_ref, kbuf, vbuf, sem,
                    *, layer, ncache, per):
    nsel = NSA_KV * TOP_N
    bi = pl.program_id(0)
    pg = kbuf.shape[2]

    def page_copies(j):
        blk = jnp.minimum(idx_ref[bi, j], ncache - 1)
        page = pt_ref[bi, blk // per]
        return (pltpu.make_async_copy(ck_hbm.at[layer, page], kbuf.at[j], sem.at[0, j]),
                pltpu.make_async_copy(cv_hbm.at[layer, page], vbuf.at[j], sem.at[1, j]))

    for j in range(nsel):
        for cp in page_copies(j):
            cp.start()

    qs = _stack_heads_q_row(q_ref[...])
    qf = qs.astype(F32)
    row_g = lax.broadcasted_iota(I32, (DEC_ROWS, LANES), 0) // NSA_REP

    def attend(s, vt, s_new, v_new):
        mx = jnp.maximum(s_new, jnp.max(s, axis=1, keepdims=True))
        p_new = jnp.exp(s_new - mx)
        p = jnp.exp(s - mx)
        den = p_new + jnp.sum(p, axis=1, keepdims=True)
        return (p_new * v_new + _dot_nt(p.astype(BF), vt)) / den

    bf_row = lambda r: r[...].astype(BF).astype(F32)
    wk = wk_ref[...]
    wv = wv_ref[...]
    s_wn = jnp.sum(qf * bf_row(wkn_ref), axis=1, keepdims=True)
    o_win = attend(_dot(qs, wk.astype(BF)), wv.astype(BF), s_wn, bf_row(wvn_ref))

    wb = wk.shape[1]
    last = lax.broadcasted_iota(I32, (D_KV, wb), 1) == wb - 1
    wko_ref[...] = jnp.where(last, wkc_ref[...], pltpu.roll(wk, wb - 1, 1))
    wvo_ref[...] = jnp.where(last, wvc_ref[...], pltpu.roll(wv, wb - 1, 1))

    for j in range(nsel):
        for cp in page_copies(j):
            cp.wait()

    s_new = jnp.sum(qf * bf_row(skn_ref), axis=1, keepdims=True)
    nkeys = TOP_N * pg
    col = lax.broadcasted_iota(I32, (DEC_ROWS, nkeys), 1)
    col_slot = col // pg
    col_blk = (col % pg) // BLK_SEL
    o_slc = jnp.zeros((DEC_ROWS, LANES), F32)
    for g in range(NSA_KV):
        pen = jnp.full((DEC_ROWS, nkeys), NEG, F32)
        for r in range(TOP_N):
            blk = idx_ref[bi, g * TOP_N + r]
            want = jnp.where(blk < ncache, blk % per, -1)
            pen = jnp.where((col_slot == r) & (col_blk == want), 0.0, pen)
        kcat = jnp.concatenate([kbuf[g * TOP_N + r] for r in range(TOP_N)], axis=1).astype(BF)
        vcat = jnp.concatenate([vbuf[g * TOP_N + r] for r in range(TOP_N)], axis=1).astype(BF)
        o_g = attend(_dot(qs, kcat) + pen, vcat, s_new, bf_row(svn_ref))
        o_slc = jnp.where(row_g == g, o_g, o_slc)

    gates = jnp.broadcast_to(gate_ref[...], (DEC_ROWS, LANES))
    out = (_gate_expand(gates, 0)[0:1] * _unstack_heads_row(ocmp_ref[...])
           + _gate_expand(gates, 1)[0:1] * _unstack_heads_row(o_slc)
           + _gate_expand(gates, 2)[0:1] * _unstack_heads_row(o_win))
    o_ref[...] = out


def _nsa_decode(layer, page_table, idx, nq, gates, ocmp, sk_new, sv_new, wk_new, wv_new, win_k, win_v,
                cache_k, cache_v):
    nb, npg = page_table.shape
    pg = cache_k.shape[3]
    per = pg // BLK_SEL
    ncache = npg * per
    wb = win_k.shape[3]
    nsel = NSA_KV * TOP_N

    row = lambda w: pl.BlockSpec((None, 1, w), lambda bi, pt, ix: (bi, 0, 0))
    colspec = pl.BlockSpec((None, D_KV, 1), lambda bi, pt, ix: (bi, 0, 0))
    win = pl.BlockSpec((None, None, D_KV, wb), lambda bi, pt, ix: (layer, bi, 0, 0))
    wout = pl.BlockSpec((None, D_KV, wb), lambda bi, pt, ix: (bi, 0, 0))
    hbm = pl.BlockSpec(memory_space=pl.ANY)
    col = lambda a: a.reshape(nb, D_KV, 1)
    return pl.pallas_call(
        functools.partial(_nsa_dec_kernel, layer=layer, ncache=ncache, per=per),
        out_shape=[jax.ShapeDtypeStruct((nb, 1, D_NSA), F32), jax.ShapeDtypeStruct((nb, D_KV, wb), F32),
                   jax.ShapeDtypeStruct((nb, D_KV, wb), F32)],
        grid_spec=pltpu.PrefetchScalarGridSpec(
            num_scalar_prefetch=2, grid=(nb,),
            in_specs=[row(D_NSA), row(LANES), pl.BlockSpec((None, NSA_HEADS, LANES), lambda bi, pt, ix: (bi, 0, 0)),
                      row(D_KV), row(D_KV), row(D_KV), row(D_KV), colspec, colspec, win, win, hbm, hbm],
            out_specs=[row(D_NSA), wout, wout],
            scratch_shapes=[pltpu.VMEM((nsel, D_KV, pg), F32), pltpu.VMEM((nsel, D_KV, pg), F32),
                            pltpu.SemaphoreType.DMA((2, nsel))]),
        compiler_params=_cparams(("arbitrary",)),
        name="nsa_decode",
    )(page_table, idx, nq, gates, ocmp, sk_new, sv_new, wk_new, wv_new, col(wk_new), col(wv_new),
      win_k, win_v, cache_k, cache_v)


def _ret_dec_kernel(q_ref, k_ref, kcol_ref, v_ref, g_ref, gcol_ref, grow_ref, st_ref, o_ref, sto_ref):
    rows = DEC_ROWS
    st = st_ref[...]
    q = jnp.broadcast_to(q_ref[...], (rows, D_RET))
    v = jnp.broadcast_to(v_ref[...], (rows, D_RET))
    lane_h = lax.broadcasted_iota(I32, (rows, D_RET), 1) // HEAD_DIM
    row = lax.broadcasted_iota(I32, (rows, D_RET), 0)
    own = lane_h == row
    qs = jnp.where(own, q, 0.0).astype(BF)
    spread = (lax.broadcasted_iota(I32, (HEAD_DIM, D_RET), 1) % HEAD_DIM
              == lax.broadcasted_iota(I32, (HEAD_DIM, D_RET), 0)).astype(BF)
    shi, slo = _split2(st)
    cross = _dot(qs, shi) + _dot(qs, slo)
    o_cross = jnp.sum(jnp.where(own, _dot_x2(cross, spread), 0.0), axis=0, keepdims=True) * grow_ref[...]
    seg = _seg_mean_mat(D_RET)
    qk = q * jnp.broadcast_to(k_ref[...], (rows, D_RET))
    o = o_cross + (_dot_x2(qk, seg) * float(HEAD_DIM)) * v

    v4 = _dot_nt(jnp.where(own, v, 0.0).astype(BF), spread)
    pick = (lax.broadcasted_iota(I32, (D_RET, rows), 0) // HEAD_DIM
            == lax.broadcasted_iota(I32, (D_RET, rows), 1)).astype(BF)
    vexp = _dot(pick, v4.astype(BF))
    sto_ref[...] = st * gcol_ref[...] + kcol_ref[...] * vexp

    mu = _dot_x2(o, seg)
    d = o - mu
    var = _dot_x2(d * d, seg)
    y = d * lax.rsqrt(var + EPS)
    o_ref[...] = y[0:1] * jax.nn.silu(g_ref[...])


def _ret_decode(layer, q, k, v, g, state):
    nb = q.shape[0]
    log_gamma = np.log1p(-np.exp2(-5.0 - np.arange(RET_HEADS, dtype=np.float64)))
    gam = np.repeat(np.exp(log_gamma), HEAD_DIM)
    gcol = jnp.asarray(gam[:, None], F32)
    grow = jnp.asarray(gam[None, :], F32)
    kcol = k.reshape(nb, D_RET, 1)
    row = pl.BlockSpec((None, 1, D_RET), lambda bi: (bi, 0, 0))
    return pl.pallas_call(
        _ret_dec_kernel,
        out_shape=[jax.ShapeDtypeStruct((nb, 1, D_RET), F32), jax.ShapeDtypeStruct((nb, D_RET, HEAD_DIM), F32)],
        grid=(nb,),
        in_specs=[row, row, pl.BlockSpec((None, D_RET, 1), lambda bi: (bi, 0, 0)), row, row,
                  _const_spec(gcol.shape), _const_spec(grow.shape),
                  pl.BlockSpec((None, None, D_RET, HEAD_DIM), lambda bi: (layer, bi, 0, 0))],
        out_specs=[row, pl.BlockSpec((None, D_RET, HEAD_DIM), lambda bi: (bi, 0, 0))],
        compiler_params=_cparams(("parallel",)),
        name="ret_decode",
    )(q, k, kcol, v, g, gcol, grow, state)


def _swap_perm(rot_dim):
    half = rot_dim // 2
    p = np.arange(HEAD_DIM)
    p[:half] = np.arange(half, rot_dim)
    p[half:rot_dim] = np.arange(half)
    return p


def _swap_cols(start, width, rot_dim):
    p = _swap_perm(rot_dim)
    return np.concatenate([start + h * HEAD_DIM + p for h in range(width // HEAD_DIM)])


def _rope_tables(pos, rot_dim, theta, width):
    half = rot_dim // 2
    inv = jnp.exp(-math.log(theta) * jnp.arange(half, dtype=F32) / half)
    ang = pos.astype(F32)[:, None] * inv[None, :]
    cos, sin = jnp.cos(ang), jnp.sin(ang)
    ones = jnp.ones((pos.shape[0], HEAD_DIM - rot_dim), F32)
    c = jnp.concatenate([cos, cos, ones], axis=1)
    s = jnp.concatenate([-sin, sin, 0.0 * ones], axis=1)
    reps = width // HEAD_DIM
    return jnp.tile(c, (1, reps)), jnp.tile(s, (1, reps))


def _layer_params(l, norm1_g, w_in, nsa_q_norm, nsa_k_norm, cmp_pos_k, cmp_pos_v, cmp_w_k, cmp_w_v, w_out,
                  norm2_g, w_up, w_down):
    w = w_in[l]
    d = w.shape[0]
    wm = jnp.pad(w, ((0, 0), (0, _N_IN_PAD - _N_IN))).astype(BF)
    cols = np.concatenate([
        _swap_cols(_C_RQ, D_RET, HEAD_DIM), _swap_cols(_C_RK, D_RET, HEAD_DIM),
        _swap_cols(_C_NQ, D_NSA, ROT_DIM), _swap_cols(_C_CK, D_KV, ROT_DIM),
        _swap_cols(_C_SK, D_KV, ROT_DIM), _swap_cols(_C_WK, D_KV, ROT_DIM)])
    ws = w[:, cols].astype(BF)
    gains = jnp.concatenate([nsa_q_norm[l][None], nsa_k_norm[l]], axis=0)
    gn = jnp.tile(gains, (1, LANES // HEAD_DIM))
    gs = jnp.tile(gains[:, _swap_perm(ROT_DIM)], (1, LANES // HEAD_DIM))
    eye2 = jnp.eye(NSA_KV, dtype=F32)
    return dict(
        g1=norm1_g[l][None], wm=wm, ws=ws, gn=gn, gs=gs,
        pk=jnp.tile(cmp_pos_k[l], (1, NSA_KV)), pv=jnp.tile(cmp_pos_v[l], (1, NSA_KV)),
        pkt=jnp.tile(cmp_pos_k[l].T, (NSA_KV, CMP_SPAN // BLK_CMP)),
        pvt=jnp.tile(cmp_pos_v[l].T, (NSA_KV, CMP_SPAN // BLK_CMP)),
        phik=jnp.kron(eye2, cmp_w_k[l]), phiv=jnp.kron(eye2, cmp_w_v[l]),
        wo=w_out[l].astype(BF), g2=norm2_g[l][None], wu=w_up[l].astype(BF), wd=w_down[l].astype(BF))


def _prompt_layer(xp2d, prm, tabs, *, b, t, tm):
    (sbk_f, sbv_f, ck_f, cv_f, sk_f, sv_f, wk_f, wv_f,
     sbq_b, sbk_b, sbv_b, rq_b, rk_b, rv_b, rg_f, nq_b, sk_b, sv_b, wk_b, wv_b, ng_f) = _project(
        xp2d, prm['g1'], prm['wm'], prm['ws'], tabs, prm['gn'], prm['gs'], tm=tm, tab_blocks=t // tm, seqs=b)
    o_sb = _sb_prompt(sbq_b, sbk_b, sbv_b, b=b, t=t)
    o_ret, ret_st = _ret_prompt(rq_b, rk_b, rv_b, rg_f, b=b, t=t)
    kc, vc = _compress_prompt(ck_f, cv_f, prm['pkt'], prm['pvt'], prm['phik'], prm['phiv'])
    o_nsa = _nsa_prompt(nq_b, ng_f, kc, vc, sk_b, sv_b, wk_b, wv_b, b=b, t=t)
    y = _out_mlp(xp2d, o_sb, o_ret, o_nsa, prm['wo'], prm['g2'], prm['wu'], prm['wd'], tm=tm)
    keep = min(WINDOW, t)
    r4 = lambda a, h: jnp.transpose(a.reshape(b, h, HEAD_DIM, a.shape[-1]), (0, 3, 1, 2))
    st = ret_st.reshape(b, RET_HEADS, HEAD_DIM, RET_HEADS, HEAD_DIM)
    st = jnp.stack([st[:, h, :, h, :] for h in range(RET_HEADS)], axis=1)
    caches = dict(
        p_sb_k=r4(sbk_f, SB_HEADS), p_sb_v=r4(sbv_f, SB_HEADS),
        p_cmp_k=r4(ck_f, NSA_KV), p_cmp_v=r4(cv_f, NSA_KV),
        p_slc_k=r4(sk_f, NSA_KV), p_slc_v=r4(sv_f, NSA_KV),
        p_win_k=r4(wk_f[:, :, t - keep:], NSA_KV), p_win_v=r4(wv_f[:, :, t - keep:], NSA_KV),
        p_ret=st)
    return y, caches


def _sample_layer(l, xs2d, prm, tabs, caches, states, page_table, *, past_len):
    nb = xs2d.shape[0]
    (sbk_f, sbv_f, ck_f, cv_f, sk_f, sv_f, wk_f, wv_f,
     sbq_b, sbk_b, sbv_b, rq_b, rk_b, rv_b, rg_f, nq_b, sk_b, sv_b, wk_b, wv_b, ng_f) = _project(
        xs2d, prm['g1'], prm['wm'], prm['ws'], tabs, prm['gn'], prm['gs'], tm=nb, tab_blocks=1)
    row = lambda a: a.astype(F32).reshape(nb, 1, a.shape[-1])
    c_sb_k, c_sb_v, c_cmp_k, c_cmp_v, c_slc_k, c_slc_v = caches
    win_k, win_v, st_ret = states
    pg = c_cmp_k.shape[3]
    nblk = page_table.shape[1] * pg // BLK_CMP
    o_sb = _sb_decode(l, page_table, row(sbq_b), c_sb_k, c_sb_v)
    o_ret, ret_new = _ret_decode(l, row(rq_b), row(rk_b), row(rv_b), row(rg_f), st_ret)
    kc, vc = _cmp_decode(l, page_table, row(ck_f), row(cv_f), c_cmp_k, c_cmp_v,
                         prm['pk'], prm['pv'], prm['pkt'], prm['pvt'], prm['phik'], prm['phiv'])
    ocmp, idx = _sel_decode(row(nq_b), kc, vc, q_pos=past_len, nblk=nblk)
    idx = jnp.transpose(idx[:, :, :NSA_KV], (0, 2, 1)).reshape(nb, NSA_KV * TOP_N)
    o_nsa, win_k_new, win_v_new = _nsa_decode(
        l, page_table, idx, row(nq_b), row(ng_f), ocmp, row(sk_f), row(sv_f), row(wk_f), row(wv_f),
        win_k, win_v, c_slc_k, c_slc_v)
    y = _out_mlp(xs2d, o_sb.reshape(nb, D_SB), o_ret.reshape(nb, D_RET), o_nsa.reshape(nb, D_NSA),
                 prm['wo'], prm['g2'], prm['wu'], prm['wd'], tm=nb)
    r4 = lambda a, h: a.reshape(nb, 1, h, HEAD_DIM)
    wb = win_k.shape[3]
    win4 = lambda a: jnp.transpose(a.reshape(nb, NSA_KV, HEAD_DIM, wb), (0, 3, 1, 2))
    out = dict(
        s_sb_k=r4(sbk_f, SB_HEADS), s_sb_v=r4(sbv_f, SB_HEADS),
        s_cmp_k=r4(ck_f, NSA_KV), s_cmp_v=r4(cv_f, NSA_KV),
        s_slc_k=r4(sk_f, NSA_KV), s_slc_v=r4(sv_f, NSA_KV),
        s_win_k=win4(win_k_new), s_win_v=win4(win_v_new),
        s_ret=ret_new.reshape(nb, RET_HEADS, HEAD_DIM, HEAD_DIM))
    return y, out


def kernel(x_prompt, x_sample, cache_sb_k, cache_sb_v, cache_cmp_k, cache_cmp_v, cache_slc_k, cache_slc_v,
           state_win_k, state_win_v, state_ret, page_table, norm1_g, w_in, nsa_q_norm, nsa_k_norm, cmp_pos_k,
           cmp_pos_v, cmp_w_k, cmp_w_v, w_out, norm2_g, w_up, w_down):
    b, t, d = x_prompt.shape
    depth = w_in.shape[0]
    tm = min(512, t)
    pos_p = jnp.arange(t, dtype=I32)
    tabs_p = (*_rope_tables(pos_p, HEAD_DIM, RET_THETA, D_RET), *_rope_tables(pos_p, ROT_DIM, ROPE_THETA, LANES))
    xp = x_prompt.reshape(b * t, d)

    nb, n_new, _ = x_sample.shape
    assert n_new == 1, "sample group kernels handle one new token per sample"
    pg = cache_sb_k.shape[2]
    past_len = page_table.shape[1] * pg
    wb = state_win_k.shape[2]
    assert wb <= WINDOW and wb <= past_len
    pos_s = jnp.full((nb,), past_len, dtype=I32)
    tabs_s = (*_rope_tables(pos_s, HEAD_DIM, RET_THETA, D_RET), *_rope_tables(pos_s, ROT_DIM, ROPE_THETA, LANES))
    xs = x_sample.reshape(nb, d)
    caches = tuple(_pages_t(c) for c in (cache_sb_k, cache_sb_v, cache_cmp_k, cache_cmp_v, cache_slc_k, cache_slc_v))
    states = (_pages_t(state_win_k), _pages_t(state_win_v),
              state_ret.reshape(depth, nb, RET_HEADS * HEAD_DIM, HEAD_DIM))

    new = {}
    for l in range(depth):
        prm = _layer_params(l, norm1_g, w_in, nsa_q_norm, nsa_k_norm, cmp_pos_k, cmp_pos_v, cmp_w_k, cmp_w_v,
                            w_out, norm2_g, w_up, w_down)
        xp, p_new = _prompt_layer(xp, prm, tabs_p, b=b, t=t, tm=tm)
        xs, s_new = _sample_layer(l, xs, prm, tabs_s, caches, states, page_table, past_len=past_len)
        for name, val in {**p_new, **s_new}.items():
            new.setdefault(name, []).append(val)
    st = lambda name: jnp.stack(new[name])
    return (xp.reshape(b, t, d), xs.reshape(nb, 1, d)) + tuple(st(nm) for nm in (
        'p_sb_k', 'p_sb_v', 'p_cmp_k', 'p_cmp_v', 'p_slc_k', 'p_slc_v', 'p_win_k', 'p_win_v', 'p_ret',
        's_sb_k', 's_sb_v', 's_cmp_k', 's_cmp_v', 's_slc_k', 's_slc_v', 's_win_k', 's_win_v', 's_ret'))
```

```python
import functools
import math

import numpy as np
import jax
import jax.numpy as jnp
from jax import lax
from jax.experimental import pallas as pl
from jax.experimental.pallas import tpu as pltpu

HEAD_DIM = 64
SB_HEADS = 4
RET_HEADS = 4
NSA_HEADS = 8
NSA_KV = 2
NSA_REP = NSA_HEADS // NSA_KV
D_SB = SB_HEADS * HEAD_DIM
D_RET = RET_HEADS * HEAD_DIM
D_NSA = NSA_HEADS * HEAD_DIM
D_KV = NSA_KV * HEAD_DIM
ROPE_THETA = 500000.0
ROT_DIM = HEAD_DIM // 4
RET_THETA = 10000.0
BLK_CMP = 32
BLK_SEL = 64
TOP_N = 16
WINDOW = 512
FORCE_SCORE = 1.0e4
NEG = -1.0e30
EPS = 1e-6
QK_SCALE = HEAD_DIM ** -0.5

LANES = 128
VMEM_LIMIT = 56 * 1024 * 1024

BF = jnp.bfloat16
F32 = jnp.float32
I32 = jnp.int32

_C_SBQ, _C_SBK, _C_SBV = 0, 256, 512
_C_RQ, _C_RK, _C_RV, _C_RG = 768, 1024, 1280, 1536
_C_NQ = 1792
_C_CK, _C_CV, _C_SK, _C_SV, _C_WK, _C_WV = 2304, 2432, 2560, 2688, 2816, 2944
_C_NG = 3072
_N_IN = 3096
_N_IN_PAD = 3200
_S_RQ, _S_RK, _S_NQ, _S_CK, _S_SK, _S_WK = 0, 256, 512, 1024, 1152, 1280
_N_SWAP = 1408


def _dot(a, b):
    return jnp.dot(a, b, preferred_element_type=F32)


def _dot_nt(a, b):
    return lax.dot_general(a, b, (((1,), (1,)), ((), ())), preferred_element_type=F32)


def _dot_tn(a, b):
    return lax.dot_general(a, b, (((0,), (0,)), ((), ())), preferred_element_type=F32)


def _split2(x):
    hi = x.astype(BF)
    lo = (x - hi.astype(F32)).astype(BF)
    return hi, lo


def _split3(x):
    hi = x.astype(BF)
    r = x - hi.astype(F32)
    mid = r.astype(BF)
    lo = (r - mid.astype(F32)).astype(BF)
    return hi, mid, lo


def _dot_x2(x, w):
    hi, lo = _split2(x)
    return _dot(hi, w) + _dot(lo, w)


def _seg_mean_mat(n):
    r = lax.broadcasted_iota(I32, (n, n), 0) // HEAD_DIM
    c = lax.broadcasted_iota(I32, (n, n), 1) // HEAD_DIM
    return jnp.where(r == c, 1.0 / HEAD_DIM, 0.0).astype(BF)


def _cparams(sem, vmem=VMEM_LIMIT):
    return pltpu.CompilerParams(dimension_semantics=sem, vmem_limit_bytes=vmem)


def _const_spec(shape):
    nd = len(shape)
    return pl.BlockSpec(shape, lambda *a: (0,) * nd)


def _proj_kernel(x_ref, g1_ref, wm_ref, ws_ref, cr_ref, sr_ref, cn_ref, sn_ref, gn_ref, gs_ref,
                 sbk_f, sbv_f, ck_f, cv_f, sk_f, sv_f, wk_f, wv_f,
                 sbq_b, sbk_b, sbv_b, rq_b, rk_b, rv_b, rg_f, nq_b, sk_b, sv_b, wk_b, wv_b, ng_f, *, t_out):
    x = x_ref[...]
    ms = jnp.mean(x * x, axis=-1, keepdims=True)
    xn = (x * lax.rsqrt(ms + EPS) * g1_ref[...]).astype(BF)

    def put(ref, val):
        ref[...] = jnp.transpose(val) if t_out else val

    def mm(w_ref, lo, n):
        return _dot(xn, w_ref[:, lo:lo + n])

    sbq_b[...] = (mm(wm_ref, _C_SBQ, D_SB) * QK_SCALE).astype(BF)
    k = mm(wm_ref, _C_SBK, D_SB)
    put(sbk_f, k)
    sbk_b[...] = k.astype(BF)
    v = mm(wm_ref, _C_SBV, D_SB)
    put(sbv_f, v)
    sbv_b[...] = v.astype(BF)

    cr = cr_ref[...]
    sr = sr_ref[...]
    rq_b[...] = (mm(wm_ref, _C_RQ, D_RET) * cr + mm(ws_ref, _S_RQ, D_RET) * sr).astype(BF)
    rk_b[...] = ((mm(wm_ref, _C_RK, D_RET) * cr + mm(ws_ref, _S_RK, D_RET) * sr) * QK_SCALE).astype(BF)
    rv_b[...] = mm(wm_ref, _C_RV, D_RET).astype(BF)
    rg_f[...] = mm(wm_ref, _C_RG, D_RET)

    seg = _seg_mean_mat(LANES)
    cn = cn_ref[...]
    sn = sn_ref[...]

    def normrope(cm, cs, gi):
        y = mm(wm_ref, cm, LANES)
        ysw = mm(ws_ref, cs, LANES)
        r = lax.rsqrt(_dot_x2(y * y, seg) + EPS)
        g = gn_ref[gi:gi + 1, :]
        gsw = gs_ref[gi:gi + 1, :]
        return r * (y * (g * cn) + ysw * (gsw * sn))

    for c in range(D_NSA // LANES):
        nq_b[:, c * LANES:(c + 1) * LANES] = (
            normrope(_C_NQ + c * LANES, _S_NQ + c * LANES, 0) * QK_SCALE).astype(BF)
    put(ck_f, normrope(_C_CK, _S_CK, 1))
    put(cv_f, mm(wm_ref, _C_CV, D_KV))
    k = normrope(_C_SK, _S_SK, 2)
    put(sk_f, k)
    sk_b[...] = k.astype(BF)
    v = mm(wm_ref, _C_SV, D_KV)
    put(sv_f, v)
    sv_b[...] = v.astype(BF)
    k = normrope(_C_WK, _S_WK, 3)
    put(wk_f, k)
    wk_b[...] = k.astype(BF)
    v = mm(wm_ref, _C_WV, D_KV)
    put(wv_f, v)
    wv_b[...] = v.astype(BF)
    ng_f[...] = jax.nn.sigmoid(mm(wm_ref, _C_NG, LANES))


def _project(x2d, g1, wm, ws, tabs, gn, gs, *, tm, tab_blocks, seqs=None):
    n, d = x2d.shape
    cr, sr, cn, sn = tabs
    grid = (n // tm,)
    tok = lambda w: pl.BlockSpec((tm, w), lambda i: (i, 0))
    tab = lambda w: pl.BlockSpec((tm, w), lambda i: (i % tab_blocks, 0))
    f32 = lambda w: jax.ShapeDtypeStruct((n, w), F32)
    bf = lambda w: jax.ShapeDtypeStruct((n, w), BF)
    out_w_f = [D_SB, D_SB] + [D_KV] * 6
    if seqs is None:
        cache_shapes = [f32(w) for w in out_w_f]
        cache_specs = [tok(w) for w in out_w_f]
    else:
        t = n // seqs
        per = t // tm
        cache_shapes = [jax.ShapeDtypeStruct((seqs, w, t), F32) for w in out_w_f]
        cache_specs = [pl.BlockSpec((None, w, tm), lambda i: (i // per, 0, i % per)) for w in out_w_f]
    out_shape = (cache_shapes
                 + [bf(D_SB)] * 3 + [bf(D_RET)] * 3 + [f32(D_RET), bf(D_NSA)] + [bf(D_KV)] * 4 + [f32(LANES)])
    out_w = [D_SB] * 3 + [D_RET] * 3 + [D_RET, D_NSA] + [D_KV] * 4 + [LANES]
    return pl.pallas_call(
        functools.partial(_proj_kernel, t_out=seqs is not None),
        out_shape=out_shape,
        grid=grid,
        in_specs=[tok(d), _const_spec((1, d)), _const_spec(wm.shape), _const_spec(ws.shape),
                  tab(D_RET), tab(D_RET), tab(LANES), tab(LANES),
                  _const_spec(gn.shape), _const_spec(gs.shape)],
        out_specs=cache_specs + [tok(w) for w in out_w],
        compiler_params=_cparams(("parallel",)),
        name="proj",
    )(x2d, g1, wm, ws, cr, sr, cn, sn, gn, gs)


SB_DEAD = -120.0


def _softplus(z):
    return jnp.maximum(z, 0.0) + jnp.log(1.0 + jnp.exp(-jnp.abs(z)))


def _sb_kernel(q_ref, k_ref, v_ref, o_ref, c_ref, acc_ref, *, tq, tk):
    i = pl.program_id(1)
    m = SB_HEADS * tq
    q = q_ref[...]
    lane_h = lax.broadcasted_iota(I32, (tq, D_SB), 1) // HEAD_DIM
    qs = jnp.concatenate([jnp.where(lane_h == h, q, jnp.zeros_like(q)) for h in range(SB_HEADS)], axis=0)
    q_pos = i * tq + lax.broadcasted_iota(I32, (m, tk), 0) % tq
    col = lax.broadcasted_iota(I32, (m, tk), 1)
    tri = (lax.broadcasted_iota(I32, (tk, tk), 0) > lax.broadcasted_iota(I32, (tk, tk), 1)).astype(BF)
    nt = (i * tq) // tk + 1

    c_ref[...] = jnp.zeros_like(c_ref)
    acc_ref[...] = jnp.zeros_like(acc_ref)

    def tile(j, masked):
        off = pl.multiple_of(j * tk, tk)
        z = _dot_nt(qs, k_ref[pl.ds(off, tk), :])
        l1m = -_softplus(z)
        if masked:
            mask = (off + col) < q_pos
            l1m = jnp.where(mask, l1m, 0.0)
        c = c_ref[...]
        after = c + _dot_x2(l1m, tri)
        lw = z + l1m + after
        if masked:
            lw = jnp.where(mask, lw, NEG)
        w = jnp.exp(lw)
        acc_ref[...] += _dot(w.astype(BF), v_ref[pl.ds(off, tk), :])
        c_ref[...] = c + jnp.sum(l1m, axis=1, keepdims=True)

    tile(nt - 1, True)

    def alive():
        return jnp.max(c_ref[...]) > SB_DEAD

    def cond(state):
        return (state[0] < nt - 1) & state[1]

    def body(state):
        tile(nt - 2 - state[0], False)
        return state[0] + 1, alive()

    lax.while_loop(cond, body, (jnp.int32(0), alive()))

    acc = acc_ref[...]
    out = jnp.zeros((tq, D_SB), F32)
    for h in range(SB_HEADS):
        out = out + jnp.where(lane_h == h, acc[h * tq:(h + 1) * tq], 0.0)
    o_ref[...] = out.astype(BF)


def _sb_prompt(q, k, v, *, b, t, tq=128, tk=256):
    n = b * t
    nq = t // tq
    return pl.pallas_call(
        functools.partial(_sb_kernel, tq=tq, tk=tk),
        out_shape=jax.ShapeDtypeStruct((n, D_SB), BF),
        grid=(b, nq),
        in_specs=[pl.BlockSpec((tq, D_SB), lambda bi, i: (bi * nq + i, 0)),
                  pl.BlockSpec((t, D_SB), lambda bi, i: (bi, 0)),
                  pl.BlockSpec((t, D_SB), lambda bi, i: (bi, 0))],
        out_specs=pl.BlockSpec((tq, D_SB), lambda bi, i: (bi * nq + i, 0)),
        scratch_shapes=[pltpu.VMEM((SB_HEADS * tq, 1), F32), pltpu.VMEM((SB_HEADS * tq, D_SB), F32)],
        compiler_params=_cparams(("parallel", "parallel")),
        name="sb_prompt",
    )(q, k, v)


def _ret_kernel(q_ref, k_ref, v_ref, g_ref, dec_ref, qd_ref, kd_ref, gc_ref, o_ref, st_ref, s_scr, *, c):
    ci = pl.program_id(1)

    @pl.when(ci == 0)
    def _():
        s_scr[...] = jnp.zeros_like(s_scr)

    q = q_ref[...]
    k = k_ref[...]
    v = v_ref[...]
    lane_h = lax.broadcasted_iota(I32, (c, D_RET), 1) // HEAD_DIM
    o = jnp.zeros((c, D_RET), F32)
    for h in range(RET_HEADS):
        kh = jnp.where(lane_h == h, k, jnp.zeros_like(k))
        vh = jnp.where(lane_h == h, v, jnp.zeros_like(v))
        s = _dot_nt(q, kh) * dec_ref[h]
        o = o + _dot(s.astype(BF), vh)
    st = s_scr[...]
    shi, slo = _split2(st)
    o = o + (_dot(q, shi) + _dot(q, slo)) * qd_ref[...]

    kd = (k.astype(F32) * kd_ref[...]).astype(BF)
    ktv = _dot_tn(kd, v)
    r = lax.broadcasted_iota(I32, (D_RET, D_RET), 0) // HEAD_DIM
    cc = lax.broadcasted_iota(I32, (D_RET, D_RET), 1) // HEAD_DIM
    new_st = st * gc_ref[...] + jnp.where(r == cc, ktv, 0.0)
    s_scr[...] = new_st
    st_ref[...] = new_st

    seg = _seg_mean_mat(D_RET)
    mu = _dot_x2(o, seg)
    d = o - mu
    var = _dot_x2(d * d, seg)
    y = d * lax.rsqrt(var + EPS)
    o_ref[...] = (y * jax.nn.silu(g_ref[...])).astype(BF)


def _ret_tables(c):
    log_gamma = np.log1p(-np.exp2(-5.0 - np.arange(RET_HEADS, dtype=np.float64)))
    idx = np.arange(c, dtype=np.float64)
    diff = idx[:, None] - idx[None, :]
    dec = np.where(diff >= 0, np.exp(np.maximum(diff, 0.0)[None] * log_gamma[:, None, None]), 0.0)
    lane_lg = np.repeat(log_gamma, HEAD_DIM)
    qd = np.exp((idx[:, None] + 1.0) * lane_lg[None, :])
    kd = np.exp((c - 1.0 - idx)[:, None] * lane_lg[None, :])
    gc = np.exp(c * lane_lg)[None, :]
    f = lambda a: jnp.asarray(a, F32)
    return f(dec), f(qd), f(kd), f(gc)


def _ret_prompt(q, k, v, g, *, b, t, c=256):
    n = b * t
    nc = t // c
    dec, qd, kd, gc = _ret_tables(c)
    tok = pl.BlockSpec((c, D_RET), lambda bi, i: (bi * nc + i, 0))
    o, st = pl.pallas_call(
        functools.partial(_ret_kernel, c=c),
        out_shape=[jax.ShapeDtypeStruct((n, D_RET), BF), jax.ShapeDtypeStruct((b, D_RET, D_RET), F32)],
        grid=(b, nc),
        in_specs=[tok, tok, tok, tok, _const_spec(dec.shape), _const_spec(qd.shape), _const_spec(kd.shape),
                  _const_spec(gc.shape)],
        out_specs=[tok, pl.BlockSpec((None, D_RET, D_RET), lambda bi, i: (bi, 0, 0))],
        scratch_shapes=[pltpu.VMEM((D_RET, D_RET), F32)],
        compiler_params=_cparams(("parallel", "arbitrary")),
        name="ret_prompt",
    )(q, k, v, g, dec, qd, kd, gc)
    return o, st


CMP_SPAN = 1024


def _compress_t(xt, post, phi_hi, phi_lo):
    n = xt.shape[1]
    sel = (lax.broadcasted_iota(I32, (n, LANES), 0) // BLK_CMP
           == lax.broadcasted_iota(I32, (n, LANES), 1)).astype(BF)
    xs_t = _dot_x2(xt * post, sel)
    hi, lo = _split2(jnp.transpose(xs_t))
    out = _dot(hi, phi_hi) + _dot(lo, phi_hi) + _dot(hi, phi_lo)
    return out[0:n // BLK_CMP]


def _compress_kernel(k_ref, v_ref, pk_ref, pv_ref, wk_ref, wv_ref, ko_ref, vo_ref):
    wk_hi, wk_lo = _split2(wk_ref[...])
    wv_hi, wv_lo = _split2(wv_ref[...])
    ko_ref[...] = _compress_t(k_ref[...], pk_ref[...], wk_hi, wk_lo).astype(BF)
    vo_ref[...] = _compress_t(v_ref[...], pv_ref[...], wv_hi, wv_lo).astype(BF)


def _compress_prompt(ckt, cvt, pkt, pvt, wk, wv):
    b, _, t = ckt.shape
    steps = t // CMP_SPAN
    tok = pl.BlockSpec((None, LANES, CMP_SPAN), lambda bi, i: (bi, 0, i))
    out = pl.BlockSpec((CMP_SPAN // BLK_CMP, LANES), lambda bi, i: (bi * steps + i, 0))
    return pl.pallas_call(
        _compress_kernel,
        out_shape=[jax.ShapeDtypeStruct((b * t // BLK_CMP, LANES), BF)] * 2,
        grid=(b, steps),
        in_specs=[tok, tok, _const_spec(pkt.shape), _const_spec(pvt.shape), _const_spec(wk.shape),
                  _const_spec(wv.shape)],
        out_specs=[out, out],
        compiler_params=_cparams(("parallel", "parallel")),
        name="compress_prompt",
    )(ckt, cvt, pkt, pvt, wk, wv)


def _stack_heads_q(q, extra):
    tq = q.shape[0]
    lane = lax.broadcasted_iota(I32, (tq, LANES), 1)
    rows = []
    for h in range(NSA_HEADS):
        g = h // NSA_REP
        blk = q[:, (h // 2) * LANES:(h // 2 + 1) * LANES]
        src_half = h % 2
        if src_half != g:
            blk32 = pltpu.roll(blk.astype(F32), HEAD_DIM, 1).astype(BF)
        else:
            blk32 = blk
        keep = (lane // HEAD_DIM) == g
        rows.append(jnp.where(keep, blk32, jnp.zeros_like(blk32)))
    qs = jnp.concatenate(rows, axis=0)
    if extra is not None:
        qs = jnp.concatenate([qs, extra], axis=1)
    return qs


def _unstack_heads(acc, tq):
    lane = lax.broadcasted_iota(I32, (tq, LANES), 1)
    cols = []
    for c in range(NSA_HEADS // 2):
        g = (2 * c) // NSA_REP
        a = acc[(2 * c) * tq:(2 * c + 1) * tq]
        b = acc[(2 * c + 1) * tq:(2 * c + 2) * tq]
        if g == 0:
            cols.append(jnp.where(lane < HEAD_DIM, a, pltpu.roll(b, HEAD_DIM, 1)))
        else:
            cols.append(jnp.where(lane < HEAD_DIM, pltpu.roll(a, HEAD_DIM, 1), b))
    return jnp.concatenate(cols, axis=1)


def _order_key(v):
    return v


def _beats(key_j, key, j_before):
    return jnp.where(j_before, (key_j >= key).astype(I32), (key_j > key).astype(I32))


def _gate_expand(gates, branch):
    r = lax.broadcasted_iota(I32, (LANES, D_NSA), 0)
    c = lax.broadcasted_iota(I32, (LANES, D_NSA), 1)
    e = (r == (c // HEAD_DIM) * 3 + branch).astype(BF)
    return _dot_x2(gates, e)


def _nsa_kernel(q_ref, gate_ref, kc_ref, vc_ref, sk_ref, sv_ref, wk_ref, wv_ref, o_ref,
                k2_scr, svt_scr, wvt_scr, imp_scr, m_scr, l_scr, acc_scr, sa_scr, sb_scr, *, tq, tk, tw, t):
    i = pl.program_id(1)
    nc = t // BLK_CMP
    ns = t // BLK_SEL
    m = NSA_HEADS * tq

    @pl.when(i == 0)
    def _():
        rows = 512
        for r0 in range(0, t, rows):
            s_idx = r0 + lax.broadcasted_iota(I32, (rows, LANES), 0)
            c_idx = lax.broadcasted_iota(I32, (rows, LANES), 1)
            e = (s_idx // BLK_SEL == c_idx).astype(BF)
            k2_scr[r0:r0 + rows, :] = jnp.concatenate([sk_ref[r0:r0 + rows, :], e], axis=1)
        for c in range(t // tk):
            svt_scr[c] = jnp.transpose(sv_ref[c * tk:(c + 1) * tk, :].astype(F32)).astype(BF)
        for c in range(t // tw):
            wvt_scr[c] = jnp.transpose(wv_ref[c * tw:(c + 1) * tw, :].astype(F32)).astype(BF)

    q = q_ref[...]
    qs = _stack_heads_q(q, None)

    kc = kc_ref[...]
    vct = jnp.transpose(vc_ref[...].astype(F32)).astype(BF)
    n_idx = lax.broadcasted_iota(I32, (nc, m), 0)
    t_idx = i * tq + lax.broadcasted_iota(I32, (nc, m), 1) % tq
    valid = (n_idx + 1) * BLK_CMP - 1 <= t_idx
    st = jnp.where(valid, _dot_nt(kc, qs), NEG)
    mx = jnp.max(st, axis=0, keepdims=True)
    e = jnp.where(valid, jnp.exp(st - mx), 0.0)
    den = jnp.sum(e, axis=0, keepdims=True)
    p = e / jnp.where(den > 0.0, den, 1.0)
    o_cmp = _dot(vct, p.astype(BF))
    imp = []
    for g in range(NSA_KV):
        acc_g = p[:, g * NSA_REP * tq:(g * NSA_REP + 1) * tq]
        for h in range(g * NSA_REP + 1, (g + 1) * NSA_REP):
            acc_g = acc_g + p[:, h * tq:(h + 1) * tq]
        imp.append(acc_g)

    pair = (lax.broadcasted_iota(I32, (ns, nc), 1) // (BLK_SEL // BLK_CMP)
            == lax.broadcasted_iota(I32, (ns, nc), 0)).astype(BF)
    blk = lax.broadcasted_iota(I32, (ns, tq), 0)
    tb = i * tq + lax.broadcasted_iota(I32, (ns, tq), 1)
    forced = (blk == 0) | (blk == tb // BLK_SEL)
    future = blk * BLK_SEL > tb
    eye = (lax.broadcasted_iota(I32, (tq, tq), 0) == lax.broadcasted_iota(I32, (tq, tq), 1)).astype(BF)
    pens = []
    for g in range(NSA_KV):
        a, b, c = _split3(imp[g])
        v = _dot(pair, a) + _dot(pair, b) + _dot(pair, c)
        v = jnp.where(forced, FORCE_SCORE, jnp.where(future, -1.0, v))
        key = _order_key(v)
        imp_scr[g] = key

        def body(j, cnt, g=g, key=key):
            return cnt + _beats(imp_scr[g, pl.ds(j, 1), :], key, blk > j)

        cnt = lax.fori_loop(0, ns, body, jnp.zeros((ns, tq), I32), unroll=8)
        sel_t = (cnt < min(TOP_N, ns)).astype(BF)
        pen = _dot_nt(eye, sel_t)
        pen = ((pen - 1.0) * (-NEG)).astype(BF)
        if ns < LANES:
            pen = jnp.concatenate([pen, jnp.zeros((tq, LANES - ns), BF)], axis=1)
        pens.append(pen)
    pen_rows = jnp.concatenate([pens[h // NSA_REP] for h in range(NSA_HEADS)], axis=0)
    qs2 = jnp.concatenate([qs, pen_rows], axis=1)

    q_pos = i * tq + lax.broadcasted_iota(I32, (1, m), 1) % tq

    def online(scores, vals_t, mask):
        if mask is not None:
            scores = jnp.where(mask, scores, NEG)
        m_old = m_scr[...]
        m_new = jnp.maximum(m_old, jnp.max(scores, axis=0, keepdims=True))
        alpha = jnp.exp(m_old - m_new)
        p = jnp.exp(scores - m_new)
        l_scr[...] = alpha * l_scr[...] + jnp.sum(p, axis=0, keepdims=True)
        acc_scr[...] = alpha * acc_scr[...] + _dot(vals_t, p.astype(BF))
        m_scr[...] = m_new

    def reset():
        m_scr[...] = jnp.full_like(m_scr, NEG)
        l_scr[...] = jnp.zeros_like(l_scr)
        acc_scr[...] = jnp.zeros_like(acc_scr)

    def result():
        return acc_scr[...] / l_scr[...]

    reset()
    nt = (i * tq) // tk + 1
    row_k = lax.broadcasted_iota(I32, (tk, 1), 0)

    def scores_into(buf, j):
        off = pl.multiple_of(j * tk, tk)
        buf[...] = _dot_nt(k2_scr[pl.ds(off, tk), :], qs2)

    def diag_mask():
        return ((nt - 1) * tk + row_k) <= q_pos

    n_pairs = (nt - 1) // 2
    scores_into(sa_scr, 0)

    def slc_body(pr, carry):
        scores_into(sb_scr, 2 * pr + 1)
        online(sa_scr[...], svt_scr[2 * pr], None)
        scores_into(sa_scr, 2 * pr + 2)
        online(sb_scr[...], svt_scr[2 * pr + 1], None)
        return carry

    lax.fori_loop(0, n_pairs, slc_body, 0)

    @pl.when((nt - 1) % 2 == 1)
    def _():
        scores_into(sb_scr, nt - 1)
        online(sa_scr[...], svt_scr[nt - 2], None)
        online(sb_scr[...], svt_scr[nt - 1], diag_mask())

    @pl.when((nt - 1) % 2 == 0)
    def _():
        online(sa_scr[...], svt_scr[nt - 1], diag_mask())

    o_slc = result()

    row_w = lax.broadcasted_iota(I32, (tw, 1), 0)
    n_band = (WINDOW + tq + tw - 1) // tw
    first = jnp.maximum(i * tq - WINDOW, 0) // tw
    s_w = []
    for c in range(n_band):
        off = pl.multiple_of((first + c) * tw, tw)
        k_pos = off + row_w
        s = _dot_nt(wk_ref[pl.ds(off, tw), :], qs)
        s_w.append(jnp.where((k_pos <= q_pos) & (q_pos - k_pos <= WINDOW), s, NEG))
    m_w = s_w[0].max(axis=0, keepdims=True)
    for s in s_w[1:]:
        m_w = jnp.maximum(m_w, s.max(axis=0, keepdims=True))
    l_w = jnp.zeros((1, m), F32)
    o_win = jnp.zeros((LANES, m), F32)
    for c, s in enumerate(s_w):
        p = jnp.exp(s - m_w)
        l_w = l_w + p.sum(axis=0, keepdims=True)
        o_win = o_win + _dot(wvt_scr[first + c], p.astype(BF))
    o_win = o_win / l_w

    gt = jnp.transpose(gate_ref[...])

    def gate_row(branch):
        return jnp.concatenate([gt[h * 3 + branch:h * 3 + branch + 1, :] for h in range(NSA_HEADS)], axis=1)

    mix = gate_row(0) * o_cmp + gate_row(1) * o_slc + gate_row(2) * o_win
    stacked = jnp.concatenate([jnp.transpose(mix[:, h * tq:(h + 1) * tq]) for h in range(NSA_HEADS)], axis=0)
    o_ref[...] = _unstack_heads(stacked, tq).astype(BF)


def _nsa_prompt(nq, gates, kc, vc, sk, sv, wk, wv, *, b, t, tq=128, tk=512, tw=128):
    assert t % tk == 0 and (WINDOW + tq + tw - 1) // tw <= t // tw
    n = b * t
    nqb = t // tq
    nc = t // BLK_CMP
    ns = t // BLK_SEL
    m = NSA_HEADS * tq
    tok = lambda w: pl.BlockSpec((tq, w), lambda bi, i: (bi * nqb + i, 0))
    seq = lambda rows, w: pl.BlockSpec((rows, w), lambda bi, i: (bi, 0))
    return pl.pallas_call(
        functools.partial(_nsa_kernel, tq=tq, tk=tk, tw=tw, t=t),
        out_shape=jax.ShapeDtypeStruct((n, D_NSA), BF),
        grid=(b, nqb),
        in_specs=[tok(D_NSA), tok(LANES), seq(nc, LANES), seq(nc, LANES),
                  seq(t, LANES), seq(t, LANES), seq(t, LANES), seq(t, LANES)],
        out_specs=tok(D_NSA),
        scratch_shapes=[pltpu.VMEM((t, 2 * LANES), BF), pltpu.VMEM((t // tk, LANES, tk), BF),
                        pltpu.VMEM((t // tw, LANES, tw), BF), pltpu.VMEM((NSA_KV, ns, tq), F32),
                        pltpu.VMEM((1, m), F32), pltpu.VMEM((1, m), F32), pltpu.VMEM((LANES, m), F32),
                        pltpu.VMEM((tk, m), F32), pltpu.VMEM((tk, m), F32)],
        compiler_params=_cparams(("parallel", "arbitrary")),
        name="nsa_prompt",
    )(nq, gates, kc, vc, sk, sv, wk, wv)


def _out_mlp_kernel(x_ref, osb_ref, oret_ref, onsa_ref, wo_ref, g2_ref, wu_ref, wd_ref, y_ref, *, ff_chunk):
    h = (x_ref[...] + _dot(osb_ref[...].astype(BF), wo_ref[0:D_SB, :])
         + _dot(oret_ref[...].astype(BF), wo_ref[D_SB:D_SB + D_RET, :])
         + _dot(onsa_ref[...].astype(BF), wo_ref[D_SB + D_RET:, :]))
    ms = jnp.mean(h * h, axis=-1, keepdims=True)
    hn = (h * lax.rsqrt(ms + EPS) * g2_ref[...]).astype(BF)
    mlp = None
    d_ff = wu_ref.shape[1]
    for c0 in range(0, d_ff, ff_chunk):
        u = jnp.maximum(_dot(hn, wu_ref[:, c0:c0 + ff_chunk]), 0.0)
        part = _dot((u * u).astype(BF), wd_ref[c0:c0 + ff_chunk, :])
        mlp = part if mlp is None else mlp + part
    y_ref[...] = h + mlp


def _out_mlp(x2d, osb, oret, onsa, wo, g2, wu, wd, *, tm, ff_chunk=1024):
    n, d = x2d.shape
    tok = lambda w: pl.BlockSpec((tm, w), lambda i: (i, 0))
    return pl.pallas_call(
        functools.partial(_out_mlp_kernel, ff_chunk=ff_chunk),
        out_shape=jax.ShapeDtypeStruct((n, d), F32),
        grid=(n // tm,),
        in_specs=[tok(d), tok(D_SB), tok(D_RET), tok(D_NSA), _const_spec(wo.shape), _const_spec((1, d)),
                  _const_spec(wu.shape), _const_spec(wd.shape)],
        out_specs=tok(d),
        compiler_params=_cparams(("parallel",)),
        name="out_mlp",
    )(x2d, osb, oret, onsa, wo, g2, wu, wd)


PAGES_PER_STEP = 8
DEC_ROWS = 16


def _pages_t(c):
    dp, npool, pg, h, dd = c.shape
    return jnp.transpose(c, (0, 1, 3, 4, 2)).reshape(dp, npool, h * dd, pg)


def _page_spec(layer, width, page_of_step, pg):
    def imap(bi, p, pt, *rest):
        return (layer, pt[bi, page_of_step(p)], 0, 0)
    return pl.BlockSpec((None, None, width, pg), imap)


def _sb_dec_kernel(pt_ref, q_ref, *refs, npp, last):
    k_refs, v_refs = refs[:npp], refs[npp:2 * npp]
    o_ref, c_scr, acc_scr, alive_scr = refs[2 * npp:]
    p = pl.program_id(1)
    rows = DEC_ROWS
    pg = k_refs[0].shape[1]

    @pl.when(p == 0)
    def _():
        c_scr[...] = jnp.zeros_like(c_scr)
        acc_scr[...] = jnp.zeros_like(acc_scr)
        alive_scr[0] = 1

    lane_h = lax.broadcasted_iota(I32, (rows, D_SB), 1) // HEAD_DIM
    row = lax.broadcasted_iota(I32, (rows, D_SB), 0)

    @pl.when(alive_scr[0] == 1)
    def _():
        q = jnp.broadcast_to(q_ref[...].astype(F32), (rows, D_SB))
        qs = jnp.where(lane_h == row, q, 0.0).astype(BF)
        tri = (lax.broadcasted_iota(I32, (pg, pg), 0) > lax.broadcasted_iota(I32, (pg, pg), 1)).astype(BF)
        c = c_scr[...]
        acc = acc_scr[...]
        for kk in range(npp):
            z = _dot(qs, k_refs[kk][...].astype(BF))
            l1m = -_softplus(z)
            after = c + _dot_x2(l1m, tri)
            w = jnp.exp(z + l1m + after)
            acc = acc + _dot_nt(w.astype(BF), v_refs[kk][...].astype(BF))
            c = c + jnp.sum(l1m, axis=1, keepdims=True)
        c_scr[...] = c
        acc_scr[...] = acc
        head_rows = lax.broadcasted_iota(I32, (rows, 1), 0) < SB_HEADS
        alive_scr[0] = (jnp.max(jnp.where(head_rows, c, NEG)) > SB_DEAD).astype(I32)

    @pl.when(p == last)
    def _():
        o_ref[...] = jnp.sum(jnp.where(lane_h == row, acc_scr[...], 0.0), axis=0, keepdims=True)


def _sb_decode(layer, page_table, q, cache_k, cache_v):
    nb, npg = page_table.shape
    npp = math.gcd(PAGES_PER_STEP, npg)
    steps = npg // npp
    pg = cache_k.shape[3]
    specs_k = [_page_spec(layer, D_SB, (lambda p, kk=kk: npg - 1 - (p * npp + kk)), pg) for kk in range(npp)]
    specs_v = [_page_spec(layer, D_SB, (lambda p, kk=kk: npg - 1 - (p * npp + kk)), pg) for kk in range(npp)]
    row = pl.BlockSpec((None, 1, D_SB), lambda bi, p, pt: (bi, 0, 0))
    return pl.pallas_call(
        functools.partial(_sb_dec_kernel, npp=npp, last=steps - 1),
        out_shape=jax.ShapeDtypeStruct((nb, 1, D_SB), F32),
        grid_spec=pltpu.PrefetchScalarGridSpec(
            num_scalar_prefetch=1, grid=(nb, steps),
            in_specs=[row] + specs_k + specs_v, out_specs=row,
            scratch_shapes=[pltpu.VMEM((DEC_ROWS, 1), F32), pltpu.VMEM((DEC_ROWS, D_SB), F32),
                            pltpu.SMEM((1,), I32)]),
        compiler_params=_cparams(("parallel", "arbitrary")),
        name="sb_decode",
    )(page_table, q, *([cache_k] * npp), *([cache_v] * npp))


def _cmp_dec_kernel(pt_ref, kn_ref, vn_ref, pk_ref, pv_ref, pkt_ref, pvt_ref, wk_ref, wv_ref, *refs,
                    npp, last, nblk):
    k_refs, v_refs = refs[:npp], refs[npp:2 * npp]
    ko_ref, vo_ref = refs[2 * npp:]
    p = pl.program_id(1)
    wk_hi, wk_lo = _split2(wk_ref[...])
    wv_hi, wv_lo = _split2(wv_ref[...])
    pg = k_refs[0].shape[1]
    per = npp * pg // BLK_CMP
    r0 = pl.multiple_of(p * per, per)
    ko_ref[pl.ds(r0, per), :] = _compress_t(
        jnp.concatenate([r[...] for r in k_refs], axis=1), pkt_ref[...], wk_hi, wk_lo)
    vo_ref[pl.ds(r0, per), :] = _compress_t(
        jnp.concatenate([r[...] for r in v_refs], axis=1), pvt_ref[...], wv_hi, wv_lo)

    @pl.when(p == last)
    def _():
        first = lax.broadcasted_iota(I32, (DEC_ROWS, LANES), 0) == 0
        xk = jnp.where(first, kn_ref[...] * pk_ref[0:1, :], 0.0)
        xv = jnp.where(first, vn_ref[...] * pv_ref[0:1, :], 0.0)
        a, b = _split2(xk)
        ko_ref[nblk:nblk + DEC_ROWS, :] = _dot(a, wk_hi) + _dot(b, wk_hi) + _dot(a, wk_lo)
        a, b = _split2(xv)
        vo_ref[nblk:nblk + DEC_ROWS, :] = _dot(a, wv_hi) + _dot(b, wv_hi) + _dot(a, wv_lo)


def _cmp_decode(layer, page_table, ck_new, cv_new, cache_k, cache_v, pk, pv, pkt, pvt, wk, wv):
    nb, npg = page_table.shape
    pg = cache_k.shape[3]
    npp = CMP_SPAN // pg
    assert npg % npp == 0
    steps = npg // npp
    nblk = npg * pg // BLK_CMP
    specs = [_page_spec(layer, D_KV, (lambda p, kk=kk: p * npp + kk), pg) for kk in range(npp)]
    row = pl.BlockSpec((None, 1, D_KV), lambda bi, p, pt: (bi, 0, 0))
    cst = lambda a: pl.BlockSpec(a.shape, lambda bi, p, pt: (0,) * a.ndim)
    out = pl.BlockSpec((None, nblk + DEC_ROWS, LANES), lambda bi, p, pt: (bi, 0, 0))
    return pl.pallas_call(
        functools.partial(_cmp_dec_kernel, npp=npp, last=steps - 1, nblk=nblk),
        out_shape=[jax.ShapeDtypeStruct((nb, nblk + DEC_ROWS, LANES), F32)] * 2,
        grid_spec=pltpu.PrefetchScalarGridSpec(
            num_scalar_prefetch=1, grid=(nb, steps),
            in_specs=[row, row, cst(pk), cst(pv), cst(pkt), cst(pvt), cst(wk), cst(wv)] + specs + specs,
            out_specs=[out, out]),
        compiler_params=_cparams(("parallel", "arbitrary")),
        name="cmp_decode",
    )(page_table, ck_new, cv_new, pk, pv, pkt, pvt, wk, wv, *([cache_k] * npp), *([cache_v] * npp))


def _stack_heads_q_row(q):
    qf = q.astype(F32)
    row = lax.broadcasted_iota(I32, (DEC_ROWS, LANES), 0)
    lane_g = lax.broadcasted_iota(I32, (DEC_ROWS, LANES), 1) // HEAD_DIM
    out = jnp.zeros((DEC_ROWS, LANES), F32)
    for h in range(NSA_HEADS):
        g = h // NSA_REP
        blk = jnp.broadcast_to(qf[:, (h // 2) * LANES:(h // 2 + 1) * LANES], (DEC_ROWS, LANES))
        if h % 2 != g:
            blk = pltpu.roll(blk, HEAD_DIM, 1)
        out = jnp.where((row == h) & (lane_g == g), blk, out)
    return out.astype(BF)


def _unstack_heads_row(acc):
    lane = lax.broadcasted_iota(I32, (1, LANES), 1)
    accr = pltpu.roll(acc, HEAD_DIM, 1)
    cols = []
    for c in range(NSA_HEADS // 2):
        g = (2 * c) // NSA_REP
        if g == 0:
            cols.append(jnp.where(lane < HEAD_DIM, acc[2 * c:2 * c + 1], accr[2 * c + 1:2 * c + 2]))
        else:
            cols.append(jnp.where(lane < HEAD_DIM, accr[2 * c:2 * c + 1], acc[2 * c + 1:2 * c + 2]))
    return jnp.concatenate(cols, axis=1)


def _sel_dec_kernel(q_ref, kc_ref, vc_ref, ocmp_ref, idx_ref, imp_scr, *, q_pos, nblk, ns_pad):
    nrow = kc_ref.shape[0]
    qs = _stack_heads_q_row(q_ref[...])
    qs = jnp.concatenate([qs, jnp.zeros((LANES - DEC_ROWS, LANES), BF)], axis=0)
    kc = kc_ref[...].astype(BF)
    st = _dot_nt(kc, qs)
    n_idx = lax.broadcasted_iota(I32, (nrow, LANES), 0)
    valid = (n_idx + 1) * BLK_CMP - 1 <= q_pos
    st = jnp.where(valid, st, NEG)
    mx = jnp.max(st, axis=0, keepdims=True)
    e = jnp.where(valid, jnp.exp(st - mx), 0.0)
    den = jnp.sum(e, axis=0, keepdims=True)
    p = e / jnp.where(den > 0.0, den, 1.0)
    ocmp_ref[...] = _dot_tn(p.astype(BF), vc_ref[...].astype(BF))[0:NSA_HEADS]

    grp = (lax.broadcasted_iota(I32, (LANES, LANES), 0) // NSA_REP
           == lax.broadcasted_iota(I32, (LANES, LANES), 1)).astype(BF)
    a, b, c = _split3(p)
    impc = _dot(a, grp) + _dot(b, grp) + _dot(c, grp)
    pair = (lax.broadcasted_iota(I32, (ns_pad, nrow), 1) // (BLK_SEL // BLK_CMP)
            == lax.broadcasted_iota(I32, (ns_pad, nrow), 0)).astype(BF)
    a, b, c = _split3(impc)
    v = _dot(pair, a) + _dot(pair, b) + _dot(pair, c)
    ns = (nblk * BLK_CMP + 1 + BLK_SEL - 1) // BLK_SEL
    blk = lax.broadcasted_iota(I32, (ns_pad, LANES), 0)
    forced = (blk == 0) | (blk == q_pos // BLK_SEL)
    future = blk * BLK_SEL > q_pos
    v = jnp.where(forced, FORCE_SCORE, jnp.where(future, -1.0, v))
    v = jnp.where(blk < ns, v, -1.0)
    key = _order_key(v)
    imp_scr[...] = key

    def body(j, cnt):
        return cnt + _beats(imp_scr[pl.ds(j, 1), :], key, blk > j)

    cnt = lax.fori_loop(0, ns_pad, body, jnp.zeros((ns_pad, LANES), I32), unroll=8)
    blk_f = blk.astype(F32)
    rows = [jnp.sum(jnp.where(cnt == r, blk_f, 0.0), axis=0, keepdims=True) for r in range(TOP_N)]
    idx_ref[...] = jnp.concatenate(rows, axis=0).astype(I32)


def _sel_decode(nq, kc, vc, *, q_pos, nblk):
    nb = nq.shape[0]
    nrow = kc.shape[1]
    ns_pad = ((nrow // 2 + 7) // 8) * 8
    return pl.pallas_call(
        functools.partial(_sel_dec_kernel, q_pos=q_pos, nblk=nblk, ns_pad=ns_pad),
        out_shape=[jax.ShapeDtypeStruct((nb, NSA_HEADS, LANES), F32), jax.ShapeDtypeStruct((nb, TOP_N, LANES), I32)],
        grid=(nb,),
        in_specs=[pl.BlockSpec((None, 1, D_NSA), lambda bi: (bi, 0, 0)),
                  pl.BlockSpec((None, nrow, LANES), lambda bi: (bi, 0, 0)),
                  pl.BlockSpec((None, nrow, LANES), lambda bi: (bi, 0, 0))],
        out_specs=[pl.BlockSpec((None, NSA_HEADS, LANES), lambda bi: (bi, 0, 0)),
                   pl.BlockSpec((None, TOP_N, LANES), lambda bi: (bi, 0, 0))],
        scratch_shapes=[pltpu.VMEM((ns_pad, LANES), F32)],
        compiler_params=_cparams(("parallel",)),
        name="sel_decode",
    )(nq, kc, vc)


def _nsa_dec_kernel(pt_ref, idx_ref, q_ref, gate_ref, ocmp_ref, skn_ref, svn_ref, wkn_ref, wvn_ref,
                    wkc_ref, wvc_ref, wk_ref, wv_ref, ck_hbm, cv_hbm, o_ref, wko_ref, wvo_ref, kbuf, vbuf, sem,
                    *, layer, ncache, per):
    nsel = NSA_KV * TOP_N
    bi = pl.program_id(0)
    pg = kbuf.shape[2]

    def page_copies(j):
        blk = jnp.minimum(idx_ref[bi, j], ncache - 1)
        page = pt_ref[bi, blk // per]
        return (pltpu.make_async_copy(ck_hbm.at[layer, page], kbuf.at[j], sem.at[0, j]),
                pltpu.make_async_copy(cv_hbm.at[layer, page], vbuf.at[j], sem.at[1, j]))

    for j in range(nsel):
        for cp in page_copies(j):
            cp.start()

    qs = _stack_heads_q_row(q_ref[...])
    qf = qs.astype(F32)
    row_g = lax.broadcasted_iota(I32, (DEC_ROWS, LANES), 0) // NSA_REP

    def attend(s, vt, s_new, v_new):
        mx = jnp.maximum(s_new, jnp.max(s, axis=1, keepdims=True))
        p_new = jnp.exp(s_new - mx)
        p = jnp.exp(s - mx)
        den = p_new + jnp.sum(p, axis=1, keepdims=True)
        return (p_new * v_new + _dot_nt(p.astype(BF), vt)) / den

    bf_row = lambda r: r[...].astype(BF).astype(F32)
    wk = wk_ref[...]
    wv = wv_ref[...]
    s_wn = jnp.sum(qf * bf_row(wkn_ref), axis=1, keepdims=True)
    o_win = attend(_dot(qs, wk.astype(BF)), wv.astype(BF), s_wn, bf_row(wvn_ref))

    wb = wk.shape[1]
    last = lax.broadcasted_iota(I32, (D_KV, wb), 1) == wb - 1
    wko_ref[...] = jnp.where(last, wkc_ref[...], pltpu.roll(wk, wb - 1, 1))
    wvo_ref[...] = jnp.where(last, wvc_ref[...], pltpu.roll(wv, wb - 1, 1))

    for j in range(nsel):
        for cp in page_copies(j):
            cp.wait()

    s_new = jnp.sum(qf * bf_row(skn_ref), axis=1, keepdims=True)
    nkeys = TOP_N * pg
    col = lax.broadcasted_iota(I32, (DEC_ROWS, nkeys), 1)
    col_slot = col // pg
    col_blk = (col % pg) // BLK_SEL
    o_slc = jnp.zeros((DEC_ROWS, LANES), F32)
    for g in range(NSA_KV):
        pen = jnp.full((DEC_ROWS, nkeys), NEG, F32)
        for r in range(TOP_N):
            blk = idx_ref[bi, g * TOP_N + r]
            want = jnp.where(blk < ncache, blk % per, -1)
            pen = jnp.where((col_slot == r) & (col_blk == want), 0.0, pen)
        kcat = jnp.concatenate([kbuf[g * TOP_N + r] for r in range(TOP_N)], axis=1).astype(BF)
        vcat = jnp.concatenate([vbuf[g * TOP_N + r] for r in range(TOP_N)], axis=1).astype(BF)
        o_g = attend(_dot(qs, kcat) + pen, vcat, s_new, bf_row(svn_ref))
        o_slc = jnp.where(row_g == g, o_g, o_slc)

    gates = jnp.broadcast_to(gate_ref[...], (DEC_ROWS, LANES))
    out = (_gate_expand(gates, 0)[0:1] * _unstack_heads_row(ocmp_ref[...])
           + _gate_expand(gates, 1)[0:1] * _unstack_heads_row(o_slc)
           + _gate_expand(gates, 2)[0:1] * _unstack_heads_row(o_win))
    o_ref[...] = out


def _nsa_decode(layer, page_table, idx, nq, gates, ocmp, sk_new, sv_new, wk_new, wv_new, win_k, win_v,
                cache_k, cache_v):
    nb, npg = page_table.shape
    pg = cache_k.shape[3]
    per = pg // BLK_SEL
    ncache = npg * per
    wb = win_k.shape[3]
    nsel = NSA_KV * TOP_N

    row = lambda w: pl.BlockSpec((None, 1, w), lambda bi, pt, ix: (bi, 0, 0))
    colspec = pl.BlockSpec((None, D_KV, 1), lambda bi, pt, ix: (bi, 0, 0))
    win = pl.BlockSpec((None, None, D_KV, wb), lambda bi, pt, ix: (layer, bi, 0, 0))
    wout = pl.BlockSpec((None, D_KV, wb), lambda bi, pt, ix: (bi, 0, 0))
    hbm = pl.BlockSpec(memory_space=pl.ANY)
    col = lambda a: a.reshape(nb, D_KV, 1)
    return pl.pallas_call(
        functools.partial(_nsa_dec_kernel, layer=layer, ncache=ncache, per=per),
        out_shape=[jax.ShapeDtypeStruct((nb, 1, D_NSA), F32), jax.ShapeDtypeStruct((nb, D_KV, wb), F32),
                   jax.ShapeDtypeStruct((nb, D_KV, wb), F32)],
        grid_spec=pltpu.PrefetchScalarGridSpec(
            num_scalar_prefetch=2, grid=(nb,),
            in_specs=[row(D_NSA), row(LANES), pl.BlockSpec((None, NSA_HEADS, LANES), lambda bi, pt, ix: (bi, 0, 0)),
                      row(D_KV), row(D_KV), row(D_KV), row(D_KV), colspec, colspec, win, win, hbm, hbm],
            out_specs=[row(D_NSA), wout, wout],
            scratch_shapes=[pltpu.VMEM((nsel, D_KV, pg), F32), pltpu.VMEM((nsel, D_KV, pg), F32),
                            pltpu.SemaphoreType.DMA((2, nsel))]),
        compiler_params=_cparams(("arbitrary",)),
        name="nsa_decode",
    )(page_table, idx, nq, gates, ocmp, sk_new, sv_new, wk_new, wv_new, col(wk_new), col(wv_new),
      win_k, win_v, cache_k, cache_v)


def _ret_dec_kernel(q_ref, k_ref, kcol_ref, v_ref, g_ref, gcol_ref, grow_ref, st_ref, o_ref, sto_ref):
    rows = DEC_ROWS
    st = st_ref[...]
    q = jnp.broadcast_to(q_ref[...], (rows, D_RET))
    v = jnp.broadcast_to(v_ref[...], (rows, D_RET))
    lane_h = lax.broadcasted_iota(I32, (rows, D_RET), 1) // HEAD_DIM
    row = lax.broadcasted_iota(I32, (rows, D_RET), 0)
    own = lane_h == row
    qs = jnp.where(own, q, 0.0).astype(BF)
    spread = (lax.broadcasted_iota(I32, (HEAD_DIM, D_RET), 1) % HEAD_DIM
              == lax.broadcasted_iota(I32, (HEAD_DIM, D_RET), 0)).astype(BF)
    shi, slo = _split2(st)
    cross = _dot(qs, shi) + _dot(qs, slo)
    o_cross = jnp.sum(jnp.where(own, _dot_x2(cross, spread), 0.0), axis=0, keepdims=True) * grow_ref[...]
    seg = _seg_mean_mat(D_RET)
    qk = q * jnp.broadcast_to(k_ref[...], (rows, D_RET))
    o = o_cross + (_dot_x2(qk, seg) * float(HEAD_DIM)) * v

    v4 = _dot_nt(jnp.where(own, v, 0.0).astype(BF), spread)
    pick = (lax.broadcasted_iota(I32, (D_RET, rows), 0) // HEAD_DIM
            == lax.broadcasted_iota(I32, (D_RET, rows), 1)).astype(BF)
    vexp = _dot(pick, v4.astype(BF))
    sto_ref[...] = st * gcol_ref[...] + kcol_ref[...] * vexp

    mu = _dot_x2(o, seg)
    d = o - mu
    var = _dot_x2(d * d, seg)
    y = d * lax.rsqrt(var + EPS)
    o_ref[...] = y[0:1] * jax.nn.silu(g_ref[...])


def _ret_decode(layer, q, k, v, g, state):
    nb = q.shape[0]
    log_gamma = np.log1p(-np.exp2(-5.0 - np.arange(RET_HEADS, dtype=np.float64)))
    gam = np.repeat(np.exp(log_gamma), HEAD_DIM)
    gcol = jnp.asarray(gam[:, None], F32)
    grow = jnp.asarray(gam[None, :], F32)
    kcol = k.reshape(nb, D_RET, 1)
    row = pl.BlockSpec((None, 1, D_RET), lambda bi: (bi, 0, 0))
    return pl.pallas_call(
        _ret_dec_kernel,
        out_shape=[jax.ShapeDtypeStruct((nb, 1, D_RET), F32), jax.ShapeDtypeStruct((nb, D_RET, HEAD_DIM), F32)],
        grid=(nb,),
        in_specs=[row, row, pl.BlockSpec((None, D_RET, 1), lambda bi: (bi, 0, 0)), row, row,
                  _const_spec(gcol.shape), _const_spec(grow.shape),
                  pl.BlockSpec((None, None, D_RET, HEAD_DIM), lambda bi: (layer, bi, 0, 0))],
        out_specs=[row, pl.BlockSpec((None, D_RET, HEAD_DIM), lambda bi: (bi, 0, 0))],
        compiler_params=_cparams(("parallel",)),
        name="ret_decode",
    )(q, k, kcol, v, g, gcol, grow, state)


def _swap_perm(rot_dim):
    half = rot_dim // 2
    p = np.arange(HEAD_DIM)
    p[:half] = np.arange(half, rot_dim)
    p[half:rot_dim] = np.arange(half)
    return p


def _swap_cols(start, width, rot_dim):
    p = _swap_perm(rot_dim)
    return np.concatenate([start + h * HEAD_DIM + p for h in range(width // HEAD_DIM)])


def _rope_tables(pos, rot_dim, theta, width):
    half = rot_dim // 2
    inv = jnp.exp(-math.log(theta) * jnp.arange(half, dtype=F32) / half)
    ang = pos.astype(F32)[:, None] * inv[None, :]
    cos, sin = jnp.cos(ang), jnp.sin(ang)
    ones = jnp.ones((pos.shape[0], HEAD_DIM - rot_dim), F32)
    c = jnp.concatenate([cos, cos, ones], axis=1)
    s = jnp.concatenate([-sin, sin, 0.0 * ones], axis=1)
    reps = width // HEAD_DIM
    return jnp.tile(c, (1, reps)), jnp.tile(s, (1, reps))


def _layer_params(l, norm1_g, w_in, nsa_q_norm, nsa_k_norm, cmp_pos_k, cmp_pos_v, cmp_w_k, cmp_w_v, w_out,
                  norm2_g, w_up, w_down):
    w = w_in[l]
    d = w.shape[0]
    wm = jnp.pad(w, ((0, 0), (0, _N_IN_PAD - _N_IN))).astype(BF)
    cols = np.concatenate([
        _swap_cols(_C_RQ, D_RET, HEAD_DIM), _swap_cols(_C_RK, D_RET, HEAD_DIM),
        _swap_cols(_C_NQ, D_NSA, ROT_DIM), _swap_cols(_C_CK, D_KV, ROT_DIM),
        _swap_cols(_C_SK, D_KV, ROT_DIM), _swap_cols(_C_WK, D_KV, ROT_DIM)])
    ws = w[:, cols].astype(BF)
    gains = jnp.concatenate([nsa_q_norm[l][None], nsa_k_norm[l]], axis=0)
    gn = jnp.tile(gains, (1, LANES // HEAD_DIM))
    gs = jnp.tile(gains[:, _swap_perm(ROT_DIM)], (1, LANES // HEAD_DIM))
    eye2 = jnp.eye(NSA_KV, dtype=F32)
    return dict(
        g1=norm1_g[l][None], wm=wm, ws=ws, gn=gn, gs=gs,
        pk=jnp.tile(cmp_pos_k[l], (1, NSA_KV)), pv=jnp.tile(cmp_pos_v[l], (1, NSA_KV)),
        pkt=jnp.tile(cmp_pos_k[l].T, (NSA_KV, CMP_SPAN // BLK_CMP)),
        pvt=jnp.tile(cmp_pos_v[l].T, (NSA_KV, CMP_SPAN // BLK_CMP)),
        phik=jnp.kron(eye2, cmp_w_k[l]), phiv=jnp.kron(eye2, cmp_w_v[l]),
        wo=w_out[l].astype(BF), g2=norm2_g[l][None], wu=w_up[l].astype(BF), wd=w_down[l].astype(BF))


def _prompt_layer(xp2d, prm, tabs, *, b, t, tm):
    (sbk_f, sbv_f, ck_f, cv_f, sk_f, sv_f, wk_f, wv_f,
     sbq_b, sbk_b, sbv_b, rq_b, rk_b, rv_b, rg_f, nq_b, sk_b, sv_b, wk_b, wv_b, ng_f) = _project(
        xp2d, prm['g1'], prm['wm'], prm['ws'], tabs, prm['gn'], prm['gs'], tm=tm, tab_blocks=t // tm, seqs=b)
    o_sb = _sb_prompt(sbq_b, sbk_b, sbv_b, b=b, t=t)
    o_ret, ret_st = _ret_prompt(rq_b, rk_b, rv_b, rg_f, b=b, t=t)
    kc, vc = _compress_prompt(ck_f, cv_f, prm['pkt'], prm['pvt'], prm['phik'], prm['phiv'])
    o_nsa = _nsa_prompt(nq_b, ng_f, kc, vc, sk_b, sv_b, wk_b, wv_b, b=b, t=t)
    y = _out_mlp(xp2d, o_sb, o_ret, o_nsa, prm['wo'], prm['g2'], prm['wu'], prm['wd'], tm=tm)
    keep = min(WINDOW, t)
    r4 = lambda a, h: jnp.transpose(a.reshape(b, h, HEAD_DIM, a.shape[-1]), (0, 3, 1, 2))
    st = ret_st.reshape(b, RET_HEADS, HEAD_DIM, RET_HEADS, HEAD_DIM)
    st = jnp.stack([st[:, h, :, h, :] for h in range(RET_HEADS)], axis=1)
    caches = dict(
        p_sb_k=r4(sbk_f, SB_HEADS), p_sb_v=r4(sbv_f, SB_HEADS),
        p_cmp_k=r4(ck_f, NSA_KV), p_cmp_v=r4(cv_f, NSA_KV),
        p_slc_k=r4(sk_f, NSA_KV), p_slc_v=r4(sv_f, NSA_KV),
        p_win_k=r4(wk_f[:, :, t - keep:], NSA_KV), p_win_v=r4(wv_f[:, :, t - keep:], NSA_KV),
        p_ret=st)
    return y, caches


def _sample_layer(l, xs2d, prm, tabs, caches, states, page_table, *, past_len):
    nb = xs2d.shape[0]
    (sbk_f, sbv_f, ck_f, cv_f, sk_f, sv_f, wk_f, wv_f,
     sbq_b, sbk_b, sbv_b, rq_b, rk_b, rv_b, rg_f, nq_b, sk_b, sv_b, wk_b, wv_b, ng_f) = _project(
        xs2d, prm['g1'], prm['wm'], prm['ws'], tabs, prm['gn'], prm['gs'], tm=nb, tab_blocks=1)
    row = lambda a: a.astype(F32).reshape(nb, 1, a.shape[-1])
    c_sb_k, c_sb_v, c_cmp_k, c_cmp_v, c_slc_k, c_slc_v = caches
    win_k, win_v, st_ret = states
    pg = c_cmp_k.shape[3]
    nblk = page_table.shape[1] * pg // BLK_CMP
    o_sb = _sb_decode(l, page_table, row(sbq_b), c_sb_k, c_sb_v)
    o_ret, ret_new = _ret_decode(l, row(rq_b), row(rk_b), row(rv_b), row(rg_f), st_ret)
    kc, vc = _cmp_decode(l, page_table, row(ck_f), row(cv_f), c_cmp_k, c_cmp_v,
                         prm['pk'], prm['pv'], prm['pkt'], prm['pvt'], prm['phik'], prm['phiv'])
    ocmp, idx = _sel_decode(row(nq_b), kc, vc, q_pos=past_len, nblk=nblk)
    idx = jnp.transpose(idx[:, :, :NSA_KV], (0, 2, 1)).reshape(nb, NSA_KV * TOP_N)
    o_nsa, win_k_new, win_v_new = _nsa_decode(
        l, page_table, idx, row(nq_b), row(ng_f), ocmp, row(sk_f), row(sv_f), row(wk_f), row(wv_f),
        win_k, win_v, c_slc_k, c_slc_v)
    y = _out_mlp(xs2d, o_sb.reshape(nb, D_SB), o_ret.reshape(nb, D_RET), o_nsa.reshape(nb, D_NSA),
                 prm['wo'], prm['g2'], prm['wu'], prm['wd'], tm=nb)
    r4 = lambda a, h: a.reshape(nb, 1, h, HEAD_DIM)
    wb = win_k.shape[3]
    win4 = lambda a: jnp.transpose(a.reshape(nb, NSA_KV, HEAD_DIM, wb), (0, 3, 1, 2))
    out = dict(
        s_sb_k=r4(sbk_f, SB_HEADS), s_sb_v=r4(sbv_f, SB_HEADS),
        s_cmp_k=r4(ck_f, NSA_KV), s_cmp_v=r4(cv_f, NSA_KV),
        s_slc_k=r4(sk_f, NSA_KV), s_slc_v=r4(sv_f, NSA_KV),
        s_win_k=win4(win_k_new), s_win_v=win4(win_v_new),
        s_ret=ret_new.reshape(nb, RET_HEADS, HEAD_DIM, HEAD_DIM))
    return y, out


def kernel(x_prompt, x_sample, cache_sb_k, cache_sb_v, cache_cmp_k, cache_cmp_v, cache_slc_k, cache_slc_v,
           state_win_k, state_win_v, state_ret, page_table, norm1_g, w_in, nsa_q_norm, nsa_k_norm, cmp_pos_k,
           cmp_pos_v, cmp_w_k, cmp_w_v, w_out, norm2_g, w_up, w_down):
    b, t, d = x_prompt.shape
    depth = w_in.shape[0]
    tm = min(512, t)
    pos_p = jnp.arange(t, dtype=I32)
    tabs_p = (*_rope_tables(pos_p, HEAD_DIM, RET_THETA, D_RET), *_rope_tables(pos_p, ROT_DIM, ROPE_THETA, LANES))
    xp = x_prompt.reshape(b * t, d)

    nb, n_new, _ = x_sample.shape
    assert n_new == 1, "sample group kernels handle one new token per sample"
    pg = cache_sb_k.shape[2]
    past_len = page_table.shape[1] * pg
    wb = state_win_k.shape[2]
    assert wb <= WINDOW and wb <= past_len
    pos_s = jnp.full((nb,), past_len, dtype=I32)
    tabs_s = (*_rope_tables(pos_s, HEAD_DIM, RET_THETA, D_RET), *_rope_tables(pos_s, ROT_DIM, ROPE_THETA, LANES))
    xs = x_sample.reshape(nb, d)
    caches = tuple(_pages_t(c) for c in (cache_sb_k, cache_sb_v, cache_cmp_k, cache_cmp_v, cache_slc_k, cache_slc_v))
    states = (_pages_t(state_win_k), _pages_t(state_win_v),
              state_ret.reshape(depth, nb, RET_HEADS * HEAD_DIM, HEAD_DIM))

    new = {}
    for l in range(depth):
        prm = _layer_params(l, norm1_g, w_in, nsa_q_norm, nsa_k_norm, cmp_pos_k, cmp_pos_v, cmp_w_k, cmp_w_v,
                            w_out, norm2_g, w_up, w_down)
        xp, p_new = _prompt_layer(xp, prm, tabs_p, b=b, t=t, tm=tm)
        xs, s_new = _sample_layer(l, xs, prm, tabs_s, caches, states, page_table, past_len=past_len)
        for name, val in {**p_new, **s_new}.items():
            new.setdefault(name, []).append(val)
    st = lambda name: jnp.stack(new[name])
    return (xp.reshape(b, t, d), xs.reshape(nb, 1, d)) + tuple(st(nm) for nm in (
        'p_sb_k', 'p_sb_v', 'p_cmp_k', 'p_cmp_v', 'p_slc_k', 'p_slc_v', 'p_win_k', 'p_win_v', 'p_ret',
        's_sb_k', 's_sb_v', 's_cmp_k', 's_cmp_v', 's_slc_k', 's_slc_v', 's_win_k', 's_win_v', 's_ret'))
```

```python
import functools
import math

import numpy as np
import jax
import jax.numpy as jnp
from jax import lax
from jax.experimental import pallas as pl
from jax.experimental.pallas import tpu as pltpu

HEAD_DIM = 64
SB_HEADS = 4
RET_HEADS = 4
NSA_HEADS = 8
NSA_KV = 2
NSA_REP = NSA_HEADS // NSA_KV
D_SB = SB_HEADS * HEAD_DIM
D_RET = RET_HEADS * HEAD_DIM
D_NSA = NSA_HEADS * HEAD_DIM
D_KV = NSA_KV * HEAD_DIM
ROPE_THETA = 500000.0
ROT_DIM = HEAD_DIM // 4
RET_THETA = 10000.0
BLK_CMP = 32
BLK_SEL = 64
TOP_N = 16
WINDOW = 512
FORCE_SCORE = 1.0e4
NEG = -1.0e30
EPS = 1e-6
QK_SCALE = HEAD_DIM ** -0.5

LANES = 128
VMEM_LIMIT = 56 * 1024 * 1024

BF = jnp.bfloat16
F32 = jnp.float32
I32 = jnp.int32

_C_SBQ, _C_SBK, _C_SBV = 0, 256, 512
_C_RQ, _C_RK, _C_RV, _C_RG = 768, 1024, 1280, 1536
_C_NQ = 1792
_C_CK, _C_CV, _C_SK, _C_SV, _C_WK, _C_WV = 2304, 2432, 2560, 2688, 2816, 2944
_C_NG = 3072
_N_IN = 3096
_N_IN_PAD = 3200


def _dot(a, b):
    return jnp.dot(a, b, preferred_element_type=F32)


def _dot_nt(a, b):
    return lax.dot_general(a, b, (((1,), (1,)), ((), ())), preferred_element_type=F32)


def _dot_tn(a, b):
    return lax.dot_general(a, b, (((0,), (0,)), ((), ())), preferred_element_type=F32)


def _split2(x):
    hi = x.astype(BF)
    lo = (x - hi.astype(F32)).astype(BF)
    return hi, lo


def _split3(x):
    hi = x.astype(BF)
    r = x - hi.astype(F32)
    mid = r.astype(BF)
    lo = (r - mid.astype(F32)).astype(BF)
    return hi, mid, lo


def _dot_x2(x, w):
    hi, lo = _split2(x)
    return _dot(hi, w) + _dot(lo, w)


def _seg_mean_mat(n):
    r = lax.broadcasted_iota(I32, (n, n), 0) // HEAD_DIM
    c = lax.broadcasted_iota(I32, (n, n), 1) // HEAD_DIM
    return jnp.where(r == c, 1.0 / HEAD_DIM, 0.0).astype(BF)


def _cparams(sem, vmem=VMEM_LIMIT):
    return pltpu.CompilerParams(dimension_semantics=sem, vmem_limit_bytes=vmem)


def _const_spec(shape):
    nd = len(shape)
    return pl.BlockSpec(shape, lambda *a: (0,) * nd)


def _proj_kernel(x_ref, g1_ref, wm_ref, cr_ref, sr_ref, cn_ref, sn_ref, gn_ref, gs_ref,
                 sbk_f, sbv_f, ck_f, cv_f, sk_f, sv_f, wk_f, wv_f,
                 sbq_b, sbk_b, sbv_b, rq_b, rk_b, rv_b, rg_f, nq_b, sk_b, sv_b, wk_b, wv_b, ng_f, *, t_out):
    x = x_ref[...]
    ms = jnp.mean(x * x, axis=-1, keepdims=True)
    xn = (x * lax.rsqrt(ms + EPS) * g1_ref[...]).astype(BF)

    def put(ref, val):
        ref[...] = jnp.transpose(val) if t_out else val

    def mm(w_ref, lo, n):
        return _dot(xn, w_ref[:, lo:lo + n])

    sbq_b[...] = (mm(wm_ref, _C_SBQ, D_SB) * QK_SCALE).astype(BF)
    k = mm(wm_ref, _C_SBK, D_SB)
    put(sbk_f, k)
    sbk_b[...] = k.astype(BF)
    v = mm(wm_ref, _C_SBV, D_SB)
    put(sbv_f, v)
    sbv_b[...] = v.astype(BF)

    cr = cr_ref[...]
    sr = sr_ref[...]
    def swap_halves(y, half):
        d = lax.broadcasted_iota(I32, (y.shape[0], LANES), 1) % HEAD_DIM
        cols = [y[:, c:c + LANES] for c in range(0, y.shape[1], LANES)]
        cols = [jnp.where(d < half, pltpu.roll(c, LANES - half, 1), pltpu.roll(c, half, 1)) for c in cols]
        return cols[0] if len(cols) == 1 else jnp.concatenate(cols, axis=1)

    y = mm(wm_ref, _C_RQ, D_RET)
    rq_b[...] = (y * cr + swap_halves(y, HEAD_DIM // 2) * sr).astype(BF)
    y = mm(wm_ref, _C_RK, D_RET)
    rk_b[...] = ((y * cr + swap_halves(y, HEAD_DIM // 2) * sr) * QK_SCALE).astype(BF)
    rv_b[...] = mm(wm_ref, _C_RV, D_RET).astype(BF)
    rg_f[...] = mm(wm_ref, _C_RG, D_RET)

    seg = _seg_mean_mat(LANES)
    cn = cn_ref[...]
    sn = sn_ref[...]

    def normrope(cm, gi):
        y = mm(wm_ref, cm, LANES)
        ysw = swap_halves(y, ROT_DIM // 2)
        r = lax.rsqrt(_dot_x2(y * y, seg) + EPS)
        g = gn_ref[gi:gi + 1, :]
        gsw = gs_ref[gi:gi + 1, :]
        return r * (y * (g * cn) + ysw * (gsw * sn))

    for c in range(D_NSA // LANES):
        nq_b[:, c * LANES:(c + 1) * LANES] = (
            normrope(_C_NQ + c * LANES, 0) * QK_SCALE).astype(BF)
    put(ck_f, normrope(_C_CK, 1))
    put(cv_f, mm(wm_ref, _C_CV, D_KV))
    k = normrope(_C_SK, 2)
    put(sk_f, k)
    sk_b[...] = k.astype(BF)
    v = mm(wm_ref, _C_SV, D_KV)
    put(sv_f, v)
    sv_b[...] = v.astype(BF)
    k = normrope(_C_WK, 3)
    put(wk_f, k)
    wk_b[...] = k.astype(BF)
    v = mm(wm_ref, _C_WV, D_KV)
    put(wv_f, v)
    wv_b[...] = v.astype(BF)
    ng_f[...] = jax.nn.sigmoid(mm(wm_ref, _C_NG, LANES))


def _project(x2d, g1, wm, tabs, gn, gs, *, tm, tab_blocks, seqs=None):
    n, d = x2d.shape
    cr, sr, cn, sn = tabs
    grid = (n // tm,)
    tok = lambda w: pl.BlockSpec((tm, w), lambda i: (i, 0))
    tab = lambda w: pl.BlockSpec((tm, w), lambda i: (i % tab_blocks, 0))
    f32 = lambda w: jax.ShapeDtypeStruct((n, w), F32)
    bf = lambda w: jax.ShapeDtypeStruct((n, w), BF)
    out_w_f = [D_SB, D_SB] + [D_KV] * 6
    if seqs is None:
        cache_shapes = [f32(w) for w in out_w_f]
        cache_specs = [tok(w) for w in out_w_f]
    else:
        t = n // seqs
        per = t // tm
        cache_shapes = [jax.ShapeDtypeStruct((seqs, w, t), F32) for w in out_w_f]
        cache_specs = [pl.BlockSpec((None, w, tm), lambda i: (i // per, 0, i % per)) for w in out_w_f]
    out_shape = (cache_shapes
                 + [bf(D_SB)] * 3 + [bf(D_RET)] * 3 + [f32(D_RET), bf(D_NSA)] + [bf(D_KV)] * 4 + [f32(LANES)])
    out_w = [D_SB] * 3 + [D_RET] * 3 + [D_RET, D_NSA] + [D_KV] * 4 + [LANES]
    return pl.pallas_call(
        functools.partial(_proj_kernel, t_out=seqs is not None),
        out_shape=out_shape,
        grid=grid,
        in_specs=[tok(d), _const_spec((1, d)), _const_spec(wm.shape),
                  tab(D_RET), tab(D_RET), tab(LANES), tab(LANES),
                  _const_spec(gn.shape), _const_spec(gs.shape)],
        out_specs=cache_specs + [tok(w) for w in out_w],
        compiler_params=_cparams(("parallel",)),
        name="proj",
    )(x2d, g1, wm, cr, sr, cn, sn, gn, gs)


SB_DEAD = -120.0


def _softplus(z):
    return jnp.maximum(z, 0.0) + jnp.log(1.0 + jnp.exp(-jnp.abs(z)))


def _sb_kernel(q_ref, k_ref, v_ref, o_ref, c_ref, acc_ref, *, tq, tk):
    i = pl.program_id(1)
    m = SB_HEADS * tq
    q = q_ref[...]
    lane_h = lax.broadcasted_iota(I32, (tq, D_SB), 1) // HEAD_DIM
    qs = jnp.concatenate([jnp.where(lane_h == h, q, jnp.zeros_like(q)) for h in range(SB_HEADS)], axis=0)
    q_pos = i * tq + lax.broadcasted_iota(I32, (m, tk), 0) % tq
    col = lax.broadcasted_iota(I32, (m, tk), 1)
    tri = (lax.broadcasted_iota(I32, (tk, tk), 0) > lax.broadcasted_iota(I32, (tk, tk), 1)).astype(BF)
    nt = (i * tq) // tk + 1

    c_ref[...] = jnp.zeros_like(c_ref)
    acc_ref[...] = jnp.zeros_like(acc_ref)

    def tile(j, masked):
        off = pl.multiple_of(j * tk, tk)
        z = _dot_nt(qs, k_ref[pl.ds(off, tk), :])
        l1m = -_softplus(z)
        if masked:
            mask = (off + col) < q_pos
            l1m = jnp.where(mask, l1m, 0.0)
        c = c_ref[...]
        after = c + _dot_x2(l1m, tri)
        lw = z + l1m + after
        if masked:
            lw = jnp.where(mask, lw, NEG)
        w = jnp.exp(lw)
        acc_ref[...] += _dot(w.astype(BF), v_ref[pl.ds(off, tk), :])
        c_ref[...] = c + jnp.sum(l1m, axis=1, keepdims=True)

    tile(nt - 1, True)

    def alive():
        return jnp.max(c_ref[...]) > SB_DEAD

    def cond(state):
        return (state[0] < nt - 1) & state[1]

    def body(state):
        tile(nt - 2 - state[0], False)
        return state[0] + 1, alive()

    lax.while_loop(cond, body, (jnp.int32(0), alive()))

    acc = acc_ref[...]
    out = jnp.zeros((tq, D_SB), F32)
    for h in range(SB_HEADS):
        out = out + jnp.where(lane_h == h, acc[h * tq:(h + 1) * tq], 0.0)
    o_ref[...] = out.astype(BF)


def _sb_prompt(q, k, v, *, b, t, tq=256, tk=256):
    n = b * t
    nq = t // tq
    return pl.pallas_call(
        functools.partial(_sb_kernel, tq=tq, tk=tk),
        out_shape=jax.ShapeDtypeStruct((n, D_SB), BF),
        grid=(b, nq),
        in_specs=[pl.BlockSpec((tq, D_SB), lambda bi, i: (bi * nq + i, 0)),
                  pl.BlockSpec((t, D_SB), lambda bi, i: (bi, 0)),
                  pl.BlockSpec((t, D_SB), lambda bi, i: (bi, 0))],
        out_specs=pl.BlockSpec((tq, D_SB), lambda bi, i: (bi * nq + i, 0)),
        scratch_shapes=[pltpu.VMEM((SB_HEADS * tq, 1), F32), pltpu.VMEM((SB_HEADS * tq, D_SB), F32)],
        compiler_params=_cparams(("parallel", "parallel")),
        name="sb_prompt",
    )(q, k, v)


def _ret_kernel(q_ref, k_ref, v_ref, g_ref, dec_ref, qd_ref, kd_ref, gc_ref, o_ref, st_ref, s_scr, *, c):
    ci = pl.program_id(1)

    @pl.when(ci == 0)
    def _():
        s_scr[...] = jnp.zeros_like(s_scr)

    q = q_ref[...]
    k = k_ref[...]
    v = v_ref[...]
    lane_h = lax.broadcasted_iota(I32, (c, D_RET), 1) // HEAD_DIM
    o = jnp.zeros((c, D_RET), F32)
    for h in range(RET_HEADS):
        kh = jnp.where(lane_h == h, k, jnp.zeros_like(k))
        vh = jnp.where(lane_h == h, v, jnp.zeros_like(v))
        s = _dot_nt(q, kh) * dec_ref[h]
        o = o + _dot(s.astype(BF), vh)
    st = s_scr[...]
    shi, slo = _split2(st)
    o = o + (_dot(q, shi) + _dot(q, slo)) * qd_ref[...]

    kd = (k.astype(F32) * kd_ref[...]).astype(BF)
    ktv = _dot_tn(kd, v)
    r = lax.broadcasted_iota(I32, (D_RET, D_RET), 0) // HEAD_DIM
    cc = lax.broadcasted_iota(I32, (D_RET, D_RET), 1) // HEAD_DIM
    new_st = st * gc_ref[...] + jnp.where(r == cc, ktv, 0.0)
    s_scr[...] = new_st
    st_ref[...] = new_st

    seg = _seg_mean_mat(D_RET)
    mu = _dot_x2(o, seg)
    d = o - mu
    var = _dot_x2(d * d, seg)
    y = d * lax.rsqrt(var + EPS)
    o_ref[...] = (y * jax.nn.silu(g_ref[...])).astype(BF)


def _ret_tables(c):
    log_gamma = np.log1p(-np.exp2(-5.0 - np.arange(RET_HEADS, dtype=np.float64)))
    idx = np.arange(c, dtype=np.float64)
    diff = idx[:, None] - idx[None, :]
    dec = np.where(diff >= 0, np.exp(np.maximum(diff, 0.0)[None] * log_gamma[:, None, None]), 0.0)
    lane_lg = np.repeat(log_gamma, HEAD_DIM)
    qd = np.exp((idx[:, None] + 1.0) * lane_lg[None, :])
    kd = np.exp((c - 1.0 - idx)[:, None] * lane_lg[None, :])
    gc = np.exp(c * lane_lg)[None, :]
    f = lambda a: jnp.asarray(a, F32)
    return f(dec), f(qd), f(kd), f(gc)


def _ret_prompt(q, k, v, g, *, b, t, c=256):
    n = b * t
    nc = t // c
    dec, qd, kd, gc = _ret_tables(c)
    tok = pl.BlockSpec((c, D_RET), lambda bi, i: (bi * nc + i, 0))
    o, st = pl.pallas_call(
        functools.partial(_ret_kernel, c=c),
        out_shape=[jax.ShapeDtypeStruct((n, D_RET), BF), jax.ShapeDtypeStruct((b, D_RET, D_RET), F32)],
        grid=(b, nc),
        in_specs=[tok, tok, tok, tok, _const_spec(dec.shape), _const_spec(qd.shape), _const_spec(kd.shape),
                  _const_spec(gc.shape)],
        out_specs=[tok, pl.BlockSpec((None, D_RET, D_RET), lambda bi, i: (bi, 0, 0))],
        scratch_shapes=[pltpu.VMEM((D_RET, D_RET), F32)],
        compiler_params=_cparams(("parallel", "arbitrary")),
        name="ret_prompt",
    )(q, k, v, g, dec, qd, kd, gc)
    return o, st


CMP_SPAN = 2048


def _compress_t(xt, post, phi_hi, phi_lo):
    n = xt.shape[1]
    sel = (lax.broadcasted_iota(I32, (n, LANES), 0) // BLK_CMP
           == lax.broadcasted_iota(I32, (n, LANES), 1)).astype(BF)
    xs_t = _dot_x2(xt * post, sel)
    hi, lo = _split2(jnp.transpose(xs_t))
    out = _dot(hi, phi_hi) + _dot(lo, phi_hi) + _dot(hi, phi_lo)
    return out[0:n // BLK_CMP]


def _compress_kernel(k_ref, v_ref, pk_ref, pv_ref, wk_ref, wv_ref, ko_ref, vo_ref):
    wk_hi, wk_lo = _split2(wk_ref[...])
    wv_hi, wv_lo = _split2(wv_ref[...])
    ko_ref[...] = _compress_t(k_ref[...], pk_ref[...], wk_hi, wk_lo).astype(BF)
    vo_ref[...] = _compress_t(v_ref[...], pv_ref[...], wv_hi, wv_lo).astype(BF)


def _compress_prompt(ckt, cvt, pkt, pvt, wk, wv):
    b, _, t = ckt.shape
    steps = t // CMP_SPAN
    tok = pl.BlockSpec((None, LANES, CMP_SPAN), lambda bi, i: (bi, 0, i))
    out = pl.BlockSpec((CMP_SPAN // BLK_CMP, LANES), lambda bi, i: (bi * steps + i, 0))
    return pl.pallas_call(
        _compress_kernel,
        out_shape=[jax.ShapeDtypeStruct((b * t // BLK_CMP, LANES), BF)] * 2,
        grid=(b, steps),
        in_specs=[tok, tok, _const_spec(pkt.shape), _const_spec(pvt.shape), _const_spec(wk.shape),
                  _const_spec(wv.shape)],
        out_specs=[out, out],
        compiler_params=_cparams(("parallel", "parallel")),
        name="compress_prompt",
    )(ckt, cvt, pkt, pvt, wk, wv)


def _stack_heads_q(q, extra):
    tq = q.shape[0]
    lane = lax.broadcasted_iota(I32, (tq, LANES), 1)
    rows = []
    for h in range(NSA_HEADS):
        g = h // NSA_REP
        blk = q[:, (h // 2) * LANES:(h // 2 + 1) * LANES]
        src_half = h % 2
        if src_half != g:
            blk32 = pltpu.roll(blk.astype(F32), HEAD_DIM, 1).astype(BF)
        else:
            blk32 = blk
        keep = (lane // HEAD_DIM) == g
        rows.append(jnp.where(keep, blk32, jnp.zeros_like(blk32)))
    qs = jnp.concatenate(rows, axis=0)
    if extra is not None:
        qs = jnp.concatenate([qs, extra], axis=1)
    return qs


def _unstack_heads(acc, tq):
    lane = lax.broadcasted_iota(I32, (tq, LANES), 1)
    cols = []
    for c in range(NSA_HEADS // 2):
        g = (2 * c) // NSA_REP
        a = acc[(2 * c) * tq:(2 * c + 1) * tq]
        b = acc[(2 * c + 1) * tq:(2 * c + 2) * tq]
        if g == 0:
            cols.append(jnp.where(lane < HEAD_DIM, a, pltpu.roll(b, HEAD_DIM, 1)))
        else:
            cols.append(jnp.where(lane < HEAD_DIM, pltpu.roll(a, HEAD_DIM, 1), b))
    return jnp.concatenate(cols, axis=1)


def _order_key(v):
    return v


def _beats(key_j, key, j_before):
    return jnp.where(j_before, (key_j >= key).astype(I32), (key_j > key).astype(I32))


def _gate_expand(gates, branch):
    r = lax.broadcasted_iota(I32, (LANES, D_NSA), 0)
    c = lax.broadcasted_iota(I32, (LANES, D_NSA), 1)
    e = (r == (c // HEAD_DIM) * 3 + branch).astype(BF)
    return _dot_x2(gates, e)


def _nsa_kernel(q_ref, gate_ref, kc_ref, vc_ref, sk_ref, sv_ref, wk_ref, wv_ref, o_ref,
                k2_scr, svt_scr, wvt_scr, imp_scr, m_scr, l_scr, acc_scr, sa_scr, sb_scr, *, tq, tk, tw, t):
    i = pl.program_id(1)
    nc = t // BLK_CMP
    ns = t // BLK_SEL
    m = NSA_HEADS * tq

    @pl.when(i == 0)
    def _():
        rows = 512
        for r0 in range(0, t, rows):
            s_idx = r0 + lax.broadcasted_iota(I32, (rows, LANES), 0)
            c_idx = lax.broadcasted_iota(I32, (rows, LANES), 1)
            e = (s_idx // BLK_SEL == c_idx).astype(BF)
            k2_scr[r0:r0 + rows, :] = jnp.concatenate([sk_ref[r0:r0 + rows, :], e], axis=1)
        for c in range(t // tk):
            svt_scr[c] = jnp.transpose(sv_ref[c * tk:(c + 1) * tk, :].astype(F32)).astype(BF)
        for c in range(t // tw):
            wvt_scr[c] = jnp.transpose(wv_ref[c * tw:(c + 1) * tw, :].astype(F32)).astype(BF)

    q = q_ref[...]
    qs = _stack_heads_q(q, None)

    kc = kc_ref[...]
    vct = jnp.transpose(vc_ref[...].astype(F32)).astype(BF)
    n_idx = lax.broadcasted_iota(I32, (nc, m), 0)
    t_idx = i * tq + lax.broadcasted_iota(I32, (nc, m), 1) % tq
    valid = (n_idx + 1) * BLK_CMP - 1 <= t_idx
    st = jnp.where(valid, _dot_nt(kc, qs), NEG)
    mx = jnp.max(st, axis=0, keepdims=True)
    e = jnp.where(valid, jnp.exp(st - mx), 0.0)
    den = jnp.sum(e, axis=0, keepdims=True)
    p = e / jnp.where(den > 0.0, den, 1.0)
    o_cmp = _dot(vct, p.astype(BF))
    imp = []
    for g in range(NSA_KV):
        acc_g = p[:, g * NSA_REP * tq:(g * NSA_REP + 1) * tq]
        for h in range(g * NSA_REP + 1, (g + 1) * NSA_REP):
            acc_g = acc_g + p[:, h * tq:(h + 1) * tq]
        imp.append(acc_g)

    pair = (lax.broadcasted_iota(I32, (ns, nc), 1) // (BLK_SEL // BLK_CMP)
            == lax.broadcasted_iota(I32, (ns, nc), 0)).astype(BF)
    blk = lax.broadcasted_iota(I32, (ns, tq), 0)
    tb = i * tq + lax.broadcasted_iota(I32, (ns, tq), 1)
    forced = (blk == 0) | (blk == tb // BLK_SEL)
    future = blk * BLK_SEL > tb
    eye = (lax.broadcasted_iota(I32, (tq, tq), 0) == lax.broadcasted_iota(I32, (tq, tq), 1)).astype(BF)
    pens = []
    for g in range(NSA_KV):
        a, b, c = _split3(imp[g])
        v = _dot(pair, a) + _dot(pair, b) + _dot(pair, c)
        v = jnp.where(forced, FORCE_SCORE, jnp.where(future, -1.0, v))
        key = _order_key(v)
        imp_scr[g] = key

        def body(j, cnt, g=g, key=key):
            return cnt + _beats(imp_scr[g, pl.ds(j, 1), :], key, blk > j)

        cnt = lax.fori_loop(0, ns, body, jnp.zeros((ns, tq), I32), unroll=8)
        sel_t = (cnt < min(TOP_N, ns)).astype(BF)
        pen = _dot_nt(eye, sel_t)
        pen = ((pen - 1.0) * (-NEG)).astype(BF)
        if ns < LANES:
            pen = jnp.concatenate([pen, jnp.zeros((tq, LANES - ns), BF)], axis=1)
        pens.append(pen)
    pen_rows = jnp.concatenate([pens[h // NSA_REP] for h in range(NSA_HEADS)], axis=0)
    qs2 = jnp.concatenate([qs, pen_rows], axis=1)

    q_pos = i * tq + lax.broadcasted_iota(I32, (1, m), 1) % tq

    def online(scores, vals_t, mask):
        if mask is not None:
            scores = jnp.where(mask, scores, NEG)
        m_old = m_scr[...]
        m_new = jnp.maximum(m_old, jnp.max(scores, axis=0, keepdims=True))
        alpha = jnp.exp(m_old - m_new)
        p = jnp.exp(scores - m_new)
        l_scr[...] = alpha * l_scr[...] + jnp.sum(p, axis=0, keepdims=True)
        acc_scr[...] = alpha * acc_scr[...] + _dot(vals_t, p.astype(BF))
        m_scr[...] = m_new

    def reset():
        m_scr[...] = jnp.full_like(m_scr, NEG)
        l_scr[...] = jnp.zeros_like(l_scr)
        acc_scr[...] = jnp.zeros_like(acc_scr)

    def result():
        return acc_scr[...] / l_scr[...]

    reset()
    nt = (i * tq) // tk + 1
    row_k = lax.broadcasted_iota(I32, (tk, 1), 0)

    def scores_into(buf, j):
        off = pl.multiple_of(j * tk, tk)
        buf[...] = _dot_nt(k2_scr[pl.ds(off, tk), :], qs2)

    def diag_mask():
        return ((nt - 1) * tk + row_k) <= q_pos

    n_pairs = (nt - 1) // 2
    scores_into(sa_scr, 0)

    def slc_body(pr, carry):
        scores_into(sb_scr, 2 * pr + 1)
        online(sa_scr[...], svt_scr[2 * pr], None)
        scores_into(sa_scr, 2 * pr + 2)
        online(sb_scr[...], svt_scr[2 * pr + 1], None)
        return carry

    lax.fori_loop(0, n_pairs, slc_body, 0)

    @pl.when((nt - 1) % 2 == 1)
    def _():
        scores_into(sb_scr, nt - 1)
        online(sa_scr[...], svt_scr[nt - 2], None)
        online(sb_scr[...], svt_scr[nt - 1], diag_mask())

    @pl.when((nt - 1) % 2 == 0)
    def _():
        online(sa_scr[...], svt_scr[nt - 1], diag_mask())

    o_slc = result()

    row_w = lax.broadcasted_iota(I32, (tw, 1), 0)
    n_band = (WINDOW + tq + tw - 1) // tw
    first = jnp.maximum(i * tq - WINDOW, 0) // tw
    s_w = []
    for c in range(n_band):
        off = pl.multiple_of((first + c) * tw, tw)
        k_pos = off + row_w
        s = _dot_nt(wk_ref[pl.ds(off, tw), :], qs)
        s_w.append(jnp.where((k_pos <= q_pos) & (q_pos - k_pos <= WINDOW), s, NEG))
    m_w = s_w[0].max(axis=0, keepdims=True)
    for s in s_w[1:]:
        m_w = jnp.maximum(m_w, s.max(axis=0, keepdims=True))
    l_w = jnp.zeros((1, m), F32)
    o_win = jnp.zeros((LANES, m), F32)
    for c, s in enumerate(s_w):
        p = jnp.exp(s - m_w)
        l_w = l_w + p.sum(axis=0, keepdims=True)
        o_win = o_win + _dot(wvt_scr[first + c], p.astype(BF))
    o_win = o_win / l_w

    gt = jnp.transpose(gate_ref[...])

    def gate_row(branch):
        return jnp.concatenate([gt[h * 3 + branch:h * 3 + branch + 1, :] for h in range(NSA_HEADS)], axis=1)

    mix = gate_row(0) * o_cmp + gate_row(1) * o_slc + gate_row(2) * o_win
    stacked = jnp.concatenate([jnp.transpose(mix[:, h * tq:(h + 1) * tq]) for h in range(NSA_HEADS)], axis=0)
    o_ref[...] = _unstack_heads(stacked, tq).astype(BF)


def _nsa_prompt(nq, gates, kc, vc, sk, sv, wk, wv, *, b, t, tq=128, tk=512, tw=128):
    assert t % tk == 0 and (WINDOW + tq + tw - 1) // tw <= t // tw
    n = b * t
    nqb = t // tq
    nc = t // BLK_CMP
    ns = t // BLK_SEL
    m = NSA_HEADS * tq
    tok = lambda w: pl.BlockSpec((tq, w), lambda bi, i: (bi * nqb + i, 0))
    seq = lambda rows, w: pl.BlockSpec((rows, w), lambda bi, i: (bi, 0))
    return pl.pallas_call(
        functools.partial(_nsa_kernel, tq=tq, tk=tk, tw=tw, t=t),
        out_shape=jax.ShapeDtypeStruct((n, D_NSA), BF),
        grid=(b, nqb),
        in_specs=[tok(D_NSA), tok(LANES), seq(nc, LANES), seq(nc, LANES),
                  seq(t, LANES), seq(t, LANES), seq(t, LANES), seq(t, LANES)],
        out_specs=tok(D_NSA),
        scratch_shapes=[pltpu.VMEM((t, 2 * LANES), BF), pltpu.VMEM((t // tk, LANES, tk), BF),
                        pltpu.VMEM((t // tw, LANES, tw), BF), pltpu.VMEM((NSA_KV, ns, tq), F32),
                        pltpu.VMEM((1, m), F32), pltpu.VMEM((1, m), F32), pltpu.VMEM((LANES, m), F32),
                        pltpu.VMEM((tk, m), F32), pltpu.VMEM((tk, m), F32)],
        compiler_params=_cparams(("parallel", "arbitrary")),
        name="nsa_prompt",
    )(nq, gates, kc, vc, sk, sv, wk, wv)


def _out_mlp_kernel(x_ref, osb_ref, oret_ref, onsa_ref, wo_ref, g2_ref, wu_ref, wd_ref, y_ref, *, ff_chunk):
    h = (x_ref[...] + _dot(osb_ref[...].astype(BF), wo_ref[0:D_SB, :])
         + _dot(oret_ref[...].astype(BF), wo_ref[D_SB:D_SB + D_RET, :])
         + _dot(onsa_ref[...].astype(BF), wo_ref[D_SB + D_RET:, :]))
    ms = jnp.mean(h * h, axis=-1, keepdims=True)
    hn = (h * lax.rsqrt(ms + EPS) * g2_ref[...]).astype(BF)
    mlp = None
    d_ff = wu_ref.shape[1]
    for c0 in range(0, d_ff, ff_chunk):
        u = jnp.maximum(_dot(hn, wu_ref[:, c0:c0 + ff_chunk]), 0.0)
        part = _dot((u * u).astype(BF), wd_ref[c0:c0 + ff_chunk, :])
        mlp = part if mlp is None else mlp + part
    y_ref[...] = h + mlp


def _out_mlp(x2d, osb, oret, onsa, wo, g2, wu, wd, *, tm, ff_chunk=1024):
    n, d = x2d.shape
    tok = lambda w: pl.BlockSpec((tm, w), lambda i: (i, 0))
    return pl.pallas_call(
        functools.partial(_out_mlp_kernel, ff_chunk=ff_chunk),
        out_shape=jax.ShapeDtypeStruct((n, d), F32),
        grid=(n // tm,),
        in_specs=[tok(d), tok(D_SB), tok(D_RET), tok(D_NSA), _const_spec(wo.shape), _const_spec((1, d)),
                  _const_spec(wu.shape), _const_spec(wd.shape)],
        out_specs=tok(d),
        compiler_params=_cparams(("parallel",)),
        name="out_mlp",
    )(x2d, osb, oret, onsa, wo, g2, wu, wd)


PAGES_PER_STEP = 8
DEC_ROWS = 16


def _pages_t(c):
    dp, npool, pg, h, dd = c.shape
    return jnp.transpose(c, (0, 1, 3, 4, 2)).reshape(dp, npool, h * dd, pg)


def _page_spec(layer, width, page_of_step, pg):
    def imap(bi, p, pt, *rest):
        return (layer, pt[bi, page_of_step(p)], 0, 0)
    return pl.BlockSpec((None, None, width, pg), imap)


def _sb_dec_kernel(pt_ref, q_ref, *refs, npp, last):
    k_refs, v_refs = refs[:npp], refs[npp:2 * npp]
    o_ref, c_scr, acc_scr, alive_scr = refs[2 * npp:]
    p = pl.program_id(1)
    rows = DEC_ROWS
    pg = k_refs[0].shape[1]

    @pl.when(p == 0)
    def _():
        c_scr[...] = jnp.zeros_like(c_scr)
        acc_scr[...] = jnp.zeros_like(acc_scr)
        alive_scr[0] = 1

    lane_h = lax.broadcasted_iota(I32, (rows, D_SB), 1) // HEAD_DIM
    row = lax.broadcasted_iota(I32, (rows, D_SB), 0)

    @pl.when(alive_scr[0] == 1)
    def _():
        q = jnp.broadcast_to(q_ref[...].astype(F32), (rows, D_SB))
        qs = jnp.where(lane_h == row, q, 0.0).astype(BF)
        tri = (lax.broadcasted_iota(I32, (pg, pg), 0) > lax.broadcasted_iota(I32, (pg, pg), 1)).astype(BF)
        c = c_scr[...]
        acc = acc_scr[...]
        for kk in range(npp):
            z = _dot(qs, k_refs[kk][...].astype(BF))
            l1m = -_softplus(z)
            after = c + _dot_x2(l1m, tri)
            w = jnp.exp(z + l1m + after)
            acc = acc + _dot_nt(w.astype(BF), v_refs[kk][...].astype(BF))
            c = c + jnp.sum(l1m, axis=1, keepdims=True)
        c_scr[...] = c
        acc_scr[...] = acc
        head_rows = lax.broadcasted_iota(I32, (rows, 1), 0) < SB_HEADS
        alive_scr[0] = (jnp.max(jnp.where(head_rows, c, NEG)) > SB_DEAD).astype(I32)

    @pl.when(p == last)
    def _():
        o_ref[...] = jnp.sum(jnp.where(lane_h == row, acc_scr[...], 0.0), axis=0, keepdims=True)


def _sb_decode(layer, page_table, q, cache_k, cache_v):
    nb, npg = page_table.shape
    npp = math.gcd(PAGES_PER_STEP, npg)
    steps = npg // npp
    pg = cache_k.shape[3]
    specs_k = [_page_spec(layer, D_SB, (lambda p, kk=kk: npg - 1 - (p * npp + kk)), pg) for kk in range(npp)]
    specs_v = [_page_spec(layer, D_SB, (lambda p, kk=kk: npg - 1 - (p * npp + kk)), pg) for kk in range(npp)]
    row = pl.BlockSpec((None, 1, D_SB), lambda bi, p, pt: (bi, 0, 0))
    return pl.pallas_call(
        functools.partial(_sb_dec_kernel, npp=npp, last=steps - 1),
        out_shape=jax.ShapeDtypeStruct((nb, 1, D_SB), F32),
        grid_spec=pltpu.PrefetchScalarGridSpec(
            num_scalar_prefetch=1, grid=(nb, steps),
            in_specs=[row] + specs_k + specs_v, out_specs=row,
            scratch_shapes=[pltpu.VMEM((DEC_ROWS, 1), F32), pltpu.VMEM((DEC_ROWS, D_SB), F32),
                            pltpu.SMEM((1,), I32)]),
        compiler_params=_cparams(("parallel", "arbitrary")),
        name="sb_decode",
    )(page_table, q, *([cache_k] * npp), *([cache_v] * npp))


def _cmp_dec_kernel(pt_ref, kn_ref, vn_ref, pk_ref, pv_ref, pkt_ref, pvt_ref, wk_ref, wv_ref, *refs,
                    npp, last, nblk):
    k_refs, v_refs = refs[:npp], refs[npp:2 * npp]
    ko_ref, vo_ref = refs[2 * npp:]
    p = pl.program_id(1)
    wk_hi, wk_lo = _split2(wk_ref[...])
    wv_hi, wv_lo = _split2(wv_ref[...])
    pg = k_refs[0].shape[1]
    per = npp * pg // BLK_CMP
    r0 = pl.multiple_of(p * per, per)
    ko_ref[pl.ds(r0, per), :] = _compress_t(
        jnp.concatenate([r[...] for r in k_refs], axis=1), pkt_ref[...], wk_hi, wk_lo)
    vo_ref[pl.ds(r0, per), :] = _compress_t(
        jnp.concatenate([r[...] for r in v_refs], axis=1), pvt_ref[...], wv_hi, wv_lo)

    @pl.when(p == last)
    def _():
        first = lax.broadcasted_iota(I32, (DEC_ROWS, LANES), 0) == 0
        xk = jnp.where(first, kn_ref[...] * pk_ref[0:1, :], 0.0)
        xv = jnp.where(first, vn_ref[...] * pv_ref[0:1, :], 0.0)
        a, b = _split2(xk)
        ko_ref[nblk:nblk + DEC_ROWS, :] = _dot(a, wk_hi) + _dot(b, wk_hi) + _dot(a, wk_lo)
        a, b = _split2(xv)
        vo_ref[nblk:nblk + DEC_ROWS, :] = _dot(a, wv_hi) + _dot(b, wv_hi) + _dot(a, wv_lo)


def _cmp_decode(layer, page_table, ck_new, cv_new, cache_k, cache_v, pk, pv, pkt, pvt, wk, wv):
    nb, npg = page_table.shape
    pg = cache_k.shape[3]
    npp = CMP_SPAN // pg
    assert npg % npp == 0
    steps = npg // npp
    nblk = npg * pg // BLK_CMP
    specs = [_page_spec(layer, D_KV, (lambda p, kk=kk: p * npp + kk), pg) for kk in range(npp)]
    row = pl.BlockSpec((None, 1, D_KV), lambda bi, p, pt: (bi, 0, 0))
    cst = lambda a: pl.BlockSpec(a.shape, lambda bi, p, pt: (0,) * a.ndim)
    out = pl.BlockSpec((None, nblk + DEC_ROWS, LANES), lambda bi, p, pt: (bi, 0, 0))
    return pl.pallas_call(
        functools.partial(_cmp_dec_kernel, npp=npp, last=steps - 1, nblk=nblk),
        out_shape=[jax.ShapeDtypeStruct((nb, nblk + DEC_ROWS, LANES), F32)] * 2,
        grid_spec=pltpu.PrefetchScalarGridSpec(
            num_scalar_prefetch=1, grid=(nb, steps),
            in_specs=[row, row, cst(pk), cst(pv), cst(pkt), cst(pvt), cst(wk), cst(wv)] + specs + specs,
            out_specs=[out, out]),
        compiler_params=_cparams(("parallel", "arbitrary")),
        name="cmp_decode",
    )(page_table, ck_new, cv_new, pk, pv, pkt, pvt, wk, wv, *([cache_k] * npp), *([cache_v] * npp))


def _stack_heads_q_row(q):
    qf = q.astype(F32)
    row = lax.broadcasted_iota(I32, (DEC_ROWS, LANES), 0)
    lane_g = lax.broadcasted_iota(I32, (DEC_ROWS, LANES), 1) // HEAD_DIM
    out = jnp.zeros((DEC_ROWS, LANES), F32)
    for h in range(NSA_HEADS):
        g = h // NSA_REP
        blk = jnp.broadcast_to(qf[:, (h // 2) * LANES:(h // 2 + 1) * LANES], (DEC_ROWS, LANES))
        if h % 2 != g:
            blk = pltpu.roll(blk, HEAD_DIM, 1)
        out = jnp.where((row == h) & (lane_g == g), blk, out)
    return out.astype(BF)


def _unstack_heads_row(acc):
    lane = lax.broadcasted_iota(I32, (1, LANES), 1)
    accr = pltpu.roll(acc, HEAD_DIM, 1)
    cols = []
    for c in range(NSA_HEADS // 2):
        g = (2 * c) // NSA_REP
        if g == 0:
            cols.append(jnp.where(lane < HEAD_DIM, acc[2 * c:2 * c + 1], accr[2 * c + 1:2 * c + 2]))
        else:
            cols.append(jnp.where(lane < HEAD_DIM, accr[2 * c:2 * c + 1], acc[2 * c + 1:2 * c + 2]))
    return jnp.concatenate(cols, axis=1)


def _sel_dec_kernel(q_ref, kc_ref, vc_ref, ocmp_ref, idx_ref, imp_scr, *, q_pos, nblk, ns_pad):
    nrow = kc_ref.shape[0]
    qs = _stack_heads_q_row(q_ref[...])
    qs = jnp.concatenate([qs, jnp.zeros((LANES - DEC_ROWS, LANES), BF)], axis=0)
    kc = kc_ref[...].astype(BF)
    st = _dot_nt(kc, qs)
    n_idx = lax.broadcasted_iota(I32, (nrow, LANES), 0)
    valid = (n_idx + 1) * BLK_CMP - 1 <= q_pos
    st = jnp.where(valid, st, NEG)
    mx = jnp.max(st, axis=0, keepdims=True)
    e = jnp.where(valid, jnp.exp(st - mx), 0.0)
    den = jnp.sum(e, axis=0, keepdims=True)
    p = e / jnp.where(den > 0.0, den, 1.0)
    ocmp_ref[...] = _dot_tn(p.astype(BF), vc_ref[...].astype(BF))[0:NSA_HEADS]

    grp = (lax.broadcasted_iota(I32, (LANES, LANES), 0) // NSA_REP
           == lax.broadcasted_iota(I32, (LANES, LANES), 1)).astype(BF)
    a, b, c = _split3(p)
    impc = _dot(a, grp) + _dot(b, grp) + _dot(c, grp)
    pair = (lax.broadcasted_iota(I32, (ns_pad, nrow), 1) // (BLK_SEL // BLK_CMP)
            == lax.broadcasted_iota(I32, (ns_pad, nrow), 0)).astype(BF)
    a, b, c = _split3(impc)
    v = _dot(pair, a) + _dot(pair, b) + _dot(pair, c)
    ns = (nblk * BLK_CMP + 1 + BLK_SEL - 1) // BLK_SEL
    blk = lax.broadcasted_iota(I32, (ns_pad, LANES), 0)
    forced = (blk == 0) | (blk == q_pos // BLK_SEL)
    future = blk * BLK_SEL > q_pos
    v = jnp.where(forced, FORCE_SCORE, jnp.where(future, -1.0, v))
    v = jnp.where(blk < ns, v, -1.0)
    key = _order_key(v)
    imp_scr[...] = key

    def body(j, cnt):
        return cnt + _beats(imp_scr[pl.ds(j, 1), :], key, blk > j)

    cnt = lax.fori_loop(0, ns_pad, body, jnp.zeros((ns_pad, LANES), I32), unroll=8)
    blk_f = blk.astype(F32)
    rows = [jnp.sum(jnp.where(cnt == r, blk_f, 0.0), axis=0, keepdims=True) for r in range(TOP_N)]
    idx_ref[...] = jnp.concatenate(rows, axis=0).astype(I32)


def _sel_decode(nq, kc, vc, *, q_pos, nblk):
    nb = nq.shape[0]
    nrow = kc.shape[1]
    ns_pad = ((nrow // 2 + 7) // 8) * 8
    return pl.pallas_call(
        functools.partial(_sel_dec_kernel, q_pos=q_pos, nblk=nblk, ns_pad=ns_pad),
        out_shape=[jax.ShapeDtypeStruct((nb, NSA_HEADS, LANES), F32), jax.ShapeDtypeStruct((nb, TOP_N, LANES), I32)],
        grid=(nb,),
        in_specs=[pl.BlockSpec((None, 1, D_NSA), lambda bi: (bi, 0, 0)),
                  pl.BlockSpec((None, nrow, LANES), lambda bi: (bi, 0, 0)),
                  pl.BlockSpec((None, nrow, LANES), lambda bi: (bi, 0, 0))],
        out_specs=[pl.BlockSpec((None, NSA_HEADS, LANES), lambda bi: (bi, 0, 0)),
                   pl.BlockSpec((None, TOP_N, LANES), lambda bi: (bi, 0, 0))],
        scratch_shapes=[pltpu.VMEM((ns_pad, LANES), F32)],
        compiler_params=_cparams(("parallel",)),
        name="sel_decode",
    )(nq, kc, vc)


def _nsa_dec_kernel(pt_ref, idx_ref, q_ref, gate_ref, ocmp_ref, skn_ref, svn_ref, wkn_ref, wvn_ref,
                    wkc_ref, wvc_ref, wk_ref, wv_ref, ck_hbm, cv_hbm, o_ref, wko_ref, wvo_ref, kbuf, vbuf, sem,
                    *, layer, ncache, per):
    nsel = NSA_KV * TOP_N
    bi = pl.program_id(0)
    pg = kbuf.shape[2]

    def page_copies(j):
        blk = jnp.minimum(idx_ref[bi, j], ncache - 1)
        page = pt_ref[bi, blk // per]
        return (pltpu.make_async_copy(ck_hbm.at[layer, page], kbuf.at[j], sem.at[0, j]),
                pltpu.make_async_copy(cv_hbm.at[layer, page], vbuf.at[j], sem.at[1, j]))

    for j in range(nsel):
        for cp in page_copies(j):
            cp.start()

    qs = _stack_heads_q_row(q_ref[...])
    qf = qs.astype(F32)
    row_g = lax.broadcasted_iota(I32, (DEC_ROWS, LANES), 0) // NSA_REP

    def attend(s, vt, s_new, v_new):
        mx = jnp.maximum(s_new, jnp.max(s, axis=1, keepdims=True))
        p_new = jnp.exp(s_new - mx)
        p = jnp.exp(s - mx)
        den = p_new + jnp.sum(p, axis=1, keepdims=True)
        return (p_new * v_new + _dot_nt(p.astype(BF), vt)) / den

    bf_row = lambda r: r[...].astype(BF).astype(F32)
    wk = wk_ref[...]
    wv = wv_ref[...]
    s_wn = jnp.sum(qf * bf_row(wkn_ref), axis=1, keepdims=True)
    o_win = attend(_dot(qs, wk.astype(BF)), wv.astype(BF), s_wn, bf_row(wvn_ref))

    wb = wk.shape[1]
    last = lax.broadcasted_iota(I32, (D_KV, wb), 1) == wb - 1
    wko_ref[...] = jnp.where(last, wkc_ref[...], pltpu.roll(wk, wb - 1, 1))
    wvo_ref[...] = jnp.where(last, wvc_ref[...], pltpu.roll(wv, wb - 1, 1))

    for j in range(nsel):
        for cp in page_copies(j):
            cp.wait()

    s_new = jnp.sum(qf * bf_row(skn_ref), axis=1, keepdims=True)
    nkeys = TOP_N * pg
    col = lax.broadcasted_iota(I32, (DEC_ROWS, nkeys), 1)
    col_slot = col // pg
    col_blk = (col % pg) // BLK_SEL
    o_slc = jnp.zeros((DEC_ROWS, LANES), F32)
    for g in range(NSA_KV):
        pen = jnp.full((DEC_ROWS, nkeys), NEG, F32)
        for r in range(TOP_N):
            blk = idx_ref[bi, g * TOP_N + r]
            want = jnp.where(blk < ncache, blk % per, -1)
            pen = jnp.where((col_slot == r) & (col_blk == want), 0.0, pen)
        kcat = jnp.concatenate([kbuf[g * TOP_N + r] for r in range(TOP_N)], axis=1).astype(BF)
        vcat = jnp.concatenate([vbuf[g * TOP_N + r] for r in range(TOP_N)], axis=1).astype(BF)
        o_g = attend(_dot(qs, kcat) + pen, vcat, s_new, bf_row(svn_ref))
        o_slc = jnp.where(row_g == g, o_g, o_slc)

    gates = jnp.broadcast_to(gate_ref[...], (DEC_ROWS, LANES))
    out = (_gate_expand(gates, 0)[0:1] * _unstack_heads_row(ocmp_ref[...])
           + _gate_expand(gates, 1)[0:1] * _unstack_heads_row(o_slc)
           + _gate_expand(gates, 2)[0:1] * _unstack_heads_row(o_win))
    o_ref[...] = out


def _nsa_decode(layer, page_table, idx, nq, gates, ocmp, sk_new, sv_new, wk_new, wv_new, win_k, win_v,
                cache_k, cache_v):
    nb, npg = page_table.shape
    pg = cache_k.shape[3]
    per = pg // BLK_SEL
    ncache = npg * per
    wb = win_k.shape[3]
    nsel = NSA_KV * TOP_N

    row = lambda w: pl.BlockSpec((None, 1, w), lambda bi, pt, ix: (bi, 0, 0))
    colspec = pl.BlockSpec((None, D_KV, 1), lambda bi, pt, ix: (bi, 0, 0))
    win = pl.BlockSpec((None, None, D_KV, wb), lambda bi, pt, ix: (layer, bi, 0, 0))
    wout = pl.BlockSpec((None, D_KV, wb), lambda bi, pt, ix: (bi, 0, 0))
    hbm = pl.BlockSpec(memory_space=pl.ANY)
    col = lambda a: a.reshape(nb, D_KV, 1)
    return pl.pallas_call(
        functools.partial(_nsa_dec_kernel, layer=layer, ncache=ncache, per=per),
        out_shape=[jax.ShapeDtypeStruct((nb, 1, D_NSA), F32), jax.ShapeDtypeStruct((nb, D_KV, wb), F32),
                   jax.ShapeDtypeStruct((nb, D_KV, wb), F32)],
        grid_spec=pltpu.PrefetchScalarGridSpec(
            num_scalar_prefetch=2, grid=(nb,),
            in_specs=[row(D_NSA), row(LANES), pl.BlockSpec((None, NSA_HEADS, LANES), lambda bi, pt, ix: (bi, 0, 0)),
                      row(D_KV), row(D_KV), row(D_KV), row(D_KV), colspec, colspec, win, win, hbm, hbm],
            out_specs=[row(D_NSA), wout, wout],
            scratch_shapes=[pltpu.VMEM((nsel, D_KV, pg), F32), pltpu.VMEM((nsel, D_KV, pg), F32),
                            pltpu.SemaphoreType.DMA((2, nsel))]),
        compiler_params=_cparams(("arbitrary",)),
        name="nsa_decode",
    )(page_table, idx, nq, gates, ocmp, sk_new, sv_new, wk_new, wv_new, col(wk_new), col(wv_new),
      win_k, win_v, cache_k, cache_v)


def _ret_dec_kernel(q_ref, k_ref, kcol_ref, v_ref, g_ref, gcol_ref, grow_ref, st_ref, o_ref, sto_ref):
    rows = DEC_ROWS
    st = st_ref[...]
    q = jnp.broadcast_to(q_ref[...], (rows, D_RET))
    v = jnp.broadcast_to(v_ref[...], (rows, D_RET))
    lane_h = lax.broadcasted_iota(I32, (rows, D_RET), 1) // HEAD_DIM
    row = lax.broadcasted_iota(I32, (rows, D_RET), 0)
    own = lane_h == row
    qs = jnp.where(own, q, 0.0).astype(BF)
    spread = (lax.broadcasted_iota(I32, (HEAD_DIM, D_RET), 1) % HEAD_DIM
              == lax.broadcasted_iota(I32, (HEAD_DIM, D_RET), 0)).astype(BF)
    shi, slo = _split2(st)
    cross = _dot(qs, shi) + _dot(qs, slo)
    o_cross = jnp.sum(jnp.where(own, _dot_x2(cross, spread), 0.0), axis=0, keepdims=True) * grow_ref[...]
    seg = _seg_mean_mat(D_RET)
    qk = q * jnp.broadcast_to(k_ref[...], (rows, D_RET))
    o = o_cross + (_dot_x2(qk, seg) * float(HEAD_DIM)) * v

    v4 = _dot_nt(jnp.where(own, v, 0.0).astype(BF), spread)
    pick = (lax.broadcasted_iota(I32, (D_RET, rows), 0) // HEAD_DIM
            == lax.broadcasted_iota(I32, (D_RET, rows), 1)).astype(BF)
    vexp = _dot(pick, v4.astype(BF))
    sto_ref[...] = st * gcol_ref[...] + kcol_ref[...] * vexp

    mu = _dot_x2(o, seg)
    d = o - mu
    var = _dot_x2(d * d, seg)
    y = d * lax.rsqrt(var + EPS)
    o_ref[...] = y[0:1] * jax.nn.silu(g_ref[...])


def _ret_decode(layer, q, k, v, g, state):
    nb = q.shape[0]
    log_gamma = np.log1p(-np.exp2(-5.0 - np.arange(RET_HEADS, dtype=np.float64)))
    gam = np.repeat(np.exp(log_gamma), HEAD_DIM)
    gcol = jnp.asarray(gam[:, None], F32)
    grow = jnp.asarray(gam[None, :], F32)
    kcol = k.reshape(nb, D_RET, 1)
    row = pl.BlockSpec((None, 1, D_RET), lambda bi: (bi, 0, 0))
    return pl.pallas_call(
        _ret_dec_kernel,
        out_shape=[jax.ShapeDtypeStruct((nb, 1, D_RET), F32), jax.ShapeDtypeStruct((nb, D_RET, HEAD_DIM), F32)],
        grid=(nb,),
        in_specs=[row, row, pl.BlockSpec((None, D_RET, 1), lambda bi: (bi, 0, 0)), row, row,
                  _const_spec(gcol.shape), _const_spec(grow.shape),
                  pl.BlockSpec((None, None, D_RET, HEAD_DIM), lambda bi: (layer, bi, 0, 0))],
        out_specs=[row, pl.BlockSpec((None, D_RET, HEAD_DIM), lambda bi: (bi, 0, 0))],
        compiler_params=_cparams(("parallel",)),
        name="ret_decode",
    )(q, k, kcol, v, g, gcol, grow, state)


def _swap_perm(rot_dim):
    half = rot_dim // 2
    p = np.arange(HEAD_DIM)
    p[:half] = np.arange(half, rot_dim)
    p[half:rot_dim] = np.arange(half)
    return p


def _rope_tables(pos, rot_dim, theta, width):
    half = rot_dim // 2
    inv = jnp.exp(-math.log(theta) * jnp.arange(half, dtype=F32) / half)
    ang = pos.astype(F32)[:, None] * inv[None, :]
    cos, sin = jnp.cos(ang), jnp.sin(ang)
    ones = jnp.ones((pos.shape[0], HEAD_DIM - rot_dim), F32)
    c = jnp.concatenate([cos, cos, ones], axis=1)
    s = jnp.concatenate([-sin, sin, 0.0 * ones], axis=1)
    reps = width // HEAD_DIM
    return jnp.tile(c, (1, reps)), jnp.tile(s, (1, reps))


def _layer_params(l, norm1_g, w_in, nsa_q_norm, nsa_k_norm, cmp_pos_k, cmp_pos_v, cmp_w_k, cmp_w_v, w_out,
                  norm2_g, w_up, w_down):
    w = w_in[l]
    d = w.shape[0]
    wm = jnp.pad(w, ((0, 0), (0, _N_IN_PAD - _N_IN))).astype(BF)
    gains = jnp.concatenate([nsa_q_norm[l][None], nsa_k_norm[l]], axis=0)
    gn = jnp.tile(gains, (1, LANES // HEAD_DIM))
    gs = jnp.tile(gains[:, _swap_perm(ROT_DIM)], (1, LANES // HEAD_DIM))
    eye2 = jnp.eye(NSA_KV, dtype=F32)
    return dict(
        g1=norm1_g[l][None], wm=wm, gn=gn, gs=gs,
        pk=jnp.tile(cmp_pos_k[l], (1, NSA_KV)), pv=jnp.tile(cmp_pos_v[l], (1, NSA_KV)),
        pkt=jnp.tile(cmp_pos_k[l].T, (NSA_KV, CMP_SPAN // BLK_CMP)),
        pvt=jnp.tile(cmp_pos_v[l].T, (NSA_KV, CMP_SPAN // BLK_CMP)),
        phik=jnp.kron(eye2, cmp_w_k[l]), phiv=jnp.kron(eye2, cmp_w_v[l]),
        wo=w_out[l].astype(BF), g2=norm2_g[l][None], wu=w_up[l].astype(BF), wd=w_down[l].astype(BF))


def _prompt_layer(xp2d, prm, tabs, *, b, t, tm):
    (sbk_f, sbv_f, ck_f, cv_f, sk_f, sv_f, wk_f, wv_f,
     sbq_b, sbk_b, sbv_b, rq_b, rk_b, rv_b, rg_f, nq_b, sk_b, sv_b, wk_b, wv_b, ng_f) = _project(
        xp2d, prm['g1'], prm['wm'], tabs, prm['gn'], prm['gs'], tm=tm, tab_blocks=t // tm, seqs=b)
    o_sb = _sb_prompt(sbq_b, sbk_b, sbv_b, b=b, t=t)
    o_ret, ret_st = _ret_prompt(rq_b, rk_b, rv_b, rg_f, b=b, t=t)
    kc, vc = _compress_prompt(ck_f, cv_f, prm['pkt'], prm['pvt'], prm['phik'], prm['phiv'])
    o_nsa = _nsa_prompt(nq_b, ng_f, kc, vc, sk_b, sv_b, wk_b, wv_b, b=b, t=t)
    y = _out_mlp(xp2d, o_sb, o_ret, o_nsa, prm['wo'], prm['g2'], prm['wu'], prm['wd'], tm=tm)
    keep = min(WINDOW, t)
    r4 = lambda a, h: jnp.transpose(a.reshape(b, h, HEAD_DIM, a.shape[-1]), (0, 3, 1, 2))
    st = ret_st.reshape(b, RET_HEADS, HEAD_DIM, RET_HEADS, HEAD_DIM)
    st = jnp.stack([st[:, h, :, h, :] for h in range(RET_HEADS)], axis=1)
    caches = dict(
        p_sb_k=r4(sbk_f, SB_HEADS), p_sb_v=r4(sbv_f, SB_HEADS),
        p_cmp_k=r4(ck_f, NSA_KV), p_cmp_v=r4(cv_f, NSA_KV),
        p_slc_k=r4(sk_f, NSA_KV), p_slc_v=r4(sv_f, NSA_KV),
        p_win_k=r4(wk_f[:, :, t - keep:], NSA_KV), p_win_v=r4(wv_f[:, :, t - keep:], NSA_KV),
        p_ret=st)
    return y, caches


def _sample_layer(l, xs2d, prm, tabs, caches, states, page_table, *, past_len):
    nb = xs2d.shape[0]
    (sbk_f, sbv_f, ck_f, cv_f, sk_f, sv_f, wk_f, wv_f,
     sbq_b, sbk_b, sbv_b, rq_b, rk_b, rv_b, rg_f, nq_b, sk_b, sv_b, wk_b, wv_b, ng_f) = _project(
        xs2d, prm['g1'], prm['wm'], tabs, prm['gn'], prm['gs'], tm=nb, tab_blocks=1)
    row = lambda a: a.astype(F32).reshape(nb, 1, a.shape[-1])
    c_sb_k, c_sb_v, c_cmp_k, c_cmp_v, c_slc_k, c_slc_v = caches
    win_k, win_v, st_ret = states
    pg = c_cmp_k.shape[3]
    nblk = page_table.shape[1] * pg // BLK_CMP
    o_sb = _sb_decode(l, page_table, row(sbq_b), c_sb_k, c_sb_v)
    o_ret, ret_new = _ret_decode(l, row(rq_b), row(rk_b), row(rv_b), row(rg_f), st_ret)
    kc, vc = _cmp_decode(l, page_table, row(ck_f), row(cv_f), c_cmp_k, c_cmp_v,
                         prm['pk'], prm['pv'], prm['pkt'], prm['pvt'], prm['phik'], prm['phiv'])
    ocmp, idx = _sel_decode(row(nq_b), kc, vc, q_pos=past_len, nblk=nblk)
    idx = jnp.transpose(idx[:, :, :NSA_KV], (0, 2, 1)).reshape(nb, NSA_KV * TOP_N)
    o_nsa, win_k_new, win_v_new = _nsa_decode(
        l, page_table, idx, row(nq_b), row(ng_f), ocmp, row(sk_f), row(sv_f), row(wk_f), row(wv_f),
        win_k, win_v, c_slc_k, c_slc_v)
    y = _out_mlp(xs2d, o_sb.reshape(nb, D_SB), o_ret.reshape(nb, D_RET), o_nsa.reshape(nb, D_NSA),
                 prm['wo'], prm['g2'], prm['wu'], prm['wd'], tm=nb)
    r4 = lambda a, h: a.reshape(nb, 1, h, HEAD_DIM)
    wb = win_k.shape[3]
    win4 = lambda a: jnp.transpose(a.reshape(nb, NSA_KV, HEAD_DIM, wb), (0, 3, 1, 2))
    out = dict(
        s_sb_k=r4(sbk_f, SB_HEADS), s_sb_v=r4(sbv_f, SB_HEADS),
        s_cmp_k=r4(ck_f, NSA_KV), s_cmp_v=r4(cv_f, NSA_KV),
        s_slc_k=r4(sk_f, NSA_KV), s_slc_v=r4(sv_f, NSA_KV),
        s_win_k=win4(win_k_new), s_win_v=win4(win_v_new),
        s_ret=ret_new.reshape(nb, RET_HEADS, HEAD_DIM, HEAD_DIM))
    return y, out


def kernel(x_prompt, x_sample, cache_sb_k, cache_sb_v, cache_cmp_k, cache_cmp_v, cache_slc_k, cache_slc_v,
           state_win_k, state_win_v, state_ret, page_table, norm1_g, w_in, nsa_q_norm, nsa_k_norm, cmp_pos_k,
           cmp_pos_v, cmp_w_k, cmp_w_v, w_out, norm2_g, w_up, w_down):
    b, t, d = x_prompt.shape
    depth = w_in.shape[0]
    tm = min(512, t)
    pos_p = jnp.arange(t, dtype=I32)
    tabs_p = (*_rope_tables(pos_p, HEAD_DIM, RET_THETA, D_RET), *_rope_tables(pos_p, ROT_DIM, ROPE_THETA, LANES))
    xp = x_prompt.reshape(b * t, d)

    nb, n_new, _ = x_sample.shape
    assert n_new == 1, "sample group kernels handle one new token per sample"
    pg = cache_sb_k.shape[2]
    past_len = page_table.shape[1] * pg
    wb = state_win_k.shape[2]
    assert wb <= WINDOW and wb <= past_len
    pos_s = jnp.full((nb,), past_len, dtype=I32)
    tabs_s = (*_rope_tables(pos_s, HEAD_DIM, RET_THETA, D_RET), *_rope_tables(pos_s, ROT_DIM, ROPE_THETA, LANES))
    xs = x_sample.reshape(nb, d)
    caches = tuple(_pages_t(c) for c in (cache_sb_k, cache_sb_v, cache_cmp_k, cache_cmp_v, cache_slc_k, cache_slc_v))
    states = (_pages_t(state_win_k), _pages_t(state_win_v),
              state_ret.reshape(depth, nb, RET_HEADS * HEAD_DIM, HEAD_DIM))

    new = {}
    for l in range(depth):
        prm = _layer_params(l, norm1_g, w_in, nsa_q_norm, nsa_k_norm, cmp_pos_k, cmp_pos_v, cmp_w_k, cmp_w_v,
                            w_out, norm2_g, w_up, w_down)
        xp, p_new = _prompt_layer(xp, prm, tabs_p, b=b, t=t, tm=tm)
        xs, s_new = _sample_layer(l, xs, prm, tabs_s, caches, states, page_table, past_len=past_len)
        for name, val in {**p_new, **s_new}.items():
            new.setdefault(name, []).append(val)
    st = lambda name: jnp.stack(new[name])
    return (xp.reshape(b, t, d), xs.reshape(nb, 1, d)) + tuple(st(nm) for nm in (
        'p_sb_k', 'p_sb_v', 'p_cmp_k', 'p_cmp_v', 'p_slc_k', 'p_slc_v', 'p_win_k', 'p_win_v', 'p_ret',
        's_sb_k', 's_sb_v', 's_cmp_k', 's_cmp_v', 's_slc_k', 's_slc_v', 's_win_k', 's_win_v', 's_ret'))
```

```python
import functools
import math

import numpy as np
import jax
import jax.numpy as jnp
from jax import lax
from jax.experimental import pallas as pl
from jax.experimental.pallas import tpu as pltpu

HEAD_DIM = 64
SB_HEADS = 4
RET_HEADS = 4
NSA_HEADS = 8
NSA_KV = 2
NSA_REP = NSA_HEADS // NSA_KV
D_SB = SB_HEADS * HEAD_DIM
D_RET = RET_HEADS * HEAD_DIM
D_NSA = NSA_HEADS * HEAD_DIM
D_KV = NSA_KV * HEAD_DIM
ROPE_THETA = 500000.0
ROT_DIM = HEAD_DIM // 4
RET_THETA = 10000.0
BLK_CMP = 32
BLK_SEL = 64
TOP_N = 16
WINDOW = 512
FORCE_SCORE = 1.0e4
NEG = -1.0e30
EPS = 1e-6
QK_SCALE = HEAD_DIM ** -0.5

LANES = 128
VMEM_LIMIT = 56 * 1024 * 1024

BF = jnp.bfloat16
F32 = jnp.float32
I32 = jnp.int32

_C_SBQ, _C_SBK, _C_SBV = 0, 256, 512
_C_RQ, _C_RK, _C_RV, _C_RG = 768, 1024, 1280, 1536
_C_NQ = 1792
_C_CK, _C_CV, _C_SK, _C_SV, _C_WK, _C_WV = 2304, 2432, 2560, 2688, 2816, 2944
_C_NG = 3072
_N_IN = 3096
_N_IN_PAD = 3200


def _dot(a, b):
    return jnp.dot(a, b, preferred_element_type=F32)


def _dot_nt(a, b):
    return lax.dot_general(a, b, (((1,), (1,)), ((), ())), preferred_element_type=F32)


def _dot_tn(a, b):
    return lax.dot_general(a, b, (((0,), (0,)), ((), ())), preferred_element_type=F32)


def _split2(x):
    hi = x.astype(BF)
    lo = (x - hi.astype(F32)).astype(BF)
    return hi, lo


def _split3(x):
    hi = x.astype(BF)
    r = x - hi.astype(F32)
    mid = r.astype(BF)
    lo = (r - mid.astype(F32)).astype(BF)
    return hi, mid, lo


def _dot_x2(x, w):
    hi, lo = _split2(x)
    return _dot(hi, w) + _dot(lo, w)


def _seg_mean_mat(n):
    r = lax.broadcasted_iota(I32, (n, n), 0) // HEAD_DIM
    c = lax.broadcasted_iota(I32, (n, n), 1) // HEAD_DIM
    return jnp.where(r == c, 1.0 / HEAD_DIM, 0.0).astype(BF)


def _cparams(sem, vmem=VMEM_LIMIT):
    return pltpu.CompilerParams(dimension_semantics=sem, vmem_limit_bytes=vmem)


def _const_spec(shape):
    nd = len(shape)
    return pl.BlockSpec(shape, lambda *a: (0,) * nd)


def _proj_kernel(x_ref, g1_ref, wm_ref, cr_ref, sr_ref, cn_ref, sn_ref, gn_ref, gs_ref,
                 sbk_f, sbv_f, ck_f, cv_f, sk_f, sv_f, wk_f, wv_f,
                 sbq_b, sbk_b, sbv_b, rq_b, rk_b, rv_b, rg_f, nq_b, sk_b, sv_b, wk_b, wv_b, ng_f, *, t_out):
    x = x_ref[...]
    ms = jnp.mean(x * x, axis=-1, keepdims=True)
    xn = (x * lax.rsqrt(ms + EPS) * g1_ref[...]).astype(BF)

    def put(ref, val):
        ref[...] = jnp.transpose(val) if t_out else val

    def mm(w_ref, lo, n):
        return _dot(xn, w_ref[:, lo:lo + n])

    sbq_b[...] = (mm(wm_ref, _C_SBQ, D_SB) * QK_SCALE).astype(BF)
    k = mm(wm_ref, _C_SBK, D_SB)
    put(sbk_f, k)
    sbk_b[...] = k.astype(BF)
    v = mm(wm_ref, _C_SBV, D_SB)
    put(sbv_f, v)
    sbv_b[...] = v.astype(BF)

    cr = cr_ref[...]
    sr = sr_ref[...]
    def swap_halves(y, half):
        d = lax.broadcasted_iota(I32, (y.shape[0], LANES), 1) % HEAD_DIM
        cols = [y[:, c:c + LANES] for c in range(0, y.shape[1], LANES)]
        cols = [jnp.where(d < half, pltpu.roll(c, LANES - half, 1), pltpu.roll(c, half, 1)) for c in cols]
        return cols[0] if len(cols) == 1 else jnp.concatenate(cols, axis=1)

    y = mm(wm_ref, _C_RQ, D_RET)
    rq_b[...] = (y * cr + swap_halves(y, HEAD_DIM // 2) * sr).astype(BF)
    y = mm(wm_ref, _C_RK, D_RET)
    rk_b[...] = ((y * cr + swap_halves(y, HEAD_DIM // 2) * sr) * QK_SCALE).astype(BF)
    rv_b[...] = mm(wm_ref, _C_RV, D_RET).astype(BF)
    rg_f[...] = mm(wm_ref, _C_RG, D_RET)

    seg = _seg_mean_mat(LANES)
    cn = cn_ref[...]
    sn = sn_ref[...]

    def normrope(cm, gi):
        y = mm(wm_ref, cm, LANES)
        ysw = swap_halves(y, ROT_DIM // 2)
        r = lax.rsqrt(_dot_x2(y * y, seg) + EPS)
        g = gn_ref[gi:gi + 1, :]
        gsw = gs_ref[gi:gi + 1, :]
        return r * (y * (g * cn) + ysw * (gsw * sn))

    for c in range(D_NSA // LANES):
        nq_b[:, c * LANES:(c + 1) * LANES] = (
            normrope(_C_NQ + c * LANES, 0) * QK_SCALE).astype(BF)
    put(ck_f, normrope(_C_CK, 1))
    put(cv_f, mm(wm_ref, _C_CV, D_KV))
    k = normrope(_C_SK, 2)
    put(sk_f, k)
    sk_b[...] = k.astype(BF)
    v = mm(wm_ref, _C_SV, D_KV)
    put(sv_f, v)
    sv_b[...] = v.astype(BF)
    k = normrope(_C_WK, 3)
    put(wk_f, k)
    wk_b[...] = k.astype(BF)
    v = mm(wm_ref, _C_WV, D_KV)
    put(wv_f, v)
    wv_b[...] = v.astype(BF)
    ng_f[...] = jax.nn.sigmoid(mm(wm_ref, _C_NG, LANES))


def _project(x2d, g1, wm, tabs, gn, gs, *, tm, tab_blocks, seqs=None):
    n, d = x2d.shape
    cr, sr, cn, sn = tabs
    grid = (n // tm,)
    tok = lambda w: pl.BlockSpec((tm, w), lambda i: (i, 0))
    tab = lambda w: pl.BlockSpec((tm, w), lambda i: (i % tab_blocks, 0))
    f32 = lambda w: jax.ShapeDtypeStruct((n, w), F32)
    bf = lambda w: jax.ShapeDtypeStruct((n, w), BF)
    out_w_f = [D_SB, D_SB] + [D_KV] * 6
    if seqs is None:
        cache_shapes = [f32(w) for w in out_w_f]
        cache_specs = [tok(w) for w in out_w_f]
    else:
        t = n // seqs
        per = t // tm
        cache_shapes = [jax.ShapeDtypeStruct((seqs, w, t), F32) for w in out_w_f]
        cache_specs = [pl.BlockSpec((None, w, tm), lambda i: (i // per, 0, i % per)) for w in out_w_f]
    out_shape = (cache_shapes
                 + [bf(D_SB)] * 3 + [bf(D_RET)] * 3 + [f32(D_RET), bf(D_NSA)] + [bf(D_KV)] * 4 + [f32(LANES)])
    out_w = [D_SB] * 3 + [D_RET] * 3 + [D_RET, D_NSA] + [D_KV] * 4 + [LANES]
    return pl.pallas_call(
        functools.partial(_proj_kernel, t_out=seqs is not None),
        out_shape=out_shape,
        grid=grid,
        in_specs=[tok(d), _const_spec((1, d)), _const_spec(wm.shape),
                  tab(D_RET), tab(D_RET), tab(LANES), tab(LANES),
                  _const_spec(gn.shape), _const_spec(gs.shape)],
        out_specs=cache_specs + [tok(w) for w in out_w],
        compiler_params=_cparams(("parallel",)),
        name="proj",
    )(x2d, g1, wm, cr, sr, cn, sn, gn, gs)


SB_DEAD = -120.0


def _softplus(z):
    return jnp.maximum(z, 0.0) + jnp.log(1.0 + jnp.exp(-jnp.abs(z)))


def _sb_kernel(q_ref, k_ref, v_ref, o_ref, c_ref, acc_ref, *, tq, tk):
    i = pl.program_id(1)
    m = SB_HEADS * tq
    q = q_ref[...]
    lane_h = lax.broadcasted_iota(I32, (tq, D_SB), 1) // HEAD_DIM
    qs = jnp.concatenate([jnp.where(lane_h == h, q, jnp.zeros_like(q)) for h in range(SB_HEADS)], axis=0)
    q_pos = i * tq + lax.broadcasted_iota(I32, (m, tk), 0) % tq
    col = lax.broadcasted_iota(I32, (m, tk), 1)
    tri = (lax.broadcasted_iota(I32, (tk, tk), 0) > lax.broadcasted_iota(I32, (tk, tk), 1)).astype(BF)
    nt = (i * tq) // tk + 1

    c_ref[...] = jnp.zeros_like(c_ref)
    acc_ref[...] = jnp.zeros_like(acc_ref)

    def tile(j, masked):
        off = pl.multiple_of(j * tk, tk)
        z = _dot_nt(qs, k_ref[pl.ds(off, tk), :])
        l1m = -_softplus(z)
        if masked:
            mask = (off + col) < q_pos
            l1m = jnp.where(mask, l1m, 0.0)
        c = c_ref[...]
        after = c + _dot_x2(l1m, tri)
        lw = z + l1m + after
        if masked:
            lw = jnp.where(mask, lw, NEG)
        w = jnp.exp(lw)
        acc_ref[...] += _dot(w.astype(BF), v_ref[pl.ds(off, tk), :])
        c_ref[...] = c + jnp.sum(l1m, axis=1, keepdims=True)

    tile(nt - 1, True)

    def alive():
        return jnp.max(c_ref[...]) > SB_DEAD

    def cond(state):
        return (state[0] < nt - 1) & state[1]

    def body(state):
        tile(nt - 2 - state[0], False)
        return state[0] + 1, alive()

    lax.while_loop(cond, body, (jnp.int32(0), alive()))

    acc = acc_ref[...]
    out = jnp.zeros((tq, D_SB), F32)
    for h in range(SB_HEADS):
        out = out + jnp.where(lane_h == h, acc[h * tq:(h + 1) * tq], 0.0)
    o_ref[...] = out.astype(BF)


def _sb_prompt(q, k, v, *, b, t, tq=256, tk=256):
    n = b * t
    nq = t // tq
    return pl.pallas_call(
        functools.partial(_sb_kernel, tq=tq, tk=tk),
        out_shape=jax.ShapeDtypeStruct((n, D_SB), BF),
        grid=(b, nq),
        in_specs=[pl.BlockSpec((tq, D_SB), lambda bi, i: (bi * nq + i, 0)),
                  pl.BlockSpec((t, D_SB), lambda bi, i: (bi, 0)),
                  pl.BlockSpec((t, D_SB), lambda bi, i: (bi, 0))],
        out_specs=pl.BlockSpec((tq, D_SB), lambda bi, i: (bi * nq + i, 0)),
        scratch_shapes=[pltpu.VMEM((SB_HEADS * tq, 1), F32), pltpu.VMEM((SB_HEADS * tq, D_SB), F32)],
        compiler_params=_cparams(("parallel", "parallel")),
        name="sb_prompt",
    )(q, k, v)


def _ret_kernel(q_ref, k_ref, v_ref, g_ref, dec_ref, qd_ref, kd_ref, gc_ref, o_ref, st_ref, s_scr, *, c):
    ci = pl.program_id(1)

    @pl.when(ci == 0)
    def _():
        s_scr[...] = jnp.zeros_like(s_scr)

    q = q_ref[...]
    k = k_ref[...]
    v = v_ref[...]
    lane_h = lax.broadcasted_iota(I32, (c, D_RET), 1) // HEAD_DIM
    o = jnp.zeros((c, D_RET), F32)
    for h in range(RET_HEADS):
        kh = jnp.where(lane_h == h, k, jnp.zeros_like(k))
        vh = jnp.where(lane_h == h, v, jnp.zeros_like(v))
        s = _dot_nt(q, kh) * dec_ref[h]
        o = o + _dot(s.astype(BF), vh)
    st = s_scr[...]
    shi, slo = _split2(st)
    o = o + (_dot(q, shi) + _dot(q, slo)) * qd_ref[...]

    kd = (k.astype(F32) * kd_ref[...]).astype(BF)
    ktv = _dot_tn(kd, v)
    r = lax.broadcasted_iota(I32, (D_RET, D_RET), 0) // HEAD_DIM
    cc = lax.broadcasted_iota(I32, (D_RET, D_RET), 1) // HEAD_DIM
    new_st = st * gc_ref[...] + jnp.where(r == cc, ktv, 0.0)
    s_scr[...] = new_st
    st_ref[...] = new_st

    seg = _seg_mean_mat(D_RET)
    mu = _dot_x2(o, seg)
    d = o - mu
    var = _dot_x2(d * d, seg)
    y = d * lax.rsqrt(var + EPS)
    o_ref[...] = (y * jax.nn.silu(g_ref[...])).astype(BF)


def _ret_tables(c):
    log_gamma = np.log1p(-np.exp2(-5.0 - np.arange(RET_HEADS, dtype=np.float64)))
    idx = np.arange(c, dtype=np.float64)
    diff = idx[:, None] - idx[None, :]
    dec = np.where(diff >= 0, np.exp(np.maximum(diff, 0.0)[None] * log_gamma[:, None, None]), 0.0)
    lane_lg = np.repeat(log_gamma, HEAD_DIM)
    qd = np.exp((idx[:, None] + 1.0) * lane_lg[None, :])
    kd = np.exp((c - 1.0 - idx)[:, None] * lane_lg[None, :])
    gc = np.exp(c * lane_lg)[None, :]
    f = lambda a: jnp.asarray(a, F32)
    return f(dec), f(qd), f(kd), f(gc)


def _ret_prompt(q, k, v, g, *, b, t, c=256):
    n = b * t
    nc = t // c
    dec, qd, kd, gc = _ret_tables(c)
    tok = pl.BlockSpec((c, D_RET), lambda bi, i: (bi * nc + i, 0))
    o, st = pl.pallas_call(
        functools.partial(_ret_kernel, c=c),
        out_shape=[jax.ShapeDtypeStruct((n, D_RET), BF), jax.ShapeDtypeStruct((b, D_RET, D_RET), F32)],
        grid=(b, nc),
        in_specs=[tok, tok, tok, tok, _const_spec(dec.shape), _const_spec(qd.shape), _const_spec(kd.shape),
                  _const_spec(gc.shape)],
        out_specs=[tok, pl.BlockSpec((None, D_RET, D_RET), lambda bi, i: (bi, 0, 0))],
        scratch_shapes=[pltpu.VMEM((D_RET, D_RET), F32)],
        compiler_params=_cparams(("parallel", "arbitrary")),
        name="ret_prompt",
    )(q, k, v, g, dec, qd, kd, gc)
    return o, st


CMP_SPAN = 2048


def _compress_t(xt, post, phi_hi, phi_lo):
    n = xt.shape[1]
    sel = (lax.broadcasted_iota(I32, (n, LANES), 0) // BLK_CMP
           == lax.broadcasted_iota(I32, (n, LANES), 1)).astype(BF)
    xs_t = _dot_x2(xt * post, sel)
    hi, lo = _split2(jnp.transpose(xs_t))
    out = _dot(hi, phi_hi) + _dot(lo, phi_hi) + _dot(hi, phi_lo)
    return out[0:n // BLK_CMP]


def _compress_kernel(k_ref, v_ref, pk_ref, pv_ref, wk_ref, wv_ref, ko_ref, vo_ref):
    wk_hi, wk_lo = _split2(wk_ref[...])
    wv_hi, wv_lo = _split2(wv_ref[...])
    ko_ref[...] = _compress_t(k_ref[...], pk_ref[...], wk_hi, wk_lo).astype(BF)
    vo_ref[...] = _compress_t(v_ref[...], pv_ref[...], wv_hi, wv_lo).astype(BF)


def _compress_prompt(ckt, cvt, pkt, pvt, wk, wv):
    b, _, t = ckt.shape
    steps = t // CMP_SPAN
    tok = pl.BlockSpec((None, LANES, CMP_SPAN), lambda bi, i: (bi, 0, i))
    out = pl.BlockSpec((CMP_SPAN // BLK_CMP, LANES), lambda bi, i: (bi * steps + i, 0))
    return pl.pallas_call(
        _compress_kernel,
        out_shape=[jax.ShapeDtypeStruct((b * t // BLK_CMP, LANES), BF)] * 2,
        grid=(b, steps),
        in_specs=[tok, tok, _const_spec(pkt.shape), _const_spec(pvt.shape), _const_spec(wk.shape),
                  _const_spec(wv.shape)],
        out_specs=[out, out],
        compiler_params=_cparams(("parallel", "parallel")),
        name="compress_prompt",
    )(ckt, cvt, pkt, pvt, wk, wv)


def _stack_heads_q(q, extra):
    tq = q.shape[0]
    lane = lax.broadcasted_iota(I32, (tq, LANES), 1)
    rows = []
    for h in range(NSA_HEADS):
        g = h // NSA_REP
        blk = q[:, (h // 2) * LANES:(h // 2 + 1) * LANES]
        src_half = h % 2
        if src_half != g:
            blk32 = pltpu.roll(blk.astype(F32), HEAD_DIM, 1).astype(BF)
        else:
            blk32 = blk
        keep = (lane // HEAD_DIM) == g
        rows.append(jnp.where(keep, blk32, jnp.zeros_like(blk32)))
    qs = jnp.concatenate(rows, axis=0)
    if extra is not None:
        qs = jnp.concatenate([qs, extra], axis=1)
    return qs


def _unstack_heads(acc, tq):
    lane = lax.broadcasted_iota(I32, (tq, LANES), 1)
    cols = []
    for c in range(NSA_HEADS // 2):
        g = (2 * c) // NSA_REP
        a = acc[(2 * c) * tq:(2 * c + 1) * tq]
        b = acc[(2 * c + 1) * tq:(2 * c + 2) * tq]
        if g == 0:
            cols.append(jnp.where(lane < HEAD_DIM, a, pltpu.roll(b, HEAD_DIM, 1)))
        else:
            cols.append(jnp.where(lane < HEAD_DIM, pltpu.roll(a, HEAD_DIM, 1), b))
    return jnp.concatenate(cols, axis=1)


def _order_key(v):
    return v


def _beats(key_j, key, j_before):
    return jnp.where(j_before, (key_j >= key).astype(I32), (key_j > key).astype(I32))


def _gate_expand(gates, branch):
    r = lax.broadcasted_iota(I32, (LANES, D_NSA), 0)
    c = lax.broadcasted_iota(I32, (LANES, D_NSA), 1)
    e = (r == (c // HEAD_DIM) * 3 + branch).astype(BF)
    return _dot_x2(gates, e)


def _nsa_kernel(q_ref, gate_ref, kc_ref, vc_ref, sk_ref, sv_ref, wk_ref, wv_ref, o_ref,
                k2_scr, svt_scr, wvt_scr, imp_scr, m_scr, l_scr, acc_scr, sa_scr, sb_scr, *, tq, tk, tw, t):
    i = pl.program_id(1)
    nc = t // BLK_CMP
    ns = t // BLK_SEL
    m = NSA_HEADS * tq

    @pl.when(i == 0)
    def _():
        rows = 512
        for r0 in range(0, t, rows):
            s_idx = r0 + lax.broadcasted_iota(I32, (rows, LANES), 0)
            c_idx = lax.broadcasted_iota(I32, (rows, LANES), 1)
            e = (s_idx // BLK_SEL == c_idx).astype(BF)
            k2_scr[r0:r0 + rows, :] = jnp.concatenate([sk_ref[r0:r0 + rows, :], e], axis=1)
        for c in range(t // tk):
            svt_scr[c] = jnp.transpose(sv_ref[c * tk:(c + 1) * tk, :].astype(F32)).astype(BF)
        for c in range(t // tw):
            wvt_scr[c] = jnp.transpose(wv_ref[c * tw:(c + 1) * tw, :].astype(F32)).astype(BF)

    q = q_ref[...]
    qs = _stack_heads_q(q, None)

    kc = kc_ref[...]
    vct = jnp.transpose(vc_ref[...].astype(F32)).astype(BF)
    n_idx = lax.broadcasted_iota(I32, (nc, m), 0)
    t_idx = i * tq + lax.broadcasted_iota(I32, (nc, m), 1) % tq
    valid = (n_idx + 1) * BLK_CMP - 1 <= t_idx
    st = jnp.where(valid, _dot_nt(kc, qs), NEG)
    mx = jnp.max(st, axis=0, keepdims=True)
    e = jnp.where(valid, jnp.exp(st - mx), 0.0)
    den = jnp.sum(e, axis=0, keepdims=True)
    p = e / jnp.where(den > 0.0, den, 1.0)
    o_cmp = _dot(vct, p.astype(BF))
    imp = []
    for g in range(NSA_KV):
        acc_g = p[:, g * NSA_REP * tq:(g * NSA_REP + 1) * tq]
        for h in range(g * NSA_REP + 1, (g + 1) * NSA_REP):
            acc_g = acc_g + p[:, h * tq:(h + 1) * tq]
        imp.append(acc_g)

    pair = (lax.broadcasted_iota(I32, (ns, nc), 1) // (BLK_SEL // BLK_CMP)
            == lax.broadcasted_iota(I32, (ns, nc), 0)).astype(BF)
    blk = lax.broadcasted_iota(I32, (ns, tq), 0)
    tb = i * tq + lax.broadcasted_iota(I32, (ns, tq), 1)
    forced = (blk == 0) | (blk == tb // BLK_SEL)
    future = blk * BLK_SEL > tb
    eye = (lax.broadcasted_iota(I32, (tq, tq), 0) == lax.broadcasted_iota(I32, (tq, tq), 1)).astype(BF)
    pens = []
    for g in range(NSA_KV):
        a, b, c = _split3(imp[g])
        v = _dot(pair, a) + _dot(pair, b) + _dot(pair, c)
        v = jnp.where(forced, FORCE_SCORE, jnp.where(future, -1.0, v))
        key = _order_key(v)
        imp_scr[g] = key

        def body(j, cnt, g=g, key=key):
            return cnt + _beats(imp_scr[g, pl.ds(j, 1), :], key, blk > j)

        cnt = lax.fori_loop(0, ns, body, jnp.zeros((ns, tq), I32), unroll=8)
        sel_t = (cnt < min(TOP_N, ns)).astype(BF)
        pen = _dot_nt(eye, sel_t)
        pen = ((pen - 1.0) * (-NEG)).astype(BF)
        if ns < LANES:
            pen = jnp.concatenate([pen, jnp.zeros((tq, LANES - ns), BF)], axis=1)
        pens.append(pen)
    pen_rows = jnp.concatenate([pens[h // NSA_REP] for h in range(NSA_HEADS)], axis=0)
    qs2 = jnp.concatenate([qs, pen_rows], axis=1)

    q_pos = i * tq + lax.broadcasted_iota(I32, (1, m), 1) % tq

    def online(scores, vals_t, mask):
        if mask is not None:
            scores = jnp.where(mask, scores, NEG)
        m_old = m_scr[...]
        m_new = jnp.maximum(m_old, jnp.max(scores, axis=0, keepdims=True))
        alpha = jnp.exp(m_old - m_new)
        p = jnp.exp(scores - m_new)
        l_scr[...] = alpha * l_scr[...] + jnp.sum(p, axis=0, keepdims=True)
        acc_scr[...] = alpha * acc_scr[...] + _dot(vals_t, p.astype(BF))
        m_scr[...] = m_new

    def reset():
        m_scr[...] = jnp.full_like(m_scr, NEG)
        l_scr[...] = jnp.zeros_like(l_scr)
        acc_scr[...] = jnp.zeros_like(acc_scr)

    def result():
        return acc_scr[...] / l_scr[...]

    reset()
    nt = (i * tq) // tk + 1
    row_k = lax.broadcasted_iota(I32, (tk, 1), 0)

    def scores_into(buf, j):
        off = pl.multiple_of(j * tk, tk)
        buf[...] = _dot_nt(k2_scr[pl.ds(off, tk), :], qs2)

    def diag_mask():
        return ((nt - 1) * tk + row_k) <= q_pos

    n_pairs = (nt - 1) // 2
    scores_into(sa_scr, 0)

    def slc_body(pr, carry):
        scores_into(sb_scr, 2 * pr + 1)
        online(sa_scr[...], svt_scr[2 * pr], None)
        scores_into(sa_scr, 2 * pr + 2)
        online(sb_scr[...], svt_scr[2 * pr + 1], None)
        return carry

    lax.fori_loop(0, n_pairs, slc_body, 0)

    @pl.when((nt - 1) % 2 == 1)
    def _():
        scores_into(sb_scr, nt - 1)
        online(sa_scr[...], svt_scr[nt - 2], None)
        online(sb_scr[...], svt_scr[nt - 1], diag_mask())

    @pl.when((nt - 1) % 2 == 0)
    def _():
        online(sa_scr[...], svt_scr[nt - 1], diag_mask())

    o_slc = result()

    row_w = lax.broadcasted_iota(I32, (tw, 1), 0)
    n_band = (WINDOW + tq + tw - 1) // tw
    first = jnp.maximum(i * tq - WINDOW, 0) // tw
    s_w = []
    for c in range(n_band):
        off = pl.multiple_of((first + c) * tw, tw)
        k_pos = off + row_w
        s = _dot_nt(wk_ref[pl.ds(off, tw), :], qs)
        s_w.append(jnp.where((k_pos <= q_pos) & (q_pos - k_pos <= WINDOW), s, NEG))
    m_w = s_w[0].max(axis=0, keepdims=True)
    for s in s_w[1:]:
        m_w = jnp.maximum(m_w, s.max(axis=0, keepdims=True))
    l_w = jnp.zeros((1, m), F32)
    o_win = jnp.zeros((LANES, m), F32)
    for c, s in enumerate(s_w):
        p = jnp.exp(s - m_w)
        l_w = l_w + p.sum(axis=0, keepdims=True)
        o_win = o_win + _dot(wvt_scr[first + c], p.astype(BF))
    o_win = o_win / l_w

    gt = jnp.transpose(gate_ref[...])

    def gate_row(branch):
        return jnp.concatenate([gt[h * 3 + branch:h * 3 + branch + 1, :] for h in range(NSA_HEADS)], axis=1)

    mix = gate_row(0) * o_cmp + gate_row(1) * o_slc + gate_row(2) * o_win
    stacked = jnp.concatenate([jnp.transpose(mix[:, h * tq:(h + 1) * tq]) for h in range(NSA_HEADS)], axis=0)
    o_ref[...] = _unstack_heads(stacked, tq).astype(BF)


def _nsa_prompt(nq, gates, kc, vc, sk, sv, wk, wv, *, b, t, tq=256, tk=512, tw=128):
    assert t % tk == 0 and (WINDOW + tq + tw - 1) // tw <= t // tw
    n = b * t
    nqb = t // tq
    nc = t // BLK_CMP
    ns = t // BLK_SEL
    m = NSA_HEADS * tq
    tok = lambda w: pl.BlockSpec((tq, w), lambda bi, i: (bi * nqb + i, 0))
    seq = lambda rows, w: pl.BlockSpec((rows, w), lambda bi, i: (bi, 0))
    return pl.pallas_call(
        functools.partial(_nsa_kernel, tq=tq, tk=tk, tw=tw, t=t),
        out_shape=jax.ShapeDtypeStruct((n, D_NSA), BF),
        grid=(b, nqb),
        in_specs=[tok(D_NSA), tok(LANES), seq(nc, LANES), seq(nc, LANES),
                  seq(t, LANES), seq(t, LANES), seq(t, LANES), seq(t, LANES)],
        out_specs=tok(D_NSA),
        scratch_shapes=[pltpu.VMEM((t, 2 * LANES), BF), pltpu.VMEM((t // tk, LANES, tk), BF),
                        pltpu.VMEM((t // tw, LANES, tw), BF), pltpu.VMEM((NSA_KV, ns, tq), F32),
                        pltpu.VMEM((1, m), F32), pltpu.VMEM((1, m), F32), pltpu.VMEM((LANES, m), F32),
                        pltpu.VMEM((tk, m), F32), pltpu.VMEM((tk, m), F32)],
        compiler_params=_cparams(("parallel", "arbitrary")),
        name="nsa_prompt",
    )(nq, gates, kc, vc, sk, sv, wk, wv)


def _out_mlp_kernel(x_ref, osb_ref, oret_ref, onsa_ref, wo_ref, g2_ref, wu_ref, wd_ref, y_ref, *, ff_chunk):
    h = (x_ref[...] + _dot(osb_ref[...].astype(BF), wo_ref[0:D_SB, :])
         + _dot(oret_ref[...].astype(BF), wo_ref[D_SB:D_SB + D_RET, :])
         + _dot(onsa_ref[...].astype(BF), wo_ref[D_SB + D_RET:, :]))
    ms = jnp.mean(h * h, axis=-1, keepdims=True)
    hn = (h * lax.rsqrt(ms + EPS) * g2_ref[...]).astype(BF)
    mlp = None
    d_ff = wu_ref.shape[1]
    for c0 in range(0, d_ff, ff_chunk):
        u = jnp.maximum(_dot(hn, wu_ref[:, c0:c0 + ff_chunk]), 0.0)
        part = _dot((u * u).astype(BF), wd_ref[c0:c0 + ff_chunk, :])
        mlp = part if mlp is None else mlp + part
    y_ref[...] = h + mlp


def _out_mlp(x2d, osb, oret, onsa, wo, g2, wu, wd, *, tm, ff_chunk=1024):
    n, d = x2d.shape
    tok = lambda w: pl.BlockSpec((tm, w), lambda i: (i, 0))
    return pl.pallas_call(
        functools.partial(_out_mlp_kernel, ff_chunk=ff_chunk),
        out_shape=jax.ShapeDtypeStruct((n, d), F32),
        grid=(n // tm,),
        in_specs=[tok(d), tok(D_SB), tok(D_RET), tok(D_NSA), _const_spec(wo.shape), _const_spec((1, d)),
                  _const_spec(wu.shape), _const_spec(wd.shape)],
        out_specs=tok(d),
        compiler_params=_cparams(("parallel",)),
        name="out_mlp",
    )(x2d, osb, oret, onsa, wo, g2, wu, wd)


SB_PAGES_PER_FETCH = 4
DEC_ROWS = 16


def _pages_t(c):
    dp, npool, pg, h, dd = c.shape
    return jnp.transpose(c, (0, 1, 3, 4, 2)).reshape(dp, npool, h * dd, pg)


def _page_spec(layer, width, page_of_step, pg):
    def imap(bi, p, pt, *rest):
        return (layer, pt[bi, page_of_step(p)], 0, 0)
    return pl.BlockSpec((None, None, width, pg), imap)


def _sb_dec_kernel(pt_ref, q_ref, k_hbm, v_hbm, o_ref, kbuf, vbuf, sem, c_scr, acc_scr, *, layer, npg, npp):
    bi = pl.program_id(0)
    rows = DEC_ROWS
    pg = kbuf.shape[2]
    nchunks = npg // npp

    def chunk_copies(c):
        out = []
        for kk in range(npp):
            page = pt_ref[bi, npg - 1 - (c * npp + kk)]
            out.append(pltpu.make_async_copy(k_hbm.at[layer, page], kbuf.at[kk], sem.at[0, kk]))
            out.append(pltpu.make_async_copy(v_hbm.at[layer, page], vbuf.at[kk], sem.at[1, kk]))
        return out

    c_scr[...] = jnp.zeros_like(c_scr)
    acc_scr[...] = jnp.zeros_like(acc_scr)
    lane_h = lax.broadcasted_iota(I32, (rows, D_SB), 1) // HEAD_DIM
    row = lax.broadcasted_iota(I32, (rows, D_SB), 0)
    q = jnp.broadcast_to(q_ref[...].astype(F32), (rows, D_SB))
    qs = jnp.where(lane_h == row, q, 0.0).astype(BF)
    tri = (lax.broadcasted_iota(I32, (pg, pg), 0) > lax.broadcasted_iota(I32, (pg, pg), 1)).astype(BF)
    head_rows = lax.broadcasted_iota(I32, (rows, 1), 0) < SB_HEADS

    def cond(state):
        return (state[0] < nchunks) & state[1]

    def body(state):
        ch = state[0]
        for cp in chunk_copies(ch):
            cp.start()
        for cp in chunk_copies(ch):
            cp.wait()
        c = c_scr[...]
        acc = acc_scr[...]
        for kk in range(npp):
            z = _dot(qs, kbuf[kk].astype(BF))
            l1m = -_softplus(z)
            after = c + _dot_x2(l1m, tri)
            w = jnp.exp(z + l1m + after)
            acc = acc + _dot_nt(w.astype(BF), vbuf[kk].astype(BF))
            c = c + jnp.sum(l1m, axis=1, keepdims=True)
        c_scr[...] = c
        acc_scr[...] = acc
        return ch + 1, jnp.max(jnp.where(head_rows, c, NEG)) > SB_DEAD

    lax.while_loop(cond, body, (jnp.int32(0), jnp.bool_(True)))
    o_ref[...] = jnp.sum(jnp.where(lane_h == row, acc_scr[...], 0.0), axis=0, keepdims=True)


def _sb_decode(layer, page_table, q, cache_k, cache_v):
    nb, npg = page_table.shape
    npp = math.gcd(SB_PAGES_PER_FETCH, npg)
    pg = cache_k.shape[3]
    row = pl.BlockSpec((None, 1, D_SB), lambda bi, pt: (bi, 0, 0))
    hbm = pl.BlockSpec(memory_space=pl.ANY)
    return pl.pallas_call(
        functools.partial(_sb_dec_kernel, layer=layer, npg=npg, npp=npp),
        out_shape=jax.ShapeDtypeStruct((nb, 1, D_SB), F32),
        grid_spec=pltpu.PrefetchScalarGridSpec(
            num_scalar_prefetch=1, grid=(nb,),
            in_specs=[row, hbm, hbm], out_specs=row,
            scratch_shapes=[pltpu.VMEM((npp, D_SB, pg), F32), pltpu.VMEM((npp, D_SB, pg), F32),
                            pltpu.SemaphoreType.DMA((2, npp)),
                            pltpu.VMEM((DEC_ROWS, 1), F32), pltpu.VMEM((DEC_ROWS, D_SB), F32)]),
        compiler_params=_cparams(("arbitrary",)),
        name="sb_decode",
    )(page_table, q, cache_k, cache_v)


def _cmp_dec_kernel(pt_ref, kn_ref, vn_ref, pk_ref, pv_ref, pkt_ref, pvt_ref, wk_ref, wv_ref, *refs,
                    npp, last, nblk):
    k_refs, v_refs = refs[:npp], refs[npp:2 * npp]
    ko_ref, vo_ref = refs[2 * npp:]
    p = pl.program_id(1)
    wk_hi, wk_lo = _split2(wk_ref[...])
    wv_hi, wv_lo = _split2(wv_ref[...])
    pg = k_refs[0].shape[1]
    per = npp * pg // BLK_CMP
    r0 = pl.multiple_of(p * per, per)
    ko_ref[pl.ds(r0, per), :] = _compress_t(
        jnp.concatenate([r[...] for r in k_refs], axis=1), pkt_ref[...], wk_hi, wk_lo)
    vo_ref[pl.ds(r0, per), :] = _compress_t(
        jnp.concatenate([r[...] for r in v_refs], axis=1), pvt_ref[...], wv_hi, wv_lo)

    @pl.when(p == last)
    def _():
        first = lax.broadcasted_iota(I32, (DEC_ROWS, LANES), 0) == 0
        xk = jnp.where(first, kn_ref[...] * pk_ref[0:1, :], 0.0)
        xv = jnp.where(first, vn_ref[...] * pv_ref[0:1, :], 0.0)
        a, b = _split2(xk)
        ko_ref[nblk:nblk + DEC_ROWS, :] = _dot(a, wk_hi) + _dot(b, wk_hi) + _dot(a, wk_lo)
        a, b = _split2(xv)
        vo_ref[nblk:nblk + DEC_ROWS, :] = _dot(a, wv_hi) + _dot(b, wv_hi) + _dot(a, wv_lo)


def _cmp_decode(layer, page_table, ck_new, cv_new, cache_k, cache_v, pk, pv, pkt, pvt, wk, wv):
    nb, npg = page_table.shape
    pg = cache_k.shape[3]
    npp = CMP_SPAN // pg
    assert npg % npp == 0
    steps = npg // npp
    nblk = npg * pg // BLK_CMP
    specs = [_page_spec(layer, D_KV, (lambda p, kk=kk: p * npp + kk), pg) for kk in range(npp)]
    row = pl.BlockSpec((None, 1, D_KV), lambda bi, p, pt: (bi, 0, 0))
    cst = lambda a: pl.BlockSpec(a.shape, lambda bi, p, pt: (0,) * a.ndim)
    out = pl.BlockSpec((None, nblk + DEC_ROWS, LANES), lambda bi, p, pt: (bi, 0, 0))
    return pl.pallas_call(
        functools.partial(_cmp_dec_kernel, npp=npp, last=steps - 1, nblk=nblk),
        out_shape=[jax.ShapeDtypeStruct((nb, nblk + DEC_ROWS, LANES), F32)] * 2,
        grid_spec=pltpu.PrefetchScalarGridSpec(
            num_scalar_prefetch=1, grid=(nb, steps),
            in_specs=[row, row, cst(pk), cst(pv), cst(pkt), cst(pvt), cst(wk), cst(wv)] + specs + specs,
            out_specs=[out, out]),
        compiler_params=_cparams(("parallel", "arbitrary")),
        name="cmp_decode",
    )(page_table, ck_new, cv_new, pk, pv, pkt, pvt, wk, wv, *([cache_k] * npp), *([cache_v] * npp))


def _stack_heads_q_row(q):
    qf = q.astype(F32)
    row = lax.broadcasted_iota(I32, (DEC_ROWS, LANES), 0)
    lane_g = lax.broadcasted_iota(I32, (DEC_ROWS, LANES), 1) // HEAD_DIM
    out = jnp.zeros((DEC_ROWS, LANES), F32)
    for h in range(NSA_HEADS):
        g = h // NSA_REP
        blk = jnp.broadcast_to(qf[:, (h // 2) * LANES:(h // 2 + 1) * LANES], (DEC_ROWS, LANES))
        if h % 2 != g:
            blk = pltpu.roll(blk, HEAD_DIM, 1)
        out = jnp.where((row == h) & (lane_g == g), blk, out)
    return out.astype(BF)


def _unstack_heads_row(acc):
    lane = lax.broadcasted_iota(I32, (1, LANES), 1)
    accr = pltpu.roll(acc, HEAD_DIM, 1)
    cols = []
    for c in range(NSA_HEADS // 2):
        g = (2 * c) // NSA_REP
        if g == 0:
            cols.append(jnp.where(lane < HEAD_DIM, acc[2 * c:2 * c + 1], accr[2 * c + 1:2 * c + 2]))
        else:
            cols.append(jnp.where(lane < HEAD_DIM, accr[2 * c:2 * c + 1], acc[2 * c + 1:2 * c + 2]))
    return jnp.concatenate(cols, axis=1)


def _sel_dec_kernel(q_ref, kc_ref, vc_ref, ocmp_ref, idx_ref, imp_scr, *, q_pos, nblk, ns_pad):
    nrow = kc_ref.shape[0]
    qs = _stack_heads_q_row(q_ref[...])
    qs = jnp.concatenate([qs, jnp.zeros((LANES - DEC_ROWS, LANES), BF)], axis=0)
    kc = kc_ref[...].astype(BF)
    st = _dot_nt(kc, qs)
    n_idx = lax.broadcasted_iota(I32, (nrow, LANES), 0)
    valid = (n_idx + 1) * BLK_CMP - 1 <= q_pos
    st = jnp.where(valid, st, NEG)
    mx = jnp.max(st, axis=0, keepdims=True)
    e = jnp.where(valid, jnp.exp(st - mx), 0.0)
    den = jnp.sum(e, axis=0, keepdims=True)
    p = e / jnp.where(den > 0.0, den, 1.0)
    ocmp_ref[...] = _dot_tn(p.astype(BF), vc_ref[...].astype(BF))[0:NSA_HEADS]

    grp = (lax.broadcasted_iota(I32, (LANES, LANES), 0) // NSA_REP
           == lax.broadcasted_iota(I32, (LANES, LANES), 1)).astype(BF)
    a, b, c = _split3(p)
    impc = _dot(a, grp) + _dot(b, grp) + _dot(c, grp)
    pair = (lax.broadcasted_iota(I32, (ns_pad, nrow), 1) // (BLK_SEL // BLK_CMP)
            == lax.broadcasted_iota(I32, (ns_pad, nrow), 0)).astype(BF)
    a, b, c = _split3(impc)
    v = _dot(pair, a) + _dot(pair, b) + _dot(pair, c)
    ns = (nblk * BLK_CMP + 1 + BLK_SEL - 1) // BLK_SEL
    blk = lax.broadcasted_iota(I32, (ns_pad, LANES), 0)
    forced = (blk == 0) | (blk == q_pos // BLK_SEL)
    future = blk * BLK_SEL > q_pos
    v = jnp.where(forced, FORCE_SCORE, jnp.where(future, -1.0, v))
    v = jnp.where(blk < ns, v, -1.0)
    key = _order_key(v)
    imp_scr[...] = key

    def body(j, cnt):
        return cnt + _beats(imp_scr[pl.ds(j, 1), :], key, blk > j)

    cnt = lax.fori_loop(0, ns_pad, body, jnp.zeros((ns_pad, LANES), I32), unroll=8)
    blk_f = blk.astype(F32)
    rows = [jnp.sum(jnp.where(cnt == r, blk_f, 0.0), axis=0, keepdims=True) for r in range(TOP_N)]
    idx_ref[...] = jnp.concatenate(rows, axis=0).astype(I32)


def _sel_decode(nq, kc, vc, *, q_pos, nblk):
    nb = nq.shape[0]
    nrow = kc.shape[1]
    ns_pad = ((nrow // 2 + 7) // 8) * 8
    return pl.pallas_call(
        functools.partial(_sel_dec_kernel, q_pos=q_pos, nblk=nblk, ns_pad=ns_pad),
        out_shape=[jax.ShapeDtypeStruct((nb, NSA_HEADS, LANES), F32), jax.ShapeDtypeStruct((nb, TOP_N, LANES), I32)],
        grid=(nb,),
        in_specs=[pl.BlockSpec((None, 1, D_NSA), lambda bi: (bi, 0, 0)),
                  pl.BlockSpec((None, nrow, LANES), lambda bi: (bi, 0, 0)),
                  pl.BlockSpec((None, nrow, LANES), lambda bi: (bi, 0, 0))],
        out_specs=[pl.BlockSpec((None, NSA_HEADS, LANES), lambda bi: (bi, 0, 0)),
                   pl.BlockSpec((None, TOP_N, LANES), lambda bi: (bi, 0, 0))],
        scratch_shapes=[pltpu.VMEM((ns_pad, LANES), F32)],
        compiler_params=_cparams(("parallel",)),
        name="sel_decode",
    )(nq, kc, vc)


def _nsa_dec_kernel(pt_ref, idx_ref, q_ref, gate_ref, ocmp_ref, skn_ref, svn_ref, wkn_ref, wvn_ref,
                    wkc_ref, wvc_ref, wk_ref, wv_ref, ck_hbm, cv_hbm, o_ref, wko_ref, wvo_ref, kbuf, vbuf, sem,
                    *, layer, ncache, per):
    nsel = NSA_KV * TOP_N
    bi = pl.program_id(0)
    pg = kbuf.shape[2]

    def page_copies(j):
        blk = jnp.minimum(idx_ref[bi, j], ncache - 1)
        page = pt_ref[bi, blk // per]
        return (pltpu.make_async_copy(ck_hbm.at[layer, page], kbuf.at[j], sem.at[0, j]),
                pltpu.make_async_copy(cv_hbm.at[layer, page], vbuf.at[j], sem.at[1, j]))

    for j in range(nsel):
        for cp in page_copies(j):
            cp.start()

    qs = _stack_heads_q_row(q_ref[...])
    qf = qs.astype(F32)
    row_g = lax.broadcasted_iota(I32, (DEC_ROWS, LANES), 0) // NSA_REP

    def attend(s, vt, s_new, v_new):
        mx = jnp.maximum(s_new, jnp.max(s, axis=1, keepdims=True))
        p_new = jnp.exp(s_new - mx)
        p = jnp.exp(s - mx)
        den = p_new + jnp.sum(p, axis=1, keepdims=True)
        return (p_new * v_new + _dot_nt(p.astype(BF), vt)) / den

    bf_row = lambda r: r[...].astype(BF).astype(F32)
    wk = wk_ref[...]
    wv = wv_ref[...]
    s_wn = jnp.sum(qf * bf_row(wkn_ref), axis=1, keepdims=True)
    o_win = attend(_dot(qs, wk.astype(BF)), wv.astype(BF), s_wn, bf_row(wvn_ref))

    wb = wk.shape[1]
    last = lax.broadcasted_iota(I32, (D_KV, wb), 1) == wb - 1
    wko_ref[...] = jnp.where(last, wkc_ref[...], pltpu.roll(wk, wb - 1, 1))
    wvo_ref[...] = jnp.where(last, wvc_ref[...], pltpu.roll(wv, wb - 1, 1))

    for j in range(nsel):
        for cp in page_copies(j):
            cp.wait()

    s_new = jnp.sum(qf * bf_row(skn_ref), axis=1, keepdims=True)
    nkeys = TOP_N * pg
    col = lax.broadcasted_iota(I32, (DEC_ROWS, nkeys), 1)
    col_slot = col // pg
    col_blk = (col % pg) // BLK_SEL
    o_slc = jnp.zeros((DEC_ROWS, LANES), F32)
    for g in range(NSA_KV):
        pen = jnp.full((DEC_ROWS, nkeys), NEG, F32)
        for r in range(TOP_N):
            blk = idx_ref[bi, g * TOP_N + r]
            want = jnp.where(blk < ncache, blk % per, -1)
            pen = jnp.where((col_slot == r) & (col_blk == want), 0.0, pen)
        kcat = jnp.concatenate([kbuf[g * TOP_N + r] for r in range(TOP_N)], axis=1).astype(BF)
        vcat = jnp.concatenate([vbuf[g * TOP_N + r] for r in range(TOP_N)], axis=1).astype(BF)
        o_g = attend(_dot(qs, kcat) + pen, vcat, s_new, bf_row(svn_ref))
        o_slc = jnp.where(row_g == g, o_g, o_slc)

    gates = jnp.broadcast_to(gate_ref[...], (DEC_ROWS, LANES))
    out = (_gate_expand(gates, 0)[0:1] * _unstack_heads_row(ocmp_ref[...])
           + _gate_expand(gates, 1)[0:1] * _unstack_heads_row(o_slc)
           + _gate_expand(gates, 2)[0:1] * _unstack_heads_row(o_win))
    o_ref[...] = out


def _nsa_decode(layer, page_table, idx, nq, gates, ocmp, sk_new, sv_new, wk_new, wv_new, win_k, win_v,
                cache_k, cache_v):
    nb, npg = page_table.shape
    pg = cache_k.shape[3]
    per = pg // BLK_SEL
    ncache = npg * per
    wb = win_k.shape[3]
    nsel = NSA_KV * TOP_N

    row = lambda w: pl.BlockSpec((None, 1, w), lambda bi, pt, ix: (bi, 0, 0))
    colspec = pl.BlockSpec((None, D_KV, 1), lambda bi, pt, ix: (bi, 0, 0))
    win = pl.BlockSpec((None, None, D_KV, wb), lambda bi, pt, ix: (layer, bi, 0, 0))
    wout = pl.BlockSpec((None, D_KV, wb), lambda bi, pt, ix: (bi, 0, 0))
    hbm = pl.BlockSpec(memory_space=pl.ANY)
    col = lambda a: a.reshape(nb, D_KV, 1)
    return pl.pallas_call(
        functools.partial(_nsa_dec_kernel, layer=layer, ncache=ncache, per=per),
        out_shape=[jax.ShapeDtypeStruct((nb, 1, D_NSA), F32), jax.ShapeDtypeStruct((nb, D_KV, wb), F32),
                   jax.ShapeDtypeStruct((nb, D_KV, wb), F32)],
        grid_spec=pltpu.PrefetchScalarGridSpec(
            num_scalar_prefetch=2, grid=(nb,),
            in_specs=[row(D_NSA), row(LANES), pl.BlockSpec((None, NSA_HEADS, LANES), lambda bi, pt, ix: (bi, 0, 0)),
                      row(D_KV), row(D_KV), row(D_KV), row(D_KV), colspec, colspec, win, win, hbm, hbm],
            out_specs=[row(D_NSA), wout, wout],
            scratch_shapes=[pltpu.VMEM((nsel, D_KV, pg), F32), pltpu.VMEM((nsel, D_KV, pg), F32),
                            pltpu.SemaphoreType.DMA((2, nsel))]),
        compiler_params=_cparams(("arbitrary",)),
        name="nsa_decode",
    )(page_table, idx, nq, gates, ocmp, sk_new, sv_new, wk_new, wv_new, col(wk_new), col(wv_new),
      win_k, win_v, cache_k, cache_v)


def _ret_dec_kernel(q_ref, k_ref, kcol_ref, v_ref, g_ref, gcol_ref, grow_ref, st_ref, o_ref, sto_ref):
    rows = DEC_ROWS
    st = st_ref[...]
    q = jnp.broadcast_to(q_ref[...], (rows, D_RET))
    v = jnp.broadcast_to(v_ref[...], (rows, D_RET))
    lane_h = lax.broadcasted_iota(I32, (rows, D_RET), 1) // HEAD_DIM
    row = lax.broadcasted_iota(I32, (rows, D_RET), 0)
    own = lane_h == row
    qs = jnp.where(own, q, 0.0).astype(BF)
    spread = (lax.broadcasted_iota(I32, (HEAD_DIM, D_RET), 1) % HEAD_DIM
              == lax.broadcasted_iota(I32, (HEAD_DIM, D_RET), 0)).astype(BF)
    shi, slo = _split2(st)
    cross = _dot(qs, shi) + _dot(qs, slo)
    o_cross = jnp.sum(jnp.where(own, _dot_x2(cross, spread), 0.0), axis=0, keepdims=True) * grow_ref[...]
    seg = _seg_mean_mat(D_RET)
    qk = q * jnp.broadcast_to(k_ref[...], (rows, D_RET))
    o = o_cross + (_dot_x2(qk, seg) * float(HEAD_DIM)) * v

    v4 = _dot_nt(jnp.where(own, v, 0.0).astype(BF), spread)
    pick = (lax.broadcasted_iota(I32, (D_RET, rows), 0) // HEAD_DIM
            == lax.broadcasted_iota(I32, (D_RET, rows), 1)).astype(BF)
    vexp = _dot(pick, v4.astype(BF))
    sto_ref[...] = st * gcol_ref[...] + kcol_ref[...] * vexp

    mu = _dot_x2(o, seg)
    d = o - mu
    var = _dot_x2(d * d, seg)
    y = d * lax.rsqrt(var + EPS)
    o_ref[...] = y[0:1] * jax.nn.silu(g_ref[...])


def _ret_decode(layer, q, k, v, g, state):
    nb = q.shape[0]
    log_gamma = np.log1p(-np.exp2(-5.0 - np.arange(RET_HEADS, dtype=np.float64)))
    gam = np.repeat(np.exp(log_gamma), HEAD_DIM)
    gcol = jnp.asarray(gam[:, None], F32)
    grow = jnp.asarray(gam[None, :], F32)
    kcol = k.reshape(nb, D_RET, 1)
    row = pl.BlockSpec((None, 1, D_RET), lambda bi: (bi, 0, 0))
    return pl.pallas_call(
        _ret_dec_kernel,
        out_shape=[jax.ShapeDtypeStruct((nb, 1, D_RET), F32), jax.ShapeDtypeStruct((nb, D_RET, HEAD_DIM), F32)],
        grid=(nb,),
        in_specs=[row, row, pl.BlockSpec((None, D_RET, 1), lambda bi: (bi, 0, 0)), row, row,
                  _const_spec(gcol.shape), _const_spec(grow.shape),
                  pl.BlockSpec((None, None, D_RET, HEAD_DIM), lambda bi: (layer, bi, 0, 0))],
        out_specs=[row, pl.BlockSpec((None, D_RET, HEAD_DIM), lambda bi: (bi, 0, 0))],
        compiler_params=_cparams(("parallel",)),
        name="ret_decode",
    )(q, k, kcol, v, g, gcol, grow, state)


def _swap_perm(rot_dim):
    half = rot_dim // 2
    p = np.arange(HEAD_DIM)
    p[:half] = np.arange(half, rot_dim)
    p[half:rot_dim] = np.arange(half)
    return p


def _rope_tables(pos, rot_dim, theta, width):
    half = rot_dim // 2
    inv = jnp.exp(-math.log(theta) * jnp.arange(half, dtype=F32) / half)
    ang = pos.astype(F32)[:, None] * inv[None, :]
    cos, sin = jnp.cos(ang), jnp.sin(ang)
    ones = jnp.ones((pos.shape[0], HEAD_DIM - rot_dim), F32)
    c = jnp.concatenate([cos, cos, ones], axis=1)
    s = jnp.concatenate([-sin, sin, 0.0 * ones], axis=1)
    reps = width // HEAD_DIM
    return jnp.tile(c, (1, reps)), jnp.tile(s, (1, reps))


def _layer_params(l, norm1_g, w_in, nsa_q_norm, nsa_k_norm, cmp_pos_k, cmp_pos_v, cmp_w_k, cmp_w_v, w_out,
                  norm2_g, w_up, w_down):
    w = w_in[l]
    d = w.shape[0]
    wm = jnp.pad(w, ((0, 0), (0, _N_IN_PAD - _N_IN))).astype(BF)
    gains = jnp.concatenate([nsa_q_norm[l][None], nsa_k_norm[l]], axis=0)
    gn = jnp.tile(gains, (1, LANES // HEAD_DIM))
    gs = jnp.tile(gains[:, _swap_perm(ROT_DIM)], (1, LANES // HEAD_DIM))
    eye2 = jnp.eye(NSA_KV, dtype=F32)
    return dict(
        g1=norm1_g[l][None], wm=wm, gn=gn, gs=gs,
        pk=jnp.tile(cmp_pos_k[l], (1, NSA_KV)), pv=jnp.tile(cmp_pos_v[l], (1, NSA_KV)),
        pkt=jnp.tile(cmp_pos_k[l].T, (NSA_KV, CMP_SPAN // BLK_CMP)),
        pvt=jnp.tile(cmp_pos_v[l].T, (NSA_KV, CMP_SPAN // BLK_CMP)),
        phik=jnp.kron(eye2, cmp_w_k[l]), phiv=jnp.kron(eye2, cmp_w_v[l]),
        wo=w_out[l].astype(BF), g2=norm2_g[l][None], wu=w_up[l].astype(BF), wd=w_down[l].astype(BF))


def _prompt_layer(xp2d, prm, tabs, *, b, t, tm):
    (sbk_f, sbv_f, ck_f, cv_f, sk_f, sv_f, wk_f, wv_f,
     sbq_b, sbk_b, sbv_b, rq_b, rk_b, rv_b, rg_f, nq_b, sk_b, sv_b, wk_b, wv_b, ng_f) = _project(
        xp2d, prm['g1'], prm['wm'], tabs, prm['gn'], prm['gs'], tm=tm, tab_blocks=t // tm, seqs=b)
    o_sb = _sb_prompt(sbq_b, sbk_b, sbv_b, b=b, t=t)
    o_ret, ret_st = _ret_prompt(rq_b, rk_b, rv_b, rg_f, b=b, t=t)
    kc, vc = _compress_prompt(ck_f, cv_f, prm['pkt'], prm['pvt'], prm['phik'], prm['phiv'])
    o_nsa = _nsa_prompt(nq_b, ng_f, kc, vc, sk_b, sv_b, wk_b, wv_b, b=b, t=t)
    y = _out_mlp(xp2d, o_sb, o_ret, o_nsa, prm['wo'], prm['g2'], prm['wu'], prm['wd'], tm=tm)
    keep = min(WINDOW, t)
    r4 = lambda a, h: jnp.transpose(a.reshape(b, h, HEAD_DIM, a.shape[-1]), (0, 3, 1, 2))
    st = ret_st.reshape(b, RET_HEADS, HEAD_DIM, RET_HEADS, HEAD_DIM)
    st = jnp.stack([st[:, h, :, h, :] for h in range(RET_HEADS)], axis=1)
    caches = dict(
        p_sb_k=r4(sbk_f, SB_HEADS), p_sb_v=r4(sbv_f, SB_HEADS),
        p_cmp_k=r4(ck_f, NSA_KV), p_cmp_v=r4(cv_f, NSA_KV),
        p_slc_k=r4(sk_f, NSA_KV), p_slc_v=r4(sv_f, NSA_KV),
        p_win_k=r4(wk_f[:, :, t - keep:], NSA_KV), p_win_v=r4(wv_f[:, :, t - keep:], NSA_KV),
        p_ret=st)
    return y, caches


def _sample_layer(l, xs2d, prm, tabs, caches, states, page_table, *, past_len):
    nb = xs2d.shape[0]
    (sbk_f, sbv_f, ck_f, cv_f, sk_f, sv_f, wk_f, wv_f,
     sbq_b, sbk_b, sbv_b, rq_b, rk_b, rv_b, rg_f, nq_b, sk_b, sv_b, wk_b, wv_b, ng_f) = _project(
        xs2d, prm['g1'], prm['wm'], tabs, prm['gn'], prm['gs'], tm=nb, tab_blocks=1)
    row = lambda a: a.astype(F32).reshape(nb, 1, a.shape[-1])
    c_sb_k, c_sb_v, c_cmp_k, c_cmp_v, c_slc_k, c_slc_v = caches
    win_k, win_v, st_ret = states
    pg = c_cmp_k.shape[3]
    nblk = page_table.shape[1] * pg // BLK_CMP
    o_sb = _sb_decode(l, page_table, row(sbq_b), c_sb_k, c_sb_v)
    o_ret, ret_new = _ret_decode(l, row(rq_b), row(rk_b), row(rv_b), row(rg_f), st_ret)
    kc, vc = _cmp_decode(l, page_table, row(ck_f), row(cv_f), c_cmp_k, c_cmp_v,
                         prm['pk'], prm['pv'], prm['pkt'], prm['pvt'], prm['phik'], prm['phiv'])
    ocmp, idx = _sel_decode(row(nq_b), kc, vc, q_pos=past_len, nblk=nblk)
    idx = jnp.transpose(idx[:, :, :NSA_KV], (0, 2, 1)).reshape(nb, NSA_KV * TOP_N)
    o_nsa, win_k_new, win_v_new = _nsa_decode(
        l, page_table, idx, row(nq_b), row(ng_f), ocmp, row(sk_f), row(sv_f), row(wk_f), row(wv_f),
        win_k, win_v, c_slc_k, c_slc_v)
    y = _out_mlp(xs2d, o_sb.reshape(nb, D_SB), o_ret.reshape(nb, D_RET), o_nsa.reshape(nb, D_NSA),
                 prm['wo'], prm['g2'], prm['wu'], prm['wd'], tm=nb)
    r4 = lambda a, h: a.reshape(nb, 1, h, HEAD_DIM)
    wb = win_k.shape[3]
    win4 = lambda a: jnp.transpose(a.reshape(nb, NSA_KV, HEAD_DIM, wb), (0, 3, 1, 2))
    out = dict(
        s_sb_k=r4(sbk_f, SB_HEADS), s_sb_v=r4(sbv_f, SB_HEADS),
        s_cmp_k=r4(ck_f, NSA_KV), s_cmp_v=r4(cv_f, NSA_KV),
        s_slc_k=r4(sk_f, NSA_KV), s_slc_v=r4(sv_f, NSA_KV),
        s_win_k=win4(win_k_new), s_win_v=win4(win_v_new),
        s_ret=ret_new.reshape(nb, RET_HEADS, HEAD_DIM, HEAD_DIM))
    return y, out


def kernel(x_prompt, x_sample, cache_sb_k, cache_sb_v, cache_cmp_k, cache_cmp_v, cache_slc_k, cache_slc_v,
           state_win_k, state_win_v, state_ret, page_table, norm1_g, w_in, nsa_q_norm, nsa_k_norm, cmp_pos_k,
           cmp_pos_v, cmp_w_k, cmp_w_v, w_out, norm2_g, w_up, w_down):
    b, t, d = x_prompt.shape
    depth = w_in.shape[0]
    tm = min(512, t)
    pos_p = jnp.arange(t, dtype=I32)
    tabs_p = (*_rope_tables(pos_p, HEAD_DIM, RET_THETA, D_RET), *_rope_tables(pos_p, ROT_DIM, ROPE_THETA, LANES))
    xp = x_prompt.reshape(b * t, d)

    nb, n_new, _ = x_sample.shape
    assert n_new == 1, "sample group kernels handle one new token per sample"
    pg = cache_sb_k.shape[2]
    past_len = page_table.shape[1] * pg
    wb = state_win_k.shape[2]
    assert wb <= WINDOW and wb <= past_len
    pos_s = jnp.full((nb,), past_len, dtype=I32)
    tabs_s = (*_rope_tables(pos_s, HEAD_DIM, RET_THETA, D_RET), *_rope_tables(pos_s, ROT_DIM, ROPE_THETA, LANES))
    xs = x_sample.reshape(nb, d)
    caches = tuple(_pages_t(c) for c in (cache_sb_k, cache_sb_v, cache_cmp_k, cache_cmp_v, cache_slc_k, cache_slc_v))
    states = (_pages_t(state_win_k), _pages_t(state_win_v),
              state_ret.reshape(depth, nb, RET_HEADS * HEAD_DIM, HEAD_DIM))

    new = {}
    for l in range(depth):
        prm = _layer_params(l, norm1_g, w_in, nsa_q_norm, nsa_k_norm, cmp_pos_k, cmp_pos_v, cmp_w_k, cmp_w_v,
                            w_out, norm2_g, w_up, w_down)
        xp, p_new = _prompt_layer(xp, prm, tabs_p, b=b, t=t, tm=tm)
        xs, s_new = _sample_layer(l, xs, prm, tabs_s, caches, states, page_table, past_len=past_len)
        for name, val in {**p_new, **s_new}.items():
            new.setdefault(name, []).append(val)
    st = lambda name: jnp.stack(new[name])
    return (xp.reshape(b, t, d), xs.reshape(nb, 1, d)) + tuple(st(nm) for nm in (
        'p_sb_k', 'p_sb_v', 'p_cmp_k', 'p_cmp_v', 'p_slc_k', 'p_slc_v', 'p_win_k', 'p_win_v', 'p_ret',
        's_sb_k', 's_sb_v', 's_cmp_k', 's_cmp_v', 's_slc_k', 's_slc_v', 's_win_k', 's_win_v', 's_ret'))
```

```python
import functools
import math

import numpy as np
import jax
import jax.numpy as jnp
from jax import lax
from jax.experimental import pallas as pl
from jax.experimental.pallas import tpu as pltpu

HEAD_DIM = 64
SB_HEADS = 4
RET_HEADS = 4
NSA_HEADS = 8
NSA_KV = 2
NSA_REP = NSA_HEADS // NSA_KV
D_SB = SB_HEADS * HEAD_DIM
D_RET = RET_HEADS * HEAD_DIM
D_NSA = NSA_HEADS * HEAD_DIM
D_KV = NSA_KV * HEAD_DIM
ROPE_THETA = 500000.0
ROT_DIM = HEAD_DIM // 4
RET_THETA = 10000.0
BLK_CMP = 32
BLK_SEL = 64
TOP_N = 16
WINDOW = 512
FORCE_SCORE = 1.0e4
NEG = -1.0e30
EPS = 1e-6
QK_SCALE = HEAD_DIM ** -0.5

LANES = 128
VMEM_LIMIT = 56 * 1024 * 1024

BF = jnp.bfloat16
F32 = jnp.float32
I32 = jnp.int32

_C_SBQ, _C_SBK, _C_SBV = 0, 256, 512
_C_RQ, _C_RK, _C_RV, _C_RG = 768, 1024, 1280, 1536
_C_NQ = 1792
_C_CK, _C_CV, _C_SK, _C_SV, _C_WK, _C_WV = 2304, 2432, 2560, 2688, 2816, 2944
_C_NG = 3072
_N_IN = 3096
_N_IN_PAD = 3200


def _dot(a, b):
    return jnp.dot(a, b, preferred_element_type=F32)


def _dot_nt(a, b):
    return lax.dot_general(a, b, (((1,), (1,)), ((), ())), preferred_element_type=F32)


def _dot_tn(a, b):
    return lax.dot_general(a, b, (((0,), (0,)), ((), ())), preferred_element_type=F32)


def _split2(x):
    hi = x.astype(BF)
    lo = (x - hi.astype(F32)).astype(BF)
    return hi, lo


def _split3(x):
    hi = x.astype(BF)
    r = x - hi.astype(F32)
    mid = r.astype(BF)
    lo = (r - mid.astype(F32)).astype(BF)
    return hi, mid, lo


def _dot_x2(x, w):
    hi, lo = _split2(x)
    return _dot(hi, w) + _dot(lo, w)


def _seg_mean_mat(n):
    r = lax.broadcasted_iota(I32, (n, n), 0) // HEAD_DIM
    c = lax.broadcasted_iota(I32, (n, n), 1) // HEAD_DIM
    return jnp.where(r == c, 1.0 / HEAD_DIM, 0.0).astype(BF)


def _cparams(sem, vmem=VMEM_LIMIT):
    return pltpu.CompilerParams(dimension_semantics=sem, vmem_limit_bytes=vmem)


def _const_spec(shape):
    nd = len(shape)
    return pl.BlockSpec(shape, lambda *a: (0,) * nd)


def _proj_kernel(x_ref, g1_ref, wm_ref, cr_ref, sr_ref, cn_ref, sn_ref, gn_ref, gs_ref,
                 sbk_f, sbv_f, ck_f, cv_f, sk_f, sv_f, wk_f, wv_f,
                 sbq_b, sbk_b, sbv_b, rq_b, rk_b, rv_b, rg_f, nq_b, sk_b, sv_b, wk_b, wv_b, ng_f, *, t_out):
    x = x_ref[...]
    ms = jnp.mean(x * x, axis=-1, keepdims=True)
    xn = (x * lax.rsqrt(ms + EPS) * g1_ref[...]).astype(BF)

    def put(ref, val):
        ref[...] = jnp.transpose(val) if t_out else val

    def mm(w_ref, lo, n):
        return _dot(xn, w_ref[:, lo:lo + n])

    sbq_b[...] = (mm(wm_ref, _C_SBQ, D_SB) * QK_SCALE).astype(BF)
    k = mm(wm_ref, _C_SBK, D_SB)
    put(sbk_f, k)
    sbk_b[...] = k.astype(BF)
    v = mm(wm_ref, _C_SBV, D_SB)
    put(sbv_f, v)
    sbv_b[...] = v.astype(BF)

    cr = cr_ref[...]
    sr = sr_ref[...]
    def swap_halves(y, half):
        d = lax.broadcasted_iota(I32, (y.shape[0], LANES), 1) % HEAD_DIM
        cols = [y[:, c:c + LANES] for c in range(0, y.shape[1], LANES)]
        cols = [jnp.where(d < half, pltpu.roll(c, LANES - half, 1), pltpu.roll(c, half, 1)) for c in cols]
        return cols[0] if len(cols) == 1 else jnp.concatenate(cols, axis=1)

    y = mm(wm_ref, _C_RQ, D_RET)
    rq_b[...] = (y * cr + swap_halves(y, HEAD_DIM // 2) * sr).astype(BF)
    y = mm(wm_ref, _C_RK, D_RET)
    rk_b[...] = ((y * cr + swap_halves(y, HEAD_DIM // 2) * sr) * QK_SCALE).astype(BF)
    rv_b[...] = mm(wm_ref, _C_RV, D_RET).astype(BF)
    rg_f[...] = mm(wm_ref, _C_RG, D_RET)

    seg = _seg_mean_mat(LANES)
    cn = cn_ref[...]
    sn = sn_ref[...]

    def normrope(cm, gi):
        y = mm(wm_ref, cm, LANES)
        ysw = swap_halves(y, ROT_DIM // 2)
        r = lax.rsqrt(_dot_x2(y * y, seg) + EPS)
        g = gn_ref[gi:gi + 1, :]
        gsw = gs_ref[gi:gi + 1, :]
        return r * (y * (g * cn) + ysw * (gsw * sn))

    for c in range(D_NSA // LANES):
        nq_b[:, c * LANES:(c + 1) * LANES] = (
            normrope(_C_NQ + c * LANES, 0) * QK_SCALE).astype(BF)
    put(ck_f, normrope(_C_CK, 1))
    put(cv_f, mm(wm_ref, _C_CV, D_KV))
    k = normrope(_C_SK, 2)
    put(sk_f, k)
    sk_b[...] = k.astype(BF)
    v = mm(wm_ref, _C_SV, D_KV)
    put(sv_f, v)
    sv_b[...] = v.astype(BF)
    k = normrope(_C_WK, 3)
    put(wk_f, k)
    wk_b[...] = k.astype(BF)
    v = mm(wm_ref, _C_WV, D_KV)
    put(wv_f, v)
    wv_b[...] = v.astype(BF)
    ng_f[...] = jax.nn.sigmoid(mm(wm_ref, _C_NG, LANES))


def _project(x2d, g1, wm, tabs, gn, gs, *, tm, tab_blocks, seqs=None):
    n, d = x2d.shape
    cr, sr, cn, sn = tabs
    grid = (n // tm,)
    tok = lambda w: pl.BlockSpec((tm, w), lambda i: (i, 0))
    tab = lambda w: pl.BlockSpec((tm, w), lambda i: (i % tab_blocks, 0))
    f32 = lambda w: jax.ShapeDtypeStruct((n, w), F32)
    bf = lambda w: jax.ShapeDtypeStruct((n, w), BF)
    out_w_f = [D_SB, D_SB] + [D_KV] * 6
    if seqs is None:
        cache_shapes = [f32(w) for w in out_w_f]
        cache_specs = [tok(w) for w in out_w_f]
    else:
        t = n // seqs
        per = t // tm
        cache_shapes = [jax.ShapeDtypeStruct((seqs, w, t), F32) for w in out_w_f]
        cache_specs = [pl.BlockSpec((None, w, tm), lambda i: (i // per, 0, i % per)) for w in out_w_f]
    out_shape = (cache_shapes
                 + [bf(D_SB)] * 3 + [bf(D_RET)] * 3 + [f32(D_RET), bf(D_NSA)] + [bf(D_KV)] * 4 + [f32(LANES)])
    out_w = [D_SB] * 3 + [D_RET] * 3 + [D_RET, D_NSA] + [D_KV] * 4 + [LANES]
    return pl.pallas_call(
        functools.partial(_proj_kernel, t_out=seqs is not None),
        out_shape=out_shape,
        grid=grid,
        in_specs=[tok(d), _const_spec((1, d)), _const_spec(wm.shape),
                  tab(D_RET), tab(D_RET), tab(LANES), tab(LANES),
                  _const_spec(gn.shape), _const_spec(gs.shape)],
        out_specs=cache_specs + [tok(w) for w in out_w],
        compiler_params=_cparams(("parallel",)),
        name="proj",
    )(x2d, g1, wm, cr, sr, cn, sn, gn, gs)


SB_DEAD = -120.0


def _softplus(z):
    return jnp.maximum(z, 0.0) + jnp.log(1.0 + jnp.exp(-jnp.abs(z)))


def _sb_kernel(q_ref, k_ref, v_ref, o_ref, c_ref, acc_ref, *, tq, tk):
    i = pl.program_id(1)
    m = SB_HEADS * tq
    q = q_ref[...]
    lane_h = lax.broadcasted_iota(I32, (tq, D_SB), 1) // HEAD_DIM
    qs = jnp.concatenate([jnp.where(lane_h == h, q, jnp.zeros_like(q)) for h in range(SB_HEADS)], axis=0)
    q_pos = i * tq + lax.broadcasted_iota(I32, (m, tk), 0) % tq
    col = lax.broadcasted_iota(I32, (m, tk), 1)
    tri = (lax.broadcasted_iota(I32, (tk, tk), 0) > lax.broadcasted_iota(I32, (tk, tk), 1)).astype(BF)
    nt = (i * tq) // tk + 1

    c_ref[...] = jnp.zeros_like(c_ref)
    acc_ref[...] = jnp.zeros_like(acc_ref)

    def tile(j, masked):
        off = pl.multiple_of(j * tk, tk)
        z = _dot_nt(qs, k_ref[pl.ds(off, tk), :])
        l1m = -_softplus(z)
        if masked:
            mask = (off + col) < q_pos
            l1m = jnp.where(mask, l1m, 0.0)
        c = c_ref[...]
        after = c + _dot_x2(l1m, tri)
        lw = z + l1m + after
        if masked:
            lw = jnp.where(mask, lw, NEG)
        w = jnp.exp(lw)
        acc_ref[...] += _dot(w.astype(BF), v_ref[pl.ds(off, tk), :])
        c_ref[...] = c + jnp.sum(l1m, axis=1, keepdims=True)

    tile(nt - 1, True)

    def alive():
        return jnp.max(c_ref[...]) > SB_DEAD

    def cond(state):
        return (state[0] < nt - 1) & state[1]

    def body(state):
        tile(nt - 2 - state[0], False)
        return state[0] + 1, alive()

    lax.while_loop(cond, body, (jnp.int32(0), alive()))

    acc = acc_ref[...]
    out = jnp.zeros((tq, D_SB), F32)
    for h in range(SB_HEADS):
        out = out + jnp.where(lane_h == h, acc[h * tq:(h + 1) * tq], 0.0)
    o_ref[...] = out.astype(BF)


def _sb_prompt(q, k, v, *, b, t, tq=256, tk=256):
    n = b * t
    nq = t // tq
    return pl.pallas_call(
        functools.partial(_sb_kernel, tq=tq, tk=tk),
        out_shape=jax.ShapeDtypeStruct((n, D_SB), BF),
        grid=(b, nq),
        in_specs=[pl.BlockSpec((tq, D_SB), lambda bi, i: (bi * nq + i, 0)),
                  pl.BlockSpec((t, D_SB), lambda bi, i: (bi, 0)),
                  pl.BlockSpec((t, D_SB), lambda bi, i: (bi, 0))],
        out_specs=pl.BlockSpec((tq, D_SB), lambda bi, i: (bi * nq + i, 0)),
        scratch_shapes=[pltpu.VMEM((SB_HEADS * tq, 1), F32), pltpu.VMEM((SB_HEADS * tq, D_SB), F32)],
        compiler_params=_cparams(("parallel", "parallel")),
        name="sb_prompt",
    )(q, k, v)


def _ret_kernel(q_ref, k_ref, v_ref, g_ref, dec_ref, qd_ref, kd_ref, gc_ref, o_ref, st_ref, s_scr, *, c):
    ci = pl.program_id(1)

    @pl.when(ci == 0)
    def _():
        s_scr[...] = jnp.zeros_like(s_scr)

    q = q_ref[...]
    k = k_ref[...]
    v = v_ref[...]
    lane_h = lax.broadcasted_iota(I32, (c, D_RET), 1) // HEAD_DIM
    o = jnp.zeros((c, D_RET), F32)
    for h in range(RET_HEADS):
        kh = jnp.where(lane_h == h, k, jnp.zeros_like(k))
        vh = jnp.where(lane_h == h, v, jnp.zeros_like(v))
        s = _dot_nt(q, kh) * dec_ref[h]
        o = o + _dot(s.astype(BF), vh)
    st = s_scr[...]
    shi, slo = _split2(st)
    o = o + (_dot(q, shi) + _dot(q, slo)) * qd_ref[...]

    kd = (k.astype(F32) * kd_ref[...]).astype(BF)
    ktv = _dot_tn(kd, v)
    r = lax.broadcasted_iota(I32, (D_RET, D_RET), 0) // HEAD_DIM
    cc = lax.broadcasted_iota(I32, (D_RET, D_RET), 1) // HEAD_DIM
    new_st = st * gc_ref[...] + jnp.where(r == cc, ktv, 0.0)
    s_scr[...] = new_st
    st_ref[...] = new_st

    seg = _seg_mean_mat(D_RET)
    mu = _dot_x2(o, seg)
    d = o - mu
    var = _dot_x2(d * d, seg)
    y = d * lax.rsqrt(var + EPS)
    o_ref[...] = (y * jax.nn.silu(g_ref[...])).astype(BF)


def _ret_tables(c):
    log_gamma = np.log1p(-np.exp2(-5.0 - np.arange(RET_HEADS, dtype=np.float64)))
    idx = np.arange(c, dtype=np.float64)
    diff = idx[:, None] - idx[None, :]
    dec = np.where(diff >= 0, np.exp(np.maximum(diff, 0.0)[None] * log_gamma[:, None, None]), 0.0)
    lane_lg = np.repeat(log_gamma, HEAD_DIM)
    qd = np.exp((idx[:, None] + 1.0) * lane_lg[None, :])
    kd = np.exp((c - 1.0 - idx)[:, None] * lane_lg[None, :])
    gc = np.exp(c * lane_lg)[None, :]
    f = lambda a: jnp.asarray(a, F32)
    return f(dec), f(qd), f(kd), f(gc)


def _ret_prompt(q, k, v, g, *, b, t, c=256):
    n = b * t
    nc = t // c
    dec, qd, kd, gc = _ret_tables(c)
    tok = pl.BlockSpec((c, D_RET), lambda bi, i: (bi * nc + i, 0))
    o, st = pl.pallas_call(
        functools.partial(_ret_kernel, c=c),
        out_shape=[jax.ShapeDtypeStruct((n, D_RET), BF), jax.ShapeDtypeStruct((b, D_RET, D_RET), F32)],
        grid=(b, nc),
        in_specs=[tok, tok, tok, tok, _const_spec(dec.shape), _const_spec(qd.shape), _const_spec(kd.shape),
                  _const_spec(gc.shape)],
        out_specs=[tok, pl.BlockSpec((None, D_RET, D_RET), lambda bi, i: (bi, 0, 0))],
        scratch_shapes=[pltpu.VMEM((D_RET, D_RET), F32)],
        compiler_params=_cparams(("parallel", "arbitrary")),
        name="ret_prompt",
    )(q, k, v, g, dec, qd, kd, gc)
    return o, st


CMP_SPAN = 2048


def _compress_t(xt, post, phi_hi, phi_lo):
    n = xt.shape[1]
    sel = (lax.broadcasted_iota(I32, (n, LANES), 0) // BLK_CMP
           == lax.broadcasted_iota(I32, (n, LANES), 1)).astype(BF)
    xs_t = _dot_x2(xt * post, sel)
    hi, lo = _split2(jnp.transpose(xs_t))
    out = _dot(hi, phi_hi) + _dot(lo, phi_hi) + _dot(hi, phi_lo)
    return out[0:n // BLK_CMP]


def _compress_kernel(k_ref, v_ref, pk_ref, pv_ref, wk_ref, wv_ref, ko_ref, vo_ref):
    wk_hi, wk_lo = _split2(wk_ref[...])
    wv_hi, wv_lo = _split2(wv_ref[...])
    ko_ref[...] = _compress_t(k_ref[...], pk_ref[...], wk_hi, wk_lo).astype(BF)
    vo_ref[...] = _compress_t(v_ref[...], pv_ref[...], wv_hi, wv_lo).astype(BF)


def _compress_prompt(ckt, cvt, pkt, pvt, wk, wv):
    b, _, t = ckt.shape
    steps = t // CMP_SPAN
    tok = pl.BlockSpec((None, LANES, CMP_SPAN), lambda bi, i: (bi, 0, i))
    out = pl.BlockSpec((CMP_SPAN // BLK_CMP, LANES), lambda bi, i: (bi * steps + i, 0))
    return pl.pallas_call(
        _compress_kernel,
        out_shape=[jax.ShapeDtypeStruct((b * t // BLK_CMP, LANES), BF)] * 2,
        grid=(b, steps),
        in_specs=[tok, tok, _const_spec(pkt.shape), _const_spec(pvt.shape), _const_spec(wk.shape),
                  _const_spec(wv.shape)],
        out_specs=[out, out],
        compiler_params=_cparams(("parallel", "parallel")),
        name="compress_prompt",
    )(ckt, cvt, pkt, pvt, wk, wv)


def _stack_heads_q(q, extra):
    tq = q.shape[0]
    lane = lax.broadcasted_iota(I32, (tq, LANES), 1)
    rows = []
    for h in range(NSA_HEADS):
        g = h // NSA_REP
        blk = q[:, (h // 2) * LANES:(h // 2 + 1) * LANES]
        src_half = h % 2
        if src_half != g:
            blk32 = pltpu.roll(blk.astype(F32), HEAD_DIM, 1).astype(BF)
        else:
            blk32 = blk
        keep = (lane // HEAD_DIM) == g
        rows.append(jnp.where(keep, blk32, jnp.zeros_like(blk32)))
    qs = jnp.concatenate(rows, axis=0)
    if extra is not None:
        qs = jnp.concatenate([qs, extra], axis=1)
    return qs


def _unstack_heads(acc, tq):
    lane = lax.broadcasted_iota(I32, (tq, LANES), 1)
    cols = []
    for c in range(NSA_HEADS // 2):
        g = (2 * c) // NSA_REP
        a = acc[(2 * c) * tq:(2 * c + 1) * tq]
        b = acc[(2 * c + 1) * tq:(2 * c + 2) * tq]
        if g == 0:
            cols.append(jnp.where(lane < HEAD_DIM, a, pltpu.roll(b, HEAD_DIM, 1)))
        else:
            cols.append(jnp.where(lane < HEAD_DIM, pltpu.roll(a, HEAD_DIM, 1), b))
    return jnp.concatenate(cols, axis=1)


RANK_UNROLL = 8


def _order_key(v):
    return v


def _beats(key_j, key, j_before):
    return jnp.where(j_before, (key_j >= key).astype(I32), (key_j > key).astype(I32))


def _gate_expand(gates, branch):
    r = lax.broadcasted_iota(I32, (LANES, D_NSA), 0)
    c = lax.broadcasted_iota(I32, (LANES, D_NSA), 1)
    e = (r == (c // HEAD_DIM) * 3 + branch).astype(BF)
    return _dot_x2(gates, e)


def _nsa_kernel(q_ref, gate_ref, kc_ref, vc_ref, sk_ref, sv_ref, wk_ref, wv_ref, o_ref,
                k2_scr, svt_scr, wvt_scr, imp_scr, m_scr, l_scr, acc_scr, sa_scr, sb_scr, *, tq, tk, tw, t):
    i = pl.program_id(1)
    nc = t // BLK_CMP
    ns = t // BLK_SEL
    m = NSA_HEADS * tq

    @pl.when(i == 0)
    def _():
        rows = 512
        for r0 in range(0, t, rows):
            s_idx = r0 + lax.broadcasted_iota(I32, (rows, LANES), 0)
            c_idx = lax.broadcasted_iota(I32, (rows, LANES), 1)
            e = (s_idx // BLK_SEL == c_idx).astype(BF)
            k2_scr[r0:r0 + rows, :] = jnp.concatenate([sk_ref[r0:r0 + rows, :], e], axis=1)
        for c in range(t // tk):
            svt_scr[c] = jnp.transpose(sv_ref[c * tk:(c + 1) * tk, :].astype(F32)).astype(BF)
        for c in range(t // tw):
            wvt_scr[c] = jnp.transpose(wv_ref[c * tw:(c + 1) * tw, :].astype(F32)).astype(BF)

    q = q_ref[...]
    qs = _stack_heads_q(q, None)

    kc = kc_ref[...]
    vct = jnp.transpose(vc_ref[...].astype(F32)).astype(BF)
    n_idx = lax.broadcasted_iota(I32, (nc, m), 0)
    t_idx = i * tq + lax.broadcasted_iota(I32, (nc, m), 1) % tq
    valid = (n_idx + 1) * BLK_CMP - 1 <= t_idx
    st = jnp.where(valid, _dot_nt(kc, qs), NEG)
    mx = jnp.max(st, axis=0, keepdims=True)
    e = jnp.where(valid, jnp.exp(st - mx), 0.0)
    den = jnp.sum(e, axis=0, keepdims=True)
    p = e / jnp.where(den > 0.0, den, 1.0)
    o_cmp = _dot(vct, p.astype(BF))
    imp = []
    for g in range(NSA_KV):
        acc_g = p[:, g * NSA_REP * tq:(g * NSA_REP + 1) * tq]
        for h in range(g * NSA_REP + 1, (g + 1) * NSA_REP):
            acc_g = acc_g + p[:, h * tq:(h + 1) * tq]
        imp.append(acc_g)

    pair = (lax.broadcasted_iota(I32, (ns, nc), 1) // (BLK_SEL // BLK_CMP)
            == lax.broadcasted_iota(I32, (ns, nc), 0)).astype(BF)
    blk = lax.broadcasted_iota(I32, (ns, tq), 0)
    tb = i * tq + lax.broadcasted_iota(I32, (ns, tq), 1)
    forced = (blk == 0) | (blk == tb // BLK_SEL)
    future = blk * BLK_SEL > tb
    eye = (lax.broadcasted_iota(I32, (tq, tq), 0) == lax.broadcasted_iota(I32, (tq, tq), 1)).astype(BF)
    pens = []
    for g in range(NSA_KV):
        a, b, c = _split3(imp[g])
        v = _dot(pair, a) + _dot(pair, b) + _dot(pair, c)
        v = jnp.where(forced, FORCE_SCORE, jnp.where(future, -1.0, v))
        key = _order_key(v)
        imp_scr[g] = key

        def body(jj, cnt, g=g, key=key):
            for u in range(RANK_UNROLL):
                j = jj * RANK_UNROLL + u
                cnt = cnt + _beats(imp_scr[g, pl.ds(j, 1), :], key, blk > j)
            return cnt

        n_vis = jnp.minimum(((i + 1) * tq - 1) // BLK_SEL + 1, ns)
        cnt = lax.fori_loop(0, (n_vis + RANK_UNROLL - 1) // RANK_UNROLL, body, jnp.zeros((ns, tq), I32))
        sel_t = (cnt < min(TOP_N, ns)).astype(BF)
        pen = _dot_nt(eye, sel_t)
        pen = ((pen - 1.0) * (-NEG)).astype(BF)
        if ns < LANES:
            pen = jnp.concatenate([pen, jnp.zeros((tq, LANES - ns), BF)], axis=1)
        pens.append(pen)
    pen_rows = jnp.concatenate([pens[h // NSA_REP] for h in range(NSA_HEADS)], axis=0)
    qs2 = jnp.concatenate([qs, pen_rows], axis=1)

    q_pos = i * tq + lax.broadcasted_iota(I32, (1, m), 1) % tq

    def online(scores, vals_t, mask):
        if mask is not None:
            scores = jnp.where(mask, scores, NEG)
        m_old = m_scr[...]
        m_new = jnp.maximum(m_old, jnp.max(scores, axis=0, keepdims=True))
        alpha = jnp.exp(m_old - m_new)
        p = jnp.exp(scores - m_new)
        l_scr[...] = alpha * l_scr[...] + jnp.sum(p, axis=0, keepdims=True)
        acc_scr[...] = alpha * acc_scr[...] + _dot(vals_t, p.astype(BF))
        m_scr[...] = m_new

    def reset():
        m_scr[...] = jnp.full_like(m_scr, NEG)
        l_scr[...] = jnp.zeros_like(l_scr)
        acc_scr[...] = jnp.zeros_like(acc_scr)

    def result():
        return acc_scr[...] / l_scr[...]

    reset()
    nt = (i * tq) // tk + 1
    row_k = lax.broadcasted_iota(I32, (tk, 1), 0)

    def scores_into(buf, j):
        off = pl.multiple_of(j * tk, tk)
        buf[...] = _dot_nt(k2_scr[pl.ds(off, tk), :], qs2)

    def diag_mask():
        return ((nt - 1) * tk + row_k) <= q_pos

    n_pairs = (nt - 1) // 2
    scores_into(sa_scr, 0)

    def slc_body(pr, carry):
        scores_into(sb_scr, 2 * pr + 1)
        online(sa_scr[...], svt_scr[2 * pr], None)
        scores_into(sa_scr, 2 * pr + 2)
        online(sb_scr[...], svt_scr[2 * pr + 1], None)
        return carry

    lax.fori_loop(0, n_pairs, slc_body, 0)

    @pl.when((nt - 1) % 2 == 1)
    def _():
        scores_into(sb_scr, nt - 1)
        online(sa_scr[...], svt_scr[nt - 2], None)
        online(sb_scr[...], svt_scr[nt - 1], diag_mask())

    @pl.when((nt - 1) % 2 == 0)
    def _():
        online(sa_scr[...], svt_scr[nt - 1], diag_mask())

    o_slc = result()

    row_w = lax.broadcasted_iota(I32, (tw, 1), 0)
    n_band = (WINDOW + tq + tw - 1) // tw
    first = jnp.maximum(i * tq - WINDOW, 0) // tw
    s_w = []
    for c in range(n_band):
        off = pl.multiple_of((first + c) * tw, tw)
        k_pos = off + row_w
        s = _dot_nt(wk_ref[pl.ds(off, tw), :], qs)
        s_w.append(jnp.where((k_pos <= q_pos) & (q_pos - k_pos <= WINDOW), s, NEG))
    m_w = s_w[0].max(axis=0, keepdims=True)
    for s in s_w[1:]:
        m_w = jnp.maximum(m_w, s.max(axis=0, keepdims=True))
    l_w = jnp.zeros((1, m), F32)
    o_win = jnp.zeros((LANES, m), F32)
    for c, s in enumerate(s_w):
        p = jnp.exp(s - m_w)
        l_w = l_w + p.sum(axis=0, keepdims=True)
        o_win = o_win + _dot(wvt_scr[first + c], p.astype(BF))
    o_win = o_win / l_w

    gt = jnp.transpose(gate_ref[...])

    def gate_row(branch):
        return jnp.concatenate([gt[h * 3 + branch:h * 3 + branch + 1, :] for h in range(NSA_HEADS)], axis=1)

    mix = gate_row(0) * o_cmp + gate_row(1) * o_slc + gate_row(2) * o_win
    stacked = jnp.concatenate([jnp.transpose(mix[:, h * tq:(h + 1) * tq]) for h in range(NSA_HEADS)], axis=0)
    o_ref[...] = _unstack_heads(stacked, tq).astype(BF)


def _nsa_prompt(nq, gates, kc, vc, sk, sv, wk, wv, *, b, t, tq=256, tk=512, tw=128):
    assert t % tk == 0 and (WINDOW + tq + tw - 1) // tw <= t // tw and (t // BLK_SEL) % RANK_UNROLL == 0
    n = b * t
    nqb = t // tq
    nc = t // BLK_CMP
    ns = t // BLK_SEL
    m = NSA_HEADS * tq
    tok = lambda w: pl.BlockSpec((tq, w), lambda bi, i: (bi * nqb + i, 0))
    seq = lambda rows, w: pl.BlockSpec((rows, w), lambda bi, i: (bi, 0))
    return pl.pallas_call(
        functools.partial(_nsa_kernel, tq=tq, tk=tk, tw=tw, t=t),
        out_shape=jax.ShapeDtypeStruct((n, D_NSA), BF),
        grid=(b, nqb),
        in_specs=[tok(D_NSA), tok(LANES), seq(nc, LANES), seq(nc, LANES),
                  seq(t, LANES), seq(t, LANES), seq(t, LANES), seq(t, LANES)],
        out_specs=tok(D_NSA),
        scratch_shapes=[pltpu.VMEM((t, 2 * LANES), BF), pltpu.VMEM((t // tk, LANES, tk), BF),
                        pltpu.VMEM((t // tw, LANES, tw), BF), pltpu.VMEM((NSA_KV, ns, tq), F32),
                        pltpu.VMEM((1, m), F32), pltpu.VMEM((1, m), F32), pltpu.VMEM((LANES, m), F32),
                        pltpu.VMEM((tk, m), F32), pltpu.VMEM((tk, m), F32)],
        compiler_params=_cparams(("parallel", "arbitrary")),
        name="nsa_prompt",
    )(nq, gates, kc, vc, sk, sv, wk, wv)


def _out_mlp_kernel(x_ref, osb_ref, oret_ref, onsa_ref, wo_ref, g2_ref, wu_ref, wd_ref, y_ref, *, ff_chunk):
    h = (x_ref[...] + _dot(osb_ref[...].astype(BF), wo_ref[0:D_SB, :])
         + _dot(oret_ref[...].astype(BF), wo_ref[D_SB:D_SB + D_RET, :])
         + _dot(onsa_ref[...].astype(BF), wo_ref[D_SB + D_RET:, :]))
    ms = jnp.mean(h * h, axis=-1, keepdims=True)
    hn = (h * lax.rsqrt(ms + EPS) * g2_ref[...]).astype(BF)
    mlp = None
    d_ff = wu_ref.shape[1]
    for c0 in range(0, d_ff, ff_chunk):
        u = jnp.maximum(_dot(hn, wu_ref[:, c0:c0 + ff_chunk]), 0.0)
        part = _dot((u * u).astype(BF), wd_ref[c0:c0 + ff_chunk, :])
        mlp = part if mlp is None else mlp + part
    y_ref[...] = h + mlp


def _out_mlp(x2d, osb, oret, onsa, wo, g2, wu, wd, *, tm, ff_chunk=1024):
    n, d = x2d.shape
    tok = lambda w: pl.BlockSpec((tm, w), lambda i: (i, 0))
    return pl.pallas_call(
        functools.partial(_out_mlp_kernel, ff_chunk=ff_chunk),
        out_shape=jax.ShapeDtypeStruct((n, d), F32),
        grid=(n // tm,),
        in_specs=[tok(d), tok(D_SB), tok(D_RET), tok(D_NSA), _const_spec(wo.shape), _const_spec((1, d)),
                  _const_spec(wu.shape), _const_spec(wd.shape)],
        out_specs=tok(d),
        compiler_params=_cparams(("parallel",)),
        name="out_mlp",
    )(x2d, osb, oret, onsa, wo, g2, wu, wd)


SB_PAGES_PER_FETCH = 4
DEC_ROWS = 16


def _pages_t(c):
    dp, npool, pg, h, dd = c.shape
    return jnp.transpose(c, (0, 1, 3, 4, 2)).reshape(dp, npool, h * dd, pg)


def _page_spec(layer, width, page_of_step, pg):
    def imap(bi, p, pt, *rest):
        return (layer, pt[bi, page_of_step(p)], 0, 0)
    return pl.BlockSpec((None, None, width, pg), imap)


def _sb_dec_kernel(pt_ref, q_ref, k_hbm, v_hbm, o_ref, kbuf, vbuf, sem, c_scr, acc_scr, *, layer, npg, npp):
    bi = pl.program_id(0)
    rows = DEC_ROWS
    pg = kbuf.shape[2]
    nchunks = npg // npp

    def chunk_copies(c):
        out = []
        for kk in range(npp):
            page = pt_ref[bi, npg - 1 - (c * npp + kk)]
            out.append(pltpu.make_async_copy(k_hbm.at[layer, page], kbuf.at[kk], sem.at[0, kk]))
            out.append(pltpu.make_async_copy(v_hbm.at[layer, page], vbuf.at[kk], sem.at[1, kk]))
        return out

    c_scr[...] = jnp.zeros_like(c_scr)
    acc_scr[...] = jnp.zeros_like(acc_scr)
    lane_h = lax.broadcasted_iota(I32, (rows, D_SB), 1) // HEAD_DIM
    row = lax.broadcasted_iota(I32, (rows, D_SB), 0)
    q = jnp.broadcast_to(q_ref[...].astype(F32), (rows, D_SB))
    qs = jnp.where(lane_h == row, q, 0.0).astype(BF)
    tri = (lax.broadcasted_iota(I32, (pg, pg), 0) > lax.broadcasted_iota(I32, (pg, pg), 1)).astype(BF)
    head_rows = lax.broadcasted_iota(I32, (rows, 1), 0) < SB_HEADS

    def cond(state):
        return (state[0] < nchunks) & state[1]

    def body(state):
        ch = state[0]
        for cp in chunk_copies(ch):
            cp.start()
        for cp in chunk_copies(ch):
            cp.wait()
        c = c_scr[...]
        acc = acc_scr[...]
        for kk in range(npp):
            z = _dot(qs, kbuf[kk].astype(BF))
            l1m = -_softplus(z)
            after = c + _dot_x2(l1m, tri)
            w = jnp.exp(z + l1m + after)
            acc = acc + _dot_nt(w.astype(BF), vbuf[kk].astype(BF))
            c = c + jnp.sum(l1m, axis=1, keepdims=True)
        c_scr[...] = c
        acc_scr[...] = acc
        return ch + 1, jnp.max(jnp.where(head_rows, c, NEG)) > SB_DEAD

    lax.while_loop(cond, body, (jnp.int32(0), jnp.bool_(True)))
    o_ref[...] = jnp.sum(jnp.where(lane_h == row, acc_scr[...], 0.0), axis=0, keepdims=True)


def _sb_decode(layer, page_table, q, cache_k, cache_v):
    nb, npg = page_table.shape
    npp = math.gcd(SB_PAGES_PER_FETCH, npg)
    pg = cache_k.shape[3]
    row = pl.BlockSpec((None, 1, D_SB), lambda bi, pt: (bi, 0, 0))
    hbm = pl.BlockSpec(memory_space=pl.ANY)
    return pl.pallas_call(
        functools.partial(_sb_dec_kernel, layer=layer, npg=npg, npp=npp),
        out_shape=jax.ShapeDtypeStruct((nb, 1, D_SB), F32),
        grid_spec=pltpu.PrefetchScalarGridSpec(
            num_scalar_prefetch=1, grid=(nb,),
            in_specs=[row, hbm, hbm], out_specs=row,
            scratch_shapes=[pltpu.VMEM((npp, D_SB, pg), F32), pltpu.VMEM((npp, D_SB, pg), F32),
                            pltpu.SemaphoreType.DMA((2, npp)),
                            pltpu.VMEM((DEC_ROWS, 1), F32), pltpu.VMEM((DEC_ROWS, D_SB), F32)]),
        compiler_params=_cparams(("arbitrary",)),
        name="sb_decode",
    )(page_table, q, cache_k, cache_v)


def _cmp_dec_kernel(pt_ref, kn_ref, vn_ref, pk_ref, pv_ref, pkt_ref, pvt_ref, wk_ref, wv_ref, *refs,
                    npp, last, nblk):
    k_refs, v_refs = refs[:npp], refs[npp:2 * npp]
    ko_ref, vo_ref = refs[2 * npp:]
    p = pl.program_id(1)
    wk_hi, wk_lo = _split2(wk_ref[...])
    wv_hi, wv_lo = _split2(wv_ref[...])
    pg = k_refs[0].shape[1]
    per = npp * pg // BLK_CMP
    r0 = pl.multiple_of(p * per, per)
    ko_ref[pl.ds(r0, per), :] = _compress_t(
        jnp.concatenate([r[...] for r in k_refs], axis=1), pkt_ref[...], wk_hi, wk_lo)
    vo_ref[pl.ds(r0, per), :] = _compress_t(
        jnp.concatenate([r[...] for r in v_refs], axis=1), pvt_ref[...], wv_hi, wv_lo)

    @pl.when(p == last)
    def _():
        first = lax.broadcasted_iota(I32, (DEC_ROWS, LANES), 0) == 0
        xk = jnp.where(first, kn_ref[...] * pk_ref[0:1, :], 0.0)
        xv = jnp.where(first, vn_ref[...] * pv_ref[0:1, :], 0.0)
        a, b = _split2(xk)
        ko_ref[nblk:nblk + DEC_ROWS, :] = _dot(a, wk_hi) + _dot(b, wk_hi) + _dot(a, wk_lo)
        a, b = _split2(xv)
        vo_ref[nblk:nblk + DEC_ROWS, :] = _dot(a, wv_hi) + _dot(b, wv_hi) + _dot(a, wv_lo)


def _cmp_decode(layer, page_table, ck_new, cv_new, cache_k, cache_v, pk, pv, pkt, pvt, wk, wv):
    nb, npg = page_table.shape
    pg = cache_k.shape[3]
    npp = CMP_SPAN // pg
    assert npg % npp == 0
    steps = npg // npp
    nblk = npg * pg // BLK_CMP
    specs = [_page_spec(layer, D_KV, (lambda p, kk=kk: p * npp + kk), pg) for kk in range(npp)]
    row = pl.BlockSpec((None, 1, D_KV), lambda bi, p, pt: (bi, 0, 0))
    cst = lambda a: pl.BlockSpec(a.shape, lambda bi, p, pt: (0,) * a.ndim)
    out = pl.BlockSpec((None, nblk + DEC_ROWS, LANES), lambda bi, p, pt: (bi, 0, 0))
    return pl.pallas_call(
        functools.partial(_cmp_dec_kernel, npp=npp, last=steps - 1, nblk=nblk),
        out_shape=[jax.ShapeDtypeStruct((nb, nblk + DEC_ROWS, LANES), F32)] * 2,
        grid_spec=pltpu.PrefetchScalarGridSpec(
            num_scalar_prefetch=1, grid=(nb, steps),
            in_specs=[row, row, cst(pk), cst(pv), cst(pkt), cst(pvt), cst(wk), cst(wv)] + specs + specs,
            out_specs=[out, out]),
        compiler_params=_cparams(("parallel", "arbitrary")),
        name="cmp_decode",
    )(page_table, ck_new, cv_new, pk, pv, pkt, pvt, wk, wv, *([cache_k] * npp), *([cache_v] * npp))


def _stack_heads_q_row(q):
    qf = q.astype(F32)
    row = lax.broadcasted_iota(I32, (DEC_ROWS, LANES), 0)
    lane_g = lax.broadcasted_iota(I32, (DEC_ROWS, LANES), 1) // HEAD_DIM
    out = jnp.zeros((DEC_ROWS, LANES), F32)
    for h in range(NSA_HEADS):
        g = h // NSA_REP
        blk = jnp.broadcast_to(qf[:, (h // 2) * LANES:(h // 2 + 1) * LANES], (DEC_ROWS, LANES))
        if h % 2 != g:
            blk = pltpu.roll(blk, HEAD_DIM, 1)
        out = jnp.where((row == h) & (lane_g == g), blk, out)
    return out.astype(BF)


def _unstack_heads_row(acc):
    lane = lax.broadcasted_iota(I32, (1, LANES), 1)
    accr = pltpu.roll(acc, HEAD_DIM, 1)
    cols = []
    for c in range(NSA_HEADS // 2):
        g = (2 * c) // NSA_REP
        if g == 0:
            cols.append(jnp.where(lane < HEAD_DIM, acc[2 * c:2 * c + 1], accr[2 * c + 1:2 * c + 2]))
        else:
            cols.append(jnp.where(lane < HEAD_DIM, accr[2 * c:2 * c + 1], acc[2 * c + 1:2 * c + 2]))
    return jnp.concatenate(cols, axis=1)


def _sel_dec_kernel(q_ref, kc_ref, vc_ref, ocmp_ref, idx_ref, imp_scr, *, q_pos, nblk, ns_pad):
    nrow = kc_ref.shape[0]
    qs = _stack_heads_q_row(q_ref[...])
    qs = jnp.concatenate([qs, jnp.zeros((LANES - DEC_ROWS, LANES), BF)], axis=0)
    kc = kc_ref[...].astype(BF)
    st = _dot_nt(kc, qs)
    n_idx = lax.broadcasted_iota(I32, (nrow, LANES), 0)
    valid = (n_idx + 1) * BLK_CMP - 1 <= q_pos
    st = jnp.where(valid, st, NEG)
    mx = jnp.max(st, axis=0, keepdims=True)
    e = jnp.where(valid, jnp.exp(st - mx), 0.0)
    den = jnp.sum(e, axis=0, keepdims=True)
    p = e / jnp.where(den > 0.0, den, 1.0)
    ocmp_ref[...] = _dot_tn(p.astype(BF), vc_ref[...].astype(BF))[0:NSA_HEADS]

    grp = (lax.broadcasted_iota(I32, (LANES, LANES), 0) // NSA_REP
           == lax.broadcasted_iota(I32, (LANES, LANES), 1)).astype(BF)
    a, b, c = _split3(p)
    impc = _dot(a, grp) + _dot(b, grp) + _dot(c, grp)
    pair = (lax.broadcasted_iota(I32, (ns_pad, nrow), 1) // (BLK_SEL // BLK_CMP)
            == lax.broadcasted_iota(I32, (ns_pad, nrow), 0)).astype(BF)
    a, b, c = _split3(impc)
    v = _dot(pair, a) + _dot(pair, b) + _dot(pair, c)
    ns = (nblk * BLK_CMP + 1 + BLK_SEL - 1) // BLK_SEL
    blk = lax.broadcasted_iota(I32, (ns_pad, LANES), 0)
    forced = (blk == 0) | (blk == q_pos // BLK_SEL)
    future = blk * BLK_SEL > q_pos
    v = jnp.where(forced, FORCE_SCORE, jnp.where(future, -1.0, v))
    v = jnp.where(blk < ns, v, -1.0)
    key = _order_key(v)
    imp_scr[...] = key

    def body(j, cnt):
        return cnt + _beats(imp_scr[pl.ds(j, 1), :], key, blk > j)

    cnt = lax.fori_loop(0, ns_pad, body, jnp.zeros((ns_pad, LANES), I32), unroll=8)
    blk_f = blk.astype(F32)
    rows = [jnp.sum(jnp.where(cnt == r, blk_f, 0.0), axis=0, keepdims=True) for r in range(TOP_N)]
    idx_ref[...] = jnp.concatenate(rows, axis=0).astype(I32)


def _sel_decode(nq, kc, vc, *, q_pos, nblk):
    nb = nq.shape[0]
    nrow = kc.shape[1]
    ns_pad = ((nrow // 2 + 7) // 8) * 8
    return pl.pallas_call(
        functools.partial(_sel_dec_kernel, q_pos=q_pos, nblk=nblk, ns_pad=ns_pad),
        out_shape=[jax.ShapeDtypeStruct((nb, NSA_HEADS, LANES), F32), jax.ShapeDtypeStruct((nb, TOP_N, LANES), I32)],
        grid=(nb,),
        in_specs=[pl.BlockSpec((None, 1, D_NSA), lambda bi: (bi, 0, 0)),
                  pl.BlockSpec((None, nrow, LANES), lambda bi: (bi, 0, 0)),
                  pl.BlockSpec((None, nrow, LANES), lambda bi: (bi, 0, 0))],
        out_specs=[pl.BlockSpec((None, NSA_HEADS, LANES), lambda bi: (bi, 0, 0)),
                   pl.BlockSpec((None, TOP_N, LANES), lambda bi: (bi, 0, 0))],
        scratch_shapes=[pltpu.VMEM((ns_pad, LANES), F32)],
        compiler_params=_cparams(("parallel",)),
        name="sel_decode",
    )(nq, kc, vc)


def _nsa_dec_kernel(pt_ref, idx_ref, q_ref, gate_ref, ocmp_ref, skn_ref, svn_ref, wkn_ref, wvn_ref,
                    wkc_ref, wvc_ref, wk_ref, wv_ref, ck_hbm, cv_hbm, o_ref, wko_ref, wvo_ref, kbuf, vbuf, sem,
                    *, layer, ncache, per):
    nsel = NSA_KV * TOP_N
    bi = pl.program_id(0)
    pg = kbuf.shape[2]

    def page_copies(j):
        blk = jnp.minimum(idx_ref[bi, j], ncache - 1)
        page = pt_ref[bi, blk // per]
        return (pltpu.make_async_copy(ck_hbm.at[layer, page], kbuf.at[j], sem.at[0, j]),
                pltpu.make_async_copy(cv_hbm.at[layer, page], vbuf.at[j], sem.at[1, j]))

    for j in range(nsel):
        for cp in page_copies(j):
            cp.start()

    qs = _stack_heads_q_row(q_ref[...])
    qf = qs.astype(F32)
    row_g = lax.broadcasted_iota(I32, (DEC_ROWS, LANES), 0) // NSA_REP

    def attend(s, vt, s_new, v_new):
        mx = jnp.maximum(s_new, jnp.max(s, axis=1, keepdims=True))
        p_new = jnp.exp(s_new - mx)
        p = jnp.exp(s - mx)
        den = p_new + jnp.sum(p, axis=1, keepdims=True)
        return (p_new * v_new + _dot_nt(p.astype(BF), vt)) / den

    bf_row = lambda r: r[...].astype(BF).astype(F32)
    wk = wk_ref[...]
    wv = wv_ref[...]
    s_wn = jnp.sum(qf * bf_row(wkn_ref), axis=1, keepdims=True)
    o_win = attend(_dot(qs, wk.astype(BF)), wv.astype(BF), s_wn, bf_row(wvn_ref))

    wb = wk.shape[1]
    last = lax.broadcasted_iota(I32, (D_KV, wb), 1) == wb - 1
    wko_ref[...] = jnp.where(last, wkc_ref[...], pltpu.roll(wk, wb - 1, 1))
    wvo_ref[...] = jnp.where(last, wvc_ref[...], pltpu.roll(wv, wb - 1, 1))

    for j in range(nsel):
        for cp in page_copies(j):
            cp.wait()

    s_new = jnp.sum(qf * bf_row(skn_ref), axis=1, keepdims=True)
    nkeys = TOP_N * pg
    col = lax.broadcasted_iota(I32, (DEC_ROWS, nkeys), 1)
    col_slot = col // pg
    col_blk = (col % pg) // BLK_SEL
    o_slc = jnp.zeros((DEC_ROWS, LANES), F32)
    for g in range(NSA_KV):
        pen = jnp.full((DEC_ROWS, nkeys), NEG, F32)
        for r in range(TOP_N):
            blk = idx_ref[bi, g * TOP_N + r]
            want = jnp.where(blk < ncache, blk % per, -1)
            pen = jnp.where((col_slot == r) & (col_blk == want), 0.0, pen)
        kcat = jnp.concatenate([kbuf[g * TOP_N + r] for r in range(TOP_N)], axis=1).astype(BF)
        vcat = jnp.concatenate([vbuf[g * TOP_N + r] for r in range(TOP_N)], axis=1).astype(BF)
        o_g = attend(_dot(qs, kcat) + pen, vcat, s_new, bf_row(svn_ref))
        o_slc = jnp.where(row_g == g, o_g, o_slc)

    gates = jnp.broadcast_to(gate_ref[...], (DEC_ROWS, LANES))
    out = (_gate_expand(gates, 0)[0:1] * _unstack_heads_row(ocmp_ref[...])
           + _gate_expand(gates, 1)[0:1] * _unstack_heads_row(o_slc)
           + _gate_expand(gates, 2)[0:1] * _unstack_heads_row(o_win))
    o_ref[...] = out


def _nsa_decode(layer, page_table, idx, nq, gates, ocmp, sk_new, sv_new, wk_new, wv_new, win_k, win_v,
                cache_k, cache_v):
    nb, npg = page_table.shape
    pg = cache_k.shape[3]
    per = pg // BLK_SEL
    ncache = npg * per
    wb = win_k.shape[3]
    nsel = NSA_KV * TOP_N

    row = lambda w: pl.BlockSpec((None, 1, w), lambda bi, pt, ix: (bi, 0, 0))
    colspec = pl.BlockSpec((None, D_KV, 1), lambda bi, pt, ix: (bi, 0, 0))
    win = pl.BlockSpec((None, None, D_KV, wb), lambda bi, pt, ix: (layer, bi, 0, 0))
    wout = pl.BlockSpec((None, D_KV, wb), lambda bi, pt, ix: (bi, 0, 0))
    hbm = pl.BlockSpec(memory_space=pl.ANY)
    col = lambda a: a.reshape(nb, D_KV, 1)
    return pl.pallas_call(
        functools.partial(_nsa_dec_kernel, layer=layer, ncache=ncache, per=per),
        out_shape=[jax.ShapeDtypeStruct((nb, 1, D_NSA), F32), jax.ShapeDtypeStruct((nb, D_KV, wb), F32),
                   jax.ShapeDtypeStruct((nb, D_KV, wb), F32)],
        grid_spec=pltpu.PrefetchScalarGridSpec(
            num_scalar_prefetch=2, grid=(nb,),
            in_specs=[row(D_NSA), row(LANES), pl.BlockSpec((None, NSA_HEADS, LANES), lambda bi, pt, ix: (bi, 0, 0)),
                      row(D_KV), row(D_KV), row(D_KV), row(D_KV), colspec, colspec, win, win, hbm, hbm],
            out_specs=[row(D_NSA), wout, wout],
            scratch_shapes=[pltpu.VMEM((nsel, D_KV, pg), F32), pltpu.VMEM((nsel, D_KV, pg), F32),
                            pltpu.SemaphoreType.DMA((2, nsel))]),
        compiler_params=_cparams(("arbitrary",)),
        name="nsa_decode",
    )(page_table, idx, nq, gates, ocmp, sk_new, sv_new, wk_new, wv_new, col(wk_new), col(wv_new),
      win_k, win_v, cache_k, cache_v)


def _ret_dec_kernel(q_ref, k_ref, kcol_ref, v_ref, g_ref, gcol_ref, grow_ref, st_ref, o_ref, sto_ref):
    rows = DEC_ROWS
    st = st_ref[...]
    q = jnp.broadcast_to(q_ref[...], (rows, D_RET))
    v = jnp.broadcast_to(v_ref[...], (rows, D_RET))
    lane_h = lax.broadcasted_iota(I32, (rows, D_RET), 1) // HEAD_DIM
    row = lax.broadcasted_iota(I32, (rows, D_RET), 0)
    own = lane_h == row
    qs = jnp.where(own, q, 0.0).astype(BF)
    spread = (lax.broadcasted_iota(I32, (HEAD_DIM, D_RET), 1) % HEAD_DIM
              == lax.broadcasted_iota(I32, (HEAD_DIM, D_RET), 0)).astype(BF)
    shi, slo = _split2(st)
    cross = _dot(qs, shi) + _dot(qs, slo)
    o_cross = jnp.sum(jnp.where(own, _dot_x2(cross, spread), 0.0), axis=0, keepdims=True) * grow_ref[...]
    seg = _seg_mean_mat(D_RET)
    qk = q * jnp.broadcast_to(k_ref[...], (rows, D_RET))
    o = o_cross + (_dot_x2(qk, seg) * float(HEAD_DIM)) * v

    v4 = _dot_nt(jnp.where(own, v, 0.0).astype(BF), spread)
    pick = (lax.broadcasted_iota(I32, (D_RET, rows), 0) // HEAD_DIM
            == lax.broadcasted_iota(I32, (D_RET, rows), 1)).astype(BF)
    vexp = _dot(pick, v4.astype(BF))
    sto_ref[...] = st * gcol_ref[...] + kcol_ref[...] * vexp

    mu = _dot_x2(o, seg)
    d = o - mu
    var = _dot_x2(d * d, seg)
    y = d * lax.rsqrt(var + EPS)
    o_ref[...] = y[0:1] * jax.nn.silu(g_ref[...])


def _ret_decode(layer, q, k, v, g, state):
    nb = q.shape[0]
    log_gamma = np.log1p(-np.exp2(-5.0 - np.arange(RET_HEADS, dtype=np.float64)))
    gam = np.repeat(np.exp(log_gamma), HEAD_DIM)
    gcol = jnp.asarray(gam[:, None], F32)
    grow = jnp.asarray(gam[None, :], F32)
    kcol = k.reshape(nb, D_RET, 1)
    row = pl.BlockSpec((None, 1, D_RET), lambda bi: (bi, 0, 0))
    return pl.pallas_call(
        _ret_dec_kernel,
        out_shape=[jax.ShapeDtypeStruct((nb, 1, D_RET), F32), jax.ShapeDtypeStruct((nb, D_RET, HEAD_DIM), F32)],
        grid=(nb,),
        in_specs=[row, row, pl.BlockSpec((None, D_RET, 1), lambda bi: (bi, 0, 0)), row, row,
                  _const_spec(gcol.shape), _const_spec(grow.shape),
                  pl.BlockSpec((None, None, D_RET, HEAD_DIM), lambda bi: (layer, bi, 0, 0))],
        out_specs=[row, pl.BlockSpec((None, D_RET, HEAD_DIM), lambda bi: (bi, 0, 0))],
        compiler_params=_cparams(("parallel",)),
        name="ret_decode",
    )(q, k, kcol, v, g, gcol, grow, state)


def _swap_perm(rot_dim):
    half = rot_dim // 2
    p = np.arange(HEAD_DIM)
    p[:half] = np.arange(half, rot_dim)
    p[half:rot_dim] = np.arange(half)
    return p


def _rope_tables(pos, rot_dim, theta, width):
    half = rot_dim // 2
    inv = jnp.exp(-math.log(theta) * jnp.arange(half, dtype=F32) / half)
    ang = pos.astype(F32)[:, None] * inv[None, :]
    cos, sin = jnp.cos(ang), jnp.sin(ang)
    ones = jnp.ones((pos.shape[0], HEAD_DIM - rot_dim), F32)
    c = jnp.concatenate([cos, cos, ones], axis=1)
    s = jnp.concatenate([-sin, sin, 0.0 * ones], axis=1)
    reps = width // HEAD_DIM
    return jnp.tile(c, (1, reps)), jnp.tile(s, (1, reps))


def _layer_params(l, norm1_g, w_in, nsa_q_norm, nsa_k_norm, cmp_pos_k, cmp_pos_v, cmp_w_k, cmp_w_v, w_out,
                  norm2_g, w_up, w_down):
    w = w_in[l]
    d = w.shape[0]
    wm = jnp.pad(w, ((0, 0), (0, _N_IN_PAD - _N_IN))).astype(BF)
    gains = jnp.concatenate([nsa_q_norm[l][None], nsa_k_norm[l]], axis=0)
    gn = jnp.tile(gains, (1, LANES // HEAD_DIM))
    gs = jnp.tile(gains[:, _swap_perm(ROT_DIM)], (1, LANES // HEAD_DIM))
    eye2 = jnp.eye(NSA_KV, dtype=F32)
    return dict(
        g1=norm1_g[l][None], wm=wm, gn=gn, gs=gs,
        pk=jnp.tile(cmp_pos_k[l], (1, NSA_KV)), pv=jnp.tile(cmp_pos_v[l], (1, NSA_KV)),
        pkt=jnp.tile(cmp_pos_k[l].T, (NSA_KV, CMP_SPAN // BLK_CMP)),
        pvt=jnp.tile(cmp_pos_v[l].T, (NSA_KV, CMP_SPAN // BLK_CMP)),
        phik=jnp.kron(eye2, cmp_w_k[l]), phiv=jnp.kron(eye2, cmp_w_v[l]),
        wo=w_out[l].astype(BF), g2=norm2_g[l][None], wu=w_up[l].astype(BF), wd=w_down[l].astype(BF))


def _prompt_layer(xp2d, prm, tabs, *, b, t, tm):
    (sbk_f, sbv_f, ck_f, cv_f, sk_f, sv_f, wk_f, wv_f,
     sbq_b, sbk_b, sbv_b, rq_b, rk_b, rv_b, rg_f, nq_b, sk_b, sv_b, wk_b, wv_b, ng_f) = _project(
        xp2d, prm['g1'], prm['wm'], tabs, prm['gn'], prm['gs'], tm=tm, tab_blocks=t // tm, seqs=b)
    o_sb = _sb_prompt(sbq_b, sbk_b, sbv_b, b=b, t=t)
    o_ret, ret_st = _ret_prompt(rq_b, rk_b, rv_b, rg_f, b=b, t=t)
    kc, vc = _compress_prompt(ck_f, cv_f, prm['pkt'], prm['pvt'], prm['phik'], prm['phiv'])
    o_nsa = _nsa_prompt(nq_b, ng_f, kc, vc, sk_b, sv_b, wk_b, wv_b, b=b, t=t)
    y = _out_mlp(xp2d, o_sb, o_ret, o_nsa, prm['wo'], prm['g2'], prm['wu'], prm['wd'], tm=tm)
    keep = min(WINDOW, t)
    r4 = lambda a, h: jnp.transpose(a.reshape(b, h, HEAD_DIM, a.shape[-1]), (0, 3, 1, 2))
    st = ret_st.reshape(b, RET_HEADS, HEAD_DIM, RET_HEADS, HEAD_DIM)
    st = jnp.stack([st[:, h, :, h, :] for h in range(RET_HEADS)], axis=1)
    caches = dict(
        p_sb_k=r4(sbk_f, SB_HEADS), p_sb_v=r4(sbv_f, SB_HEADS),
        p_cmp_k=r4(ck_f, NSA_KV), p_cmp_v=r4(cv_f, NSA_KV),
        p_slc_k=r4(sk_f, NSA_KV), p_slc_v=r4(sv_f, NSA_KV),
        p_win_k=r4(wk_f[:, :, t - keep:], NSA_KV), p_win_v=r4(wv_f[:, :, t - keep:], NSA_KV),
        p_ret=st)
    return y, caches


def _sample_layer(l, xs2d, prm, tabs, caches, states, page_table, *, past_len):
    nb = xs2d.shape[0]
    (sbk_f, sbv_f, ck_f, cv_f, sk_f, sv_f, wk_f, wv_f,
     sbq_b, sbk_b, sbv_b, rq_b, rk_b, rv_b, rg_f, nq_b, sk_b, sv_b, wk_b, wv_b, ng_f) = _project(
        xs2d, prm['g1'], prm['wm'], tabs, prm['gn'], prm['gs'], tm=nb, tab_blocks=1)
    row = lambda a: a.astype(F32).reshape(nb, 1, a.shape[-1])
    c_sb_k, c_sb_v, c_cmp_k, c_cmp_v, c_slc_k, c_slc_v = caches
    win_k, win_v, st_ret = states
    pg = c_cmp_k.shape[3]
    nblk = page_table.shape[1] * pg // BLK_CMP
    o_sb = _sb_decode(l, page_table, row(sbq_b), c_sb_k, c_sb_v)
    o_ret, ret_new = _ret_decode(l, row(rq_b), row(rk_b), row(rv_b), row(rg_f), st_ret)
    kc, vc = _cmp_decode(l, page_table, row(ck_f), row(cv_f), c_cmp_k, c_cmp_v,
                         prm['pk'], prm['pv'], prm['pkt'], prm['pvt'], prm['phik'], prm['phiv'])
    ocmp, idx = _sel_decode(row(nq_b), kc, vc, q_pos=past_len, nblk=nblk)
    idx = jnp.transpose(idx[:, :, :NSA_KV], (0, 2, 1)).reshape(nb, NSA_KV * TOP_N)
    o_nsa, win_k_new, win_v_new = _nsa_decode(
        l, page_table, idx, row(nq_b), row(ng_f), ocmp, row(sk_f), row(sv_f), row(wk_f), row(wv_f),
        win_k, win_v, c_slc_k, c_slc_v)
    y = _out_mlp(xs2d, o_sb.reshape(nb, D_SB), o_ret.reshape(nb, D_RET), o_nsa.reshape(nb, D_NSA),
                 prm['wo'], prm['g2'], prm['wu'], prm['wd'], tm=nb)
    r4 = lambda a, h: a.reshape(nb, 1, h, HEAD_DIM)
    wb = win_k.shape[3]
    win4 = lambda a: jnp.transpose(a.reshape(nb, NSA_KV, HEAD_DIM, wb), (0, 3, 1, 2))
    out = dict(
        s_sb_k=r4(sbk_f, SB_HEADS), s_sb_v=r4(sbv_f, SB_HEADS),
        s_cmp_k=r4(ck_f, NSA_KV), s_cmp_v=r4(cv_f, NSA_KV),
        s_slc_k=r4(sk_f, NSA_KV), s_slc_v=r4(sv_f, NSA_KV),
        s_win_k=win4(win_k_new), s_win_v=win4(win_v_new),
        s_ret=ret_new.reshape(nb, RET_HEADS, HEAD_DIM, HEAD_DIM))
    return y, out


def kernel(x_prompt, x_sample, cache_sb_k, cache_sb_v, cache_cmp_k, cache_cmp_v, cache_slc_k, cache_slc_v,
           state_win_k, state_win_v, state_ret, page_table, norm1_g, w_in, nsa_q_norm, nsa_k_norm, cmp_pos_k,
           cmp_pos_v, cmp_w_k, cmp_w_v, w_out, norm2_g, w_up, w_down):
    b, t, d = x_prompt.shape
    depth = w_in.shape[0]
    tm = min(512, t)
    pos_p = jnp.arange(t, dtype=I32)
    tabs_p = (*_rope_tables(pos_p, HEAD_DIM, RET_THETA, D_RET), *_rope_tables(pos_p, ROT_DIM, ROPE_THETA, LANES))
    xp = x_prompt.reshape(b * t, d)

    nb, n_new, _ = x_sample.shape
    assert n_new == 1, "sample group kernels handle one new token per sample"
    pg = cache_sb_k.shape[2]
    past_len = page_table.shape[1] * pg
    wb = state_win_k.shape[2]
    assert wb <= WINDOW and wb <= past_len
    pos_s = jnp.full((nb,), past_len, dtype=I32)
    tabs_s = (*_rope_tables(pos_s, HEAD_DIM, RET_THETA, D_RET), *_rope_tables(pos_s, ROT_DIM, ROPE_THETA, LANES))
    xs = x_sample.reshape(nb, d)
    caches = tuple(_pages_t(c) for c in (cache_sb_k, cache_sb_v, cache_cmp_k, cache_cmp_v, cache_slc_k, cache_slc_v))
    states = (_pages_t(state_win_k), _pages_t(state_win_v),
              state_ret.reshape(depth, nb, RET_HEADS * HEAD_DIM, HEAD_DIM))

    new = {}
    for l in range(depth):
        prm = _layer_params(l, norm1_g, w_in, nsa_q_norm, nsa_k_norm, cmp_pos_k, cmp_pos_v, cmp_w_k, cmp_w_v,
                            w_out, norm2_g, w_up, w_down)
        xp, p_new = _prompt_layer(xp, prm, tabs_p, b=b, t=t, tm=tm)
        xs, s_new = _sample_layer(l, xs, prm, tabs_s, caches, states, page_table, past_len=past_len)
        for name, val in {**p_new, **s_new}.items():
            new.setdefault(name, []).append(val)
    st = lambda name: jnp.stack(new[name])
    return (xp.reshape(b, t, d), xs.reshape(nb, 1, d)) + tuple(st(nm) for nm in (
        'p_sb_k', 'p_sb_v', 'p_cmp_k', 'p_cmp_v', 'p_slc_k', 'p_slc_v', 'p_win_k', 'p_win_v', 'p_ret',
        's_sb_k', 's_sb_v', 's_cmp_k', 's_cmp_v', 's_slc_k', 's_slc_v', 's_win_k', 's_win_v', 's_ret'))
```

```python
import functools
import math

import numpy as np
import jax
import jax.numpy as jnp
from jax import lax
from jax.experimental import pallas as pl
from jax.experimental.pallas import tpu as pltpu

HEAD_DIM = 64
SB_HEADS = 4
RET_HEADS = 4
NSA_HEADS = 8
NSA_KV = 2
NSA_REP = NSA_HEADS // NSA_KV
D_SB = SB_HEADS * HEAD_DIM
D_RET = RET_HEADS * HEAD_DIM
D_NSA = NSA_HEADS * HEAD_DIM
D_KV = NSA_KV * HEAD_DIM
ROPE_THETA = 500000.0
ROT_DIM = HEAD_DIM // 4
RET_THETA = 10000.0
BLK_CMP = 32
BLK_SEL = 64
TOP_N = 16
WINDOW = 512
FORCE_SCORE = 1.0e4
NEG = -1.0e30
EPS = 1e-6
QK_SCALE = HEAD_DIM ** -0.5

LANES = 128
VMEM_LIMIT = 56 * 1024 * 1024

BF = jnp.bfloat16
F32 = jnp.float32
I32 = jnp.int32

_C_SBQ, _C_SBK, _C_SBV = 0, 256, 512
_C_RQ, _C_RK, _C_RV, _C_RG = 768, 1024, 1280, 1536
_C_NQ = 1792
_C_CK, _C_CV, _C_SK, _C_SV, _C_WK, _C_WV = 2304, 2432, 2560, 2688, 2816, 2944
_C_NG = 3072
_N_IN = 3096
_N_IN_PAD = 3200


def _dot(a, b):
    return jnp.dot(a, b, preferred_element_type=F32)


def _dot_nt(a, b):
    return lax.dot_general(a, b, (((1,), (1,)), ((), ())), preferred_element_type=F32)


def _dot_tn(a, b):
    return lax.dot_general(a, b, (((0,), (0,)), ((), ())), preferred_element_type=F32)


def _split2(x):
    hi = x.astype(BF)
    lo = (x - hi.astype(F32)).astype(BF)
    return hi, lo


def _split3(x):
    hi = x.astype(BF)
    r = x - hi.astype(F32)
    mid = r.astype(BF)
    lo = (r - mid.astype(F32)).astype(BF)
    return hi, mid, lo


def _dot_x2(x, w):
    hi, lo = _split2(x)
    return _dot(hi, w) + _dot(lo, w)


def _seg_mean_mat(n):
    r = lax.broadcasted_iota(I32, (n, n), 0) // HEAD_DIM
    c = lax.broadcasted_iota(I32, (n, n), 1) // HEAD_DIM
    return jnp.where(r == c, 1.0 / HEAD_DIM, 0.0).astype(BF)


def _cparams(sem, vmem=VMEM_LIMIT):
    return pltpu.CompilerParams(dimension_semantics=sem, vmem_limit_bytes=vmem)


def _const_spec(shape):
    nd = len(shape)
    return pl.BlockSpec(shape, lambda *a: (0,) * nd)


def _proj_kernel(x_ref, g1_ref, wm_ref, cr_ref, sr_ref, cn_ref, sn_ref, gn_ref, gs_ref,
                 sbk_f, sbv_f, ck_f, cv_f, sk_f, sv_f, wk_f, wv_f,
                 sbq_b, sbk_b, sbv_b, rq_b, rk_b, rv_b, rg_f, nq_b, sk_b, sv_b, wk_b, wv_b, ng_f, *, t_out):
    x = x_ref[...]
    ms = jnp.mean(x * x, axis=-1, keepdims=True)
    xn = (x * lax.rsqrt(ms + EPS) * g1_ref[...]).astype(BF)

    def put(ref, val):
        ref[...] = jnp.transpose(val) if t_out else val

    def mm(w_ref, lo, n):
        return _dot(xn, w_ref[:, lo:lo + n])

    sbq_b[...] = (mm(wm_ref, _C_SBQ, D_SB) * QK_SCALE).astype(BF)
    k = mm(wm_ref, _C_SBK, D_SB)
    put(sbk_f, k)
    sbk_b[...] = k.astype(BF)
    v = mm(wm_ref, _C_SBV, D_SB)
    put(sbv_f, v)
    sbv_b[...] = v.astype(BF)

    cr = cr_ref[...]
    sr = sr_ref[...]
    def swap_halves(y, half):
        d = lax.broadcasted_iota(I32, (y.shape[0], LANES), 1) % HEAD_DIM
        cols = [y[:, c:c + LANES] for c in range(0, y.shape[1], LANES)]
        cols = [jnp.where(d < half, pltpu.roll(c, LANES - half, 1), pltpu.roll(c, half, 1)) for c in cols]
        return cols[0] if len(cols) == 1 else jnp.concatenate(cols, axis=1)

    y = mm(wm_ref, _C_RQ, D_RET)
    rq_b[...] = (y * cr + swap_halves(y, HEAD_DIM // 2) * sr).astype(BF)
    y = mm(wm_ref, _C_RK, D_RET)
    rk_b[...] = ((y * cr + swap_halves(y, HEAD_DIM // 2) * sr) * QK_SCALE).astype(BF)
    rv_b[...] = mm(wm_ref, _C_RV, D_RET).astype(BF)
    rg_f[...] = mm(wm_ref, _C_RG, D_RET)

    seg = _seg_mean_mat(LANES)
    cn = cn_ref[...]
    sn = sn_ref[...]

    def normrope(cm, gi):
        y = mm(wm_ref, cm, LANES)
        ysw = swap_halves(y, ROT_DIM // 2)
        r = lax.rsqrt(_dot_x2(y * y, seg) + EPS)
        g = gn_ref[gi:gi + 1, :]
        gsw = gs_ref[gi:gi + 1, :]
        return r * (y * (g * cn) + ysw * (gsw * sn))

    for c in range(D_NSA // LANES):
        nq_b[:, c * LANES:(c + 1) * LANES] = (
            normrope(_C_NQ + c * LANES, 0) * QK_SCALE).astype(BF)
    put(ck_f, normrope(_C_CK, 1))
    put(cv_f, mm(wm_ref, _C_CV, D_KV))
    k = normrope(_C_SK, 2)
    put(sk_f, k)
    sk_b[...] = k.astype(BF)
    v = mm(wm_ref, _C_SV, D_KV)
    put(sv_f, v)
    sv_b[...] = v.astype(BF)
    k = normrope(_C_WK, 3)
    put(wk_f, k)
    wk_b[...] = k.astype(BF)
    v = mm(wm_ref, _C_WV, D_KV)
    put(wv_f, v)
    wv_b[...] = v.astype(BF)
    ng_f[...] = jax.nn.sigmoid(mm(wm_ref, _C_NG, LANES))


def _project(x2d, g1, wm, tabs, gn, gs, *, tm, tab_blocks, seqs=None):
    n, d = x2d.shape
    cr, sr, cn, sn = tabs
    grid = (n // tm,)
    tok = lambda w: pl.BlockSpec((tm, w), lambda i: (i, 0))
    tab = lambda w: pl.BlockSpec((tm, w), lambda i: (i % tab_blocks, 0))
    f32 = lambda w: jax.ShapeDtypeStruct((n, w), F32)
    bf = lambda w: jax.ShapeDtypeStruct((n, w), BF)
    out_w_f = [D_SB, D_SB] + [D_KV] * 6
    if seqs is None:
        cache_shapes = [f32(w) for w in out_w_f]
        cache_specs = [tok(w) for w in out_w_f]
    else:
        t = n // seqs
        per = t // tm
        cache_shapes = [jax.ShapeDtypeStruct((seqs, w, t), F32) for w in out_w_f]
        cache_specs = [pl.BlockSpec((None, w, tm), lambda i: (i // per, 0, i % per)) for w in out_w_f]
    out_shape = (cache_shapes
                 + [bf(D_SB)] * 3 + [bf(D_RET)] * 3 + [f32(D_RET), bf(D_NSA)] + [bf(D_KV)] * 4 + [f32(LANES)])
    out_w = [D_SB] * 3 + [D_RET] * 3 + [D_RET, D_NSA] + [D_KV] * 4 + [LANES]
    return pl.pallas_call(
        functools.partial(_proj_kernel, t_out=seqs is not None),
        out_shape=out_shape,
        grid=grid,
        in_specs=[tok(d), _const_spec((1, d)), _const_spec(wm.shape),
                  tab(D_RET), tab(D_RET), tab(LANES), tab(LANES),
                  _const_spec(gn.shape), _const_spec(gs.shape)],
        out_specs=cache_specs + [tok(w) for w in out_w],
        compiler_params=_cparams(("parallel",)),
        name="proj",
    )(x2d, g1, wm, cr, sr, cn, sn, gn, gs)


SB_DEAD = -120.0


def _softplus(z):
    return jnp.maximum(z, 0.0) + jnp.log(1.0 + jnp.exp(-jnp.abs(z)))


def _sb_kernel(q_ref, k_ref, v_ref, o_ref, c_ref, acc_ref, *, tq, tk):
    i = pl.program_id(1)
    m = SB_HEADS * tq
    q = q_ref[...]
    lane_h = lax.broadcasted_iota(I32, (tq, D_SB), 1) // HEAD_DIM
    qs = jnp.concatenate([jnp.where(lane_h == h, q, jnp.zeros_like(q)) for h in range(SB_HEADS)], axis=0)
    q_pos = i * tq + lax.broadcasted_iota(I32, (m, tk), 0) % tq
    col = lax.broadcasted_iota(I32, (m, tk), 1)
    tri = (lax.broadcasted_iota(I32, (tk, tk), 0) > lax.broadcasted_iota(I32, (tk, tk), 1)).astype(BF)
    nt = (i * tq) // tk + 1

    c_ref[...] = jnp.zeros_like(c_ref)
    acc_ref[...] = jnp.zeros_like(acc_ref)

    def tile(j, masked):
        off = pl.multiple_of(j * tk, tk)
        z = _dot_nt(qs, k_ref[pl.ds(off, tk), :])
        l1m = -_softplus(z)
        if masked:
            mask = (off + col) < q_pos
            l1m = jnp.where(mask, l1m, 0.0)
        c = c_ref[...]
        after = c + _dot_x2(l1m, tri)
        lw = z + l1m + after
        if masked:
            lw = jnp.where(mask, lw, NEG)
        w = jnp.exp(lw)
        acc_ref[...] += _dot(w.astype(BF), v_ref[pl.ds(off, tk), :])
        c_ref[...] = c + jnp.sum(l1m, axis=1, keepdims=True)

    tile(nt - 1, True)

    def alive():
        return jnp.max(c_ref[...]) > SB_DEAD

    def cond(state):
        return (state[0] < nt - 1) & state[1]

    def body(state):
        tile(nt - 2 - state[0], False)
        return state[0] + 1, alive()

    lax.while_loop(cond, body, (jnp.int32(0), alive()))

    acc = acc_ref[...]
    out = jnp.zeros((tq, D_SB), F32)
    for h in range(SB_HEADS):
        out = out + jnp.where(lane_h == h, acc[h * tq:(h + 1) * tq], 0.0)
    o_ref[...] = out.astype(BF)


def _sb_prompt(q, k, v, *, b, t, tq=256, tk=256):
    n = b * t
    nq = t // tq
    return pl.pallas_call(
        functools.partial(_sb_kernel, tq=tq, tk=tk),
        out_shape=jax.ShapeDtypeStruct((n, D_SB), BF),
        grid=(b, nq),
        in_specs=[pl.BlockSpec((tq, D_SB), lambda bi, i: (bi * nq + i, 0)),
                  pl.BlockSpec((t, D_SB), lambda bi, i: (bi, 0)),
                  pl.BlockSpec((t, D_SB), lambda bi, i: (bi, 0))],
        out_specs=pl.BlockSpec((tq, D_SB), lambda bi, i: (bi * nq + i, 0)),
        scratch_shapes=[pltpu.VMEM((SB_HEADS * tq, 1), F32), pltpu.VMEM((SB_HEADS * tq, D_SB), F32)],
        compiler_params=_cparams(("parallel", "parallel")),
        name="sb_prompt",
    )(q, k, v)


def _ret_kernel(q_ref, k_ref, v_ref, g_ref, dec_ref, qd_ref, kd_ref, gc_ref, o_ref, st_ref, s_scr, *, c):
    ci = pl.program_id(1)

    @pl.when(ci == 0)
    def _():
        s_scr[...] = jnp.zeros_like(s_scr)

    q = q_ref[...]
    k = k_ref[...]
    v = v_ref[...]
    lane_h = lax.broadcasted_iota(I32, (c, D_RET), 1) // HEAD_DIM
    o = jnp.zeros((c, D_RET), F32)
    for h in range(RET_HEADS):
        kh = jnp.where(lane_h == h, k, jnp.zeros_like(k))
        vh = jnp.where(lane_h == h, v, jnp.zeros_like(v))
        s = _dot_nt(q, kh) * dec_ref[h]
        o = o + _dot(s.astype(BF), vh)
    st = s_scr[...]
    shi, slo = _split2(st)
    o = o + (_dot(q, shi) + _dot(q, slo)) * qd_ref[...]

    kd = (k.astype(F32) * kd_ref[...]).astype(BF)
    ktv = _dot_tn(kd, v)
    r = lax.broadcasted_iota(I32, (D_RET, D_RET), 0) // HEAD_DIM
    cc = lax.broadcasted_iota(I32, (D_RET, D_RET), 1) // HEAD_DIM
    new_st = st * gc_ref[...] + jnp.where(r == cc, ktv, 0.0)
    s_scr[...] = new_st
    st_ref[...] = new_st

    seg = _seg_mean_mat(D_RET)
    mu = _dot_x2(o, seg)
    d = o - mu
    var = _dot_x2(d * d, seg)
    y = d * lax.rsqrt(var + EPS)
    o_ref[...] = (y * jax.nn.silu(g_ref[...])).astype(BF)


def _ret_tables(c):
    log_gamma = np.log1p(-np.exp2(-5.0 - np.arange(RET_HEADS, dtype=np.float64)))
    idx = np.arange(c, dtype=np.float64)
    diff = idx[:, None] - idx[None, :]
    dec = np.where(diff >= 0, np.exp(np.maximum(diff, 0.0)[None] * log_gamma[:, None, None]), 0.0)
    lane_lg = np.repeat(log_gamma, HEAD_DIM)
    qd = np.exp((idx[:, None] + 1.0) * lane_lg[None, :])
    kd = np.exp((c - 1.0 - idx)[:, None] * lane_lg[None, :])
    gc = np.exp(c * lane_lg)[None, :]
    f = lambda a: jnp.asarray(a, F32)
    return f(dec), f(qd), f(kd), f(gc)


def _ret_prompt(q, k, v, g, *, b, t, c=256):
    n = b * t
    nc = t // c
    dec, qd, kd, gc = _ret_tables(c)
    tok = pl.BlockSpec((c, D_RET), lambda bi, i: (bi * nc + i, 0))
    o, st = pl.pallas_call(
        functools.partial(_ret_kernel, c=c),
        out_shape=[jax.ShapeDtypeStruct((n, D_RET), BF), jax.ShapeDtypeStruct((b, D_RET, D_RET), F32)],
        grid=(b, nc),
        in_specs=[tok, tok, tok, tok, _const_spec(dec.shape), _const_spec(qd.shape), _const_spec(kd.shape),
                  _const_spec(gc.shape)],
        out_specs=[tok, pl.BlockSpec((None, D_RET, D_RET), lambda bi, i: (bi, 0, 0))],
        scratch_shapes=[pltpu.VMEM((D_RET, D_RET), F32)],
        compiler_params=_cparams(("parallel", "arbitrary")),
        name="ret_prompt",
    )(q, k, v, g, dec, qd, kd, gc)
    return o, st


CMP_SPAN = 4096


def _compress_t(xt, post, phi_hi, phi_lo):
    n = xt.shape[1]
    sel = (lax.broadcasted_iota(I32, (n, LANES), 0) // BLK_CMP
           == lax.broadcasted_iota(I32, (n, LANES), 1)).astype(BF)
    xs_t = _dot_x2(xt * post, sel)
    hi, lo = _split2(jnp.transpose(xs_t))
    out = _dot(hi, phi_hi) + _dot(lo, phi_hi) + _dot(hi, phi_lo)
    return out[0:n // BLK_CMP]


def _compress_kernel(k_ref, v_ref, pk_ref, pv_ref, wk_ref, wv_ref, ko_ref, vo_ref):
    wk_hi, wk_lo = _split2(wk_ref[...])
    wv_hi, wv_lo = _split2(wv_ref[...])
    ko_ref[...] = _compress_t(k_ref[...], pk_ref[...], wk_hi, wk_lo).astype(BF)
    vo_ref[...] = _compress_t(v_ref[...], pv_ref[...], wv_hi, wv_lo).astype(BF)


def _compress_prompt(ckt, cvt, pkt, pvt, wk, wv):
    b, _, t = ckt.shape
    steps = t // CMP_SPAN
    tok = pl.BlockSpec((None, LANES, CMP_SPAN), lambda bi, i: (bi, 0, i))
    out = pl.BlockSpec((CMP_SPAN // BLK_CMP, LANES), lambda bi, i: (bi * steps + i, 0))
    return pl.pallas_call(
        _compress_kernel,
        out_shape=[jax.ShapeDtypeStruct((b * t // BLK_CMP, LANES), BF)] * 2,
        grid=(b, steps),
        in_specs=[tok, tok, _const_spec(pkt.shape), _const_spec(pvt.shape), _const_spec(wk.shape),
                  _const_spec(wv.shape)],
        out_specs=[out, out],
        compiler_params=_cparams(("parallel", "parallel")),
        name="compress_prompt",
    )(ckt, cvt, pkt, pvt, wk, wv)


def _stack_heads_q(q, extra):
    tq = q.shape[0]
    lane = lax.broadcasted_iota(I32, (tq, LANES), 1)
    rows = []
    for h in range(NSA_HEADS):
        g = h // NSA_REP
        blk = q[:, (h // 2) * LANES:(h // 2 + 1) * LANES]
        src_half = h % 2
        if src_half != g:
            blk32 = pltpu.roll(blk.astype(F32), HEAD_DIM, 1).astype(BF)
        else:
            blk32 = blk
        keep = (lane // HEAD_DIM) == g
        rows.append(jnp.where(keep, blk32, jnp.zeros_like(blk32)))
    qs = jnp.concatenate(rows, axis=0)
    if extra is not None:
        qs = jnp.concatenate([qs, extra], axis=1)
    return qs


def _unstack_heads(acc, tq):
    lane = lax.broadcasted_iota(I32, (tq, LANES), 1)
    cols = []
    for c in range(NSA_HEADS // 2):
        g = (2 * c) // NSA_REP
        a = acc[(2 * c) * tq:(2 * c + 1) * tq]
        b = acc[(2 * c + 1) * tq:(2 * c + 2) * tq]
        if g == 0:
            cols.append(jnp.where(lane < HEAD_DIM, a, pltpu.roll(b, HEAD_DIM, 1)))
        else:
            cols.append(jnp.where(lane < HEAD_DIM, pltpu.roll(a, HEAD_DIM, 1), b))
    return jnp.concatenate(cols, axis=1)


RANK_UNROLL = 8


def _order_key(v):
    return v


def _beats(key_j, key, j_before):
    return jnp.where(j_before, (key_j >= key).astype(I32), (key_j > key).astype(I32))


def _gate_expand(gates, branch):
    r = lax.broadcasted_iota(I32, (LANES, D_NSA), 0)
    c = lax.broadcasted_iota(I32, (LANES, D_NSA), 1)
    e = (r == (c // HEAD_DIM) * 3 + branch).astype(BF)
    return _dot_x2(gates, e)


def _nsa_kernel(q_ref, gate_ref, kc_ref, vc_ref, sk_ref, sv_ref, wk_ref, wv_ref, o_ref,
                k2_scr, svt_scr, wvt_scr, imp_scr, m_scr, l_scr, acc_scr, sa_scr, sb_scr, *, tq, tk, tw, t):
    i = pl.program_id(1)
    nc = t // BLK_CMP
    ns = t // BLK_SEL
    m = NSA_HEADS * tq

    @pl.when(i == 0)
    def _():
        rows = 512
        for r0 in range(0, t, rows):
            s_idx = r0 + lax.broadcasted_iota(I32, (rows, LANES), 0)
            c_idx = lax.broadcasted_iota(I32, (rows, LANES), 1)
            e = (s_idx // BLK_SEL == c_idx).astype(BF)
            k2_scr[r0:r0 + rows, :] = jnp.concatenate([sk_ref[r0:r0 + rows, :], e], axis=1)
        for c in range(t // tk):
            svt_scr[c] = jnp.transpose(sv_ref[c * tk:(c + 1) * tk, :].astype(F32)).astype(BF)
        for c in range(t // tw):
            wvt_scr[c] = jnp.transpose(wv_ref[c * tw:(c + 1) * tw, :].astype(F32)).astype(BF)

    q = q_ref[...]
    qs = _stack_heads_q(q, None)

    kc = kc_ref[...]
    vct = jnp.transpose(vc_ref[...].astype(F32)).astype(BF)
    n_idx = lax.broadcasted_iota(I32, (nc, m), 0)
    t_idx = i * tq + lax.broadcasted_iota(I32, (nc, m), 1) % tq
    valid = (n_idx + 1) * BLK_CMP - 1 <= t_idx
    st = jnp.where(valid, _dot_nt(kc, qs), NEG)
    mx = jnp.max(st, axis=0, keepdims=True)
    e = jnp.where(valid, jnp.exp(st - mx), 0.0)
    den = jnp.sum(e, axis=0, keepdims=True)
    p = e / jnp.where(den > 0.0, den, 1.0)
    o_cmp = _dot(vct, p.astype(BF))
    imp = []
    for g in range(NSA_KV):
        acc_g = p[:, g * NSA_REP * tq:(g * NSA_REP + 1) * tq]
        for h in range(g * NSA_REP + 1, (g + 1) * NSA_REP):
            acc_g = acc_g + p[:, h * tq:(h + 1) * tq]
        imp.append(acc_g)

    pair = (lax.broadcasted_iota(I32, (ns, nc), 1) // (BLK_SEL // BLK_CMP)
            == lax.broadcasted_iota(I32, (ns, nc), 0)).astype(BF)
    blk = lax.broadcasted_iota(I32, (ns, tq), 0)
    tb = i * tq + lax.broadcasted_iota(I32, (ns, tq), 1)
    forced = (blk == 0) | (blk == tb // BLK_SEL)
    future = blk * BLK_SEL > tb
    eye = (lax.broadcasted_iota(I32, (tq, tq), 0) == lax.broadcasted_iota(I32, (tq, tq), 1)).astype(BF)
    pens = []
    for g in range(NSA_KV):
        a, b, c = _split3(imp[g])
        v = _dot(pair, a) + _dot(pair, b) + _dot(pair, c)
        v = jnp.where(forced, FORCE_SCORE, jnp.where(future, -1.0, v))
        key = _order_key(v)
        imp_scr[g] = key

        def body(jj, cnt, g=g, key=key):
            for u in range(RANK_UNROLL):
                j = jj * RANK_UNROLL + u
                cnt = cnt + _beats(imp_scr[g, pl.ds(j, 1), :], key, blk > j)
            return cnt

        n_vis = jnp.minimum(((i + 1) * tq - 1) // BLK_SEL + 1, ns)
        cnt = lax.fori_loop(0, (n_vis + RANK_UNROLL - 1) // RANK_UNROLL, body, jnp.zeros((ns, tq), I32))
        sel_t = (cnt < min(TOP_N, ns)).astype(BF)
        pen = _dot_nt(eye, sel_t)
        pen = ((pen - 1.0) * (-NEG)).astype(BF)
        if ns < LANES:
            pen = jnp.concatenate([pen, jnp.zeros((tq, LANES - ns), BF)], axis=1)
        pens.append(pen)
    pen_rows = jnp.concatenate([pens[h // NSA_REP] for h in range(NSA_HEADS)], axis=0)
    qs2 = jnp.concatenate([qs, pen_rows], axis=1)

    q_pos = i * tq + lax.broadcasted_iota(I32, (1, m), 1) % tq

    def online(scores, vals_t, mask):
        if mask is not None:
            scores = jnp.where(mask, scores, NEG)
        m_old = m_scr[...]
        m_new = jnp.maximum(m_old, jnp.max(scores, axis=0, keepdims=True))
        alpha = jnp.exp(m_old - m_new)
        p = jnp.exp(scores - m_new)
        l_scr[...] = alpha * l_scr[...] + jnp.sum(p, axis=0, keepdims=True)
        acc_scr[...] = alpha * acc_scr[...] + _dot(vals_t, p.astype(BF))
        m_scr[...] = m_new

    def reset():
        m_scr[...] = jnp.full_like(m_scr, NEG)
        l_scr[...] = jnp.zeros_like(l_scr)
        acc_scr[...] = jnp.zeros_like(acc_scr)

    def result():
        return acc_scr[...] / l_scr[...]

    reset()
    nt = (i * tq) // tk + 1
    row_k = lax.broadcasted_iota(I32, (tk, 1), 0)

    def scores_into(buf, j):
        off = pl.multiple_of(j * tk, tk)
        buf[...] = _dot_nt(k2_scr[pl.ds(off, tk), :], qs2)

    def diag_mask():
        return ((nt - 1) * tk + row_k) <= q_pos

    n_pairs = (nt - 1) // 2
    scores_into(sa_scr, 0)

    def slc_body(pr, carry):
        scores_into(sb_scr, 2 * pr + 1)
        online(sa_scr[...], svt_scr[2 * pr], None)
        scores_into(sa_scr, 2 * pr + 2)
        online(sb_scr[...], svt_scr[2 * pr + 1], None)
        return carry

    lax.fori_loop(0, n_pairs, slc_body, 0)

    @pl.when((nt - 1) % 2 == 1)
    def _():
        scores_into(sb_scr, nt - 1)
        online(sa_scr[...], svt_scr[nt - 2], None)
        online(sb_scr[...], svt_scr[nt - 1], diag_mask())

    @pl.when((nt - 1) % 2 == 0)
    def _():
        online(sa_scr[...], svt_scr[nt - 1], diag_mask())

    o_slc = result()

    row_w = lax.broadcasted_iota(I32, (tw, 1), 0)
    n_band = (WINDOW + tq + tw - 1) // tw
    first = jnp.maximum(i * tq - WINDOW, 0) // tw
    s_w = []
    for c in range(n_band):
        off = pl.multiple_of((first + c) * tw, tw)
        k_pos = off + row_w
        s = _dot_nt(wk_ref[pl.ds(off, tw), :], qs)
        s_w.append(jnp.where((k_pos <= q_pos) & (q_pos - k_pos <= WINDOW), s, NEG))
    m_w = s_w[0].max(axis=0, keepdims=True)
    for s in s_w[1:]:
        m_w = jnp.maximum(m_w, s.max(axis=0, keepdims=True))
    l_w = jnp.zeros((1, m), F32)
    o_win = jnp.zeros((LANES, m), F32)
    for c, s in enumerate(s_w):
        p = jnp.exp(s - m_w)
        l_w = l_w + p.sum(axis=0, keepdims=True)
        o_win = o_win + _dot(wvt_scr[first + c], p.astype(BF))
    o_win = o_win / l_w

    gt = jnp.transpose(gate_ref[...])

    def gate_row(branch):
        return jnp.concatenate([gt[h * 3 + branch:h * 3 + branch + 1, :] for h in range(NSA_HEADS)], axis=1)

    mix = gate_row(0) * o_cmp + gate_row(1) * o_slc + gate_row(2) * o_win
    stacked = jnp.concatenate([jnp.transpose(mix[:, h * tq:(h + 1) * tq]) for h in range(NSA_HEADS)], axis=0)
    o_ref[...] = _unstack_heads(stacked, tq).astype(BF)


def _nsa_prompt(nq, gates, kc, vc, sk, sv, wk, wv, *, b, t, tq=256, tk=512, tw=128):
    assert t % tk == 0 and (WINDOW + tq + tw - 1) // tw <= t // tw and (t // BLK_SEL) % RANK_UNROLL == 0
    n = b * t
    nqb = t // tq
    nc = t // BLK_CMP
    ns = t // BLK_SEL
    m = NSA_HEADS * tq
    tok = lambda w: pl.BlockSpec((tq, w), lambda bi, i: (bi * nqb + i, 0))
    seq = lambda rows, w: pl.BlockSpec((rows, w), lambda bi, i: (bi, 0))
    return pl.pallas_call(
        functools.partial(_nsa_kernel, tq=tq, tk=tk, tw=tw, t=t),
        out_shape=jax.ShapeDtypeStruct((n, D_NSA), BF),
        grid=(b, nqb),
        in_specs=[tok(D_NSA), tok(LANES), seq(nc, LANES), seq(nc, LANES),
                  seq(t, LANES), seq(t, LANES), seq(t, LANES), seq(t, LANES)],
        out_specs=tok(D_NSA),
        scratch_shapes=[pltpu.VMEM((t, 2 * LANES), BF), pltpu.VMEM((t // tk, LANES, tk), BF),
                        pltpu.VMEM((t // tw, LANES, tw), BF), pltpu.VMEM((NSA_KV, ns, tq), F32),
                        pltpu.VMEM((1, m), F32), pltpu.VMEM((1, m), F32), pltpu.VMEM((LANES, m), F32),
                        pltpu.VMEM((tk, m), F32), pltpu.VMEM((tk, m), F32)],
        compiler_params=_cparams(("parallel", "arbitrary")),
        name="nsa_prompt",
    )(nq, gates, kc, vc, sk, sv, wk, wv)


def _out_mlp_kernel(x_ref, osb_ref, oret_ref, onsa_ref, wo_ref, g2_ref, wu_ref, wd_ref, y_ref, *, ff_chunk):
    h = (x_ref[...] + _dot(osb_ref[...].astype(BF), wo_ref[0:D_SB, :])
         + _dot(oret_ref[...].astype(BF), wo_ref[D_SB:D_SB + D_RET, :])
         + _dot(onsa_ref[...].astype(BF), wo_ref[D_SB + D_RET:, :]))
    ms = jnp.mean(h * h, axis=-1, keepdims=True)
    hn = (h * lax.rsqrt(ms + EPS) * g2_ref[...]).astype(BF)
    mlp = None
    d_ff = wu_ref.shape[1]
    for c0 in range(0, d_ff, ff_chunk):
        u = jnp.maximum(_dot(hn, wu_ref[:, c0:c0 + ff_chunk]), 0.0)
        part = _dot((u * u).astype(BF), wd_ref[c0:c0 + ff_chunk, :])
        mlp = part if mlp is None else mlp + part
    y_ref[...] = h + mlp


def _out_mlp(x2d, osb, oret, onsa, wo, g2, wu, wd, *, tm, ff_chunk=1024):
    n, d = x2d.shape
    tok = lambda w: pl.BlockSpec((tm, w), lambda i: (i, 0))
    return pl.pallas_call(
        functools.partial(_out_mlp_kernel, ff_chunk=ff_chunk),
        out_shape=jax.ShapeDtypeStruct((n, d), F32),
        grid=(n // tm,),
        in_specs=[tok(d), tok(D_SB), tok(D_RET), tok(D_NSA), _const_spec(wo.shape), _const_spec((1, d)),
                  _const_spec(wu.shape), _const_spec(wd.shape)],
        out_specs=tok(d),
        compiler_params=_cparams(("parallel",)),
        name="out_mlp",
    )(x2d, osb, oret, onsa, wo, g2, wu, wd)


SB_PAGES_PER_FETCH = 2
DEC_ROWS = 16


def _pages_t(c):
    dp, npool, pg, h, dd = c.shape
    return jnp.transpose(c, (0, 1, 3, 4, 2)).reshape(dp, npool, h * dd, pg)


def _page_spec(layer, width, page_of_step, pg):
    def imap(bi, p, pt, *rest):
        return (layer, pt[bi, page_of_step(p)], 0, 0)
    return pl.BlockSpec((None, None, width, pg), imap)


def _sb_dec_kernel(pt_ref, q_ref, k_hbm, v_hbm, o_ref, kbuf, vbuf, sem, c_scr, acc_scr, *, layer, npg, npp):
    bi = pl.program_id(0)
    rows = DEC_ROWS
    pg = kbuf.shape[2]
    nchunks = npg // npp

    def chunk_copies(c):
        out = []
        for kk in range(npp):
            page = pt_ref[bi, npg - 1 - (c * npp + kk)]
            out.append(pltpu.make_async_copy(k_hbm.at[layer, page], kbuf.at[kk], sem.at[0, kk]))
            out.append(pltpu.make_async_copy(v_hbm.at[layer, page], vbuf.at[kk], sem.at[1, kk]))
        return out

    c_scr[...] = jnp.zeros_like(c_scr)
    acc_scr[...] = jnp.zeros_like(acc_scr)
    lane_h = lax.broadcasted_iota(I32, (rows, D_SB), 1) // HEAD_DIM
    row = lax.broadcasted_iota(I32, (rows, D_SB), 0)
    q = jnp.broadcast_to(q_ref[...].astype(F32), (rows, D_SB))
    qs = jnp.where(lane_h == row, q, 0.0).astype(BF)
    tri = (lax.broadcasted_iota(I32, (pg, pg), 0) > lax.broadcasted_iota(I32, (pg, pg), 1)).astype(BF)
    head_rows = lax.broadcasted_iota(I32, (rows, 1), 0) < SB_HEADS

    def cond(state):
        return (state[0] < nchunks) & state[1]

    def body(state):
        ch = state[0]
        for cp in chunk_copies(ch):
            cp.start()
        for cp in chunk_copies(ch):
            cp.wait()
        c = c_scr[...]
        acc = acc_scr[...]
        for kk in range(npp):
            z = _dot(qs, kbuf[kk].astype(BF))
            l1m = -_softplus(z)
            after = c + _dot_x2(l1m, tri)
            w = jnp.exp(z + l1m + after)
            acc = acc + _dot_nt(w.astype(BF), vbuf[kk].astype(BF))
            c = c + jnp.sum(l1m, axis=1, keepdims=True)
        c_scr[...] = c
        acc_scr[...] = acc
        return ch + 1, jnp.max(jnp.where(head_rows, c, NEG)) > SB_DEAD

    lax.while_loop(cond, body, (jnp.int32(0), jnp.bool_(True)))
    o_ref[...] = jnp.sum(jnp.where(lane_h == row, acc_scr[...], 0.0), axis=0, keepdims=True)


def _sb_decode(layer, page_table, q, cache_k, cache_v):
    nb, npg = page_table.shape
    npp = math.gcd(SB_PAGES_PER_FETCH, npg)
    pg = cache_k.shape[3]
    row = pl.BlockSpec((None, 1, D_SB), lambda bi, pt: (bi, 0, 0))
    hbm = pl.BlockSpec(memory_space=pl.ANY)
    return pl.pallas_call(
        functools.partial(_sb_dec_kernel, layer=layer, npg=npg, npp=npp),
        out_shape=jax.ShapeDtypeStruct((nb, 1, D_SB), F32),
        grid_spec=pltpu.PrefetchScalarGridSpec(
            num_scalar_prefetch=1, grid=(nb,),
            in_specs=[row, hbm, hbm], out_specs=row,
            scratch_shapes=[pltpu.VMEM((npp, D_SB, pg), F32), pltpu.VMEM((npp, D_SB, pg), F32),
                            pltpu.SemaphoreType.DMA((2, npp)),
                            pltpu.VMEM((DEC_ROWS, 1), F32), pltpu.VMEM((DEC_ROWS, D_SB), F32)]),
        compiler_params=_cparams(("arbitrary",)),
        name="sb_decode",
    )(page_table, q, cache_k, cache_v)


def _cmp_dec_kernel(pt_ref, kn_ref, vn_ref, pk_ref, pv_ref, pkt_ref, pvt_ref, wk_ref, wv_ref, *refs,
                    npp, last, nblk):
    k_refs, v_refs = refs[:npp], refs[npp:2 * npp]
    ko_ref, vo_ref = refs[2 * npp:]
    p = pl.program_id(1)
    wk_hi, wk_lo = _split2(wk_ref[...])
    wv_hi, wv_lo = _split2(wv_ref[...])
    pg = k_refs[0].shape[1]
    per = npp * pg // BLK_CMP
    r0 = pl.multiple_of(p * per, per)
    ko_ref[pl.ds(r0, per), :] = _compress_t(
        jnp.concatenate([r[...] for r in k_refs], axis=1), pkt_ref[...], wk_hi, wk_lo)
    vo_ref[pl.ds(r0, per), :] = _compress_t(
        jnp.concatenate([r[...] for r in v_refs], axis=1), pvt_ref[...], wv_hi, wv_lo)

    @pl.when(p == last)
    def _():
        first = lax.broadcasted_iota(I32, (DEC_ROWS, LANES), 0) == 0
        xk = jnp.where(first, kn_ref[...] * pk_ref[0:1, :], 0.0)
        xv = jnp.where(first, vn_ref[...] * pv_ref[0:1, :], 0.0)
        a, b = _split2(xk)
        ko_ref[nblk:nblk + DEC_ROWS, :] = _dot(a, wk_hi) + _dot(b, wk_hi) + _dot(a, wk_lo)
        a, b = _split2(xv)
        vo_ref[nblk:nblk + DEC_ROWS, :] = _dot(a, wv_hi) + _dot(b, wv_hi) + _dot(a, wv_lo)


def _cmp_decode(layer, page_table, ck_new, cv_new, cache_k, cache_v, pk, pv, pkt, pvt, wk, wv):
    nb, npg = page_table.shape
    pg = cache_k.shape[3]
    npp = CMP_SPAN // pg
    assert npg % npp == 0
    steps = npg // npp
    nblk = npg * pg // BLK_CMP
    specs = [_page_spec(layer, D_KV, (lambda p, kk=kk: p * npp + kk), pg) for kk in range(npp)]
    row = pl.BlockSpec((None, 1, D_KV), lambda bi, p, pt: (bi, 0, 0))
    cst = lambda a: pl.BlockSpec(a.shape, lambda bi, p, pt: (0,) * a.ndim)
    out = pl.BlockSpec((None, nblk + DEC_ROWS, LANES), lambda bi, p, pt: (bi, 0, 0))
    return pl.pallas_call(
        functools.partial(_cmp_dec_kernel, npp=npp, last=steps - 1, nblk=nblk),
        out_shape=[jax.ShapeDtypeStruct((nb, nblk + DEC_ROWS, LANES), F32)] * 2,
        grid_spec=pltpu.PrefetchScalarGridSpec(
            num_scalar_prefetch=1, grid=(nb, steps),
            in_specs=[row, row, cst(pk), cst(pv), cst(pkt), cst(pvt), cst(wk), cst(wv)] + specs + specs,
            out_specs=[out, out]),
        compiler_params=_cparams(("parallel", "arbitrary")),
        name="cmp_decode",
    )(page_table, ck_new, cv_new, pk, pv, pkt, pvt, wk, wv, *([cache_k] * npp), *([cache_v] * npp))


def _stack_heads_q_row(q):
    qf = q.astype(F32)
    row = lax.broadcasted_iota(I32, (DEC_ROWS, LANES), 0)
    lane_g = lax.broadcasted_iota(I32, (DEC_ROWS, LANES), 1) // HEAD_DIM
    out = jnp.zeros((DEC_ROWS, LANES), F32)
    for h in range(NSA_HEADS):
        g = h // NSA_REP
        blk = jnp.broadcast_to(qf[:, (h // 2) * LANES:(h // 2 + 1) * LANES], (DEC_ROWS, LANES))
        if h % 2 != g:
            blk = pltpu.roll(blk, HEAD_DIM, 1)
        out = jnp.where((row == h) & (lane_g == g), blk, out)
    return out.astype(BF)


def _unstack_heads_row(acc):
    lane = lax.broadcasted_iota(I32, (1, LANES), 1)
    accr = pltpu.roll(acc, HEAD_DIM, 1)
    cols = []
    for c in range(NSA_HEADS // 2):
        g = (2 * c) // NSA_REP
        if g == 0:
            cols.append(jnp.where(lane < HEAD_DIM, acc[2 * c:2 * c + 1], accr[2 * c + 1:2 * c + 2]))
        else:
            cols.append(jnp.where(lane < HEAD_DIM, accr[2 * c:2 * c + 1], acc[2 * c + 1:2 * c + 2]))
    return jnp.concatenate(cols, axis=1)


def _sel_dec_kernel(q_ref, kc_ref, vc_ref, ocmp_ref, idx_ref, imp_scr, *, q_pos, nblk, ns_pad):
    nrow = kc_ref.shape[0]
    qs = _stack_heads_q_row(q_ref[...])
    qs = jnp.concatenate([qs, jnp.zeros((LANES - DEC_ROWS, LANES), BF)], axis=0)
    kc = kc_ref[...].astype(BF)
    st = _dot_nt(kc, qs)
    n_idx = lax.broadcasted_iota(I32, (nrow, LANES), 0)
    valid = (n_idx + 1) * BLK_CMP - 1 <= q_pos
    st = jnp.where(valid, st, NEG)
    mx = jnp.max(st, axis=0, keepdims=True)
    e = jnp.where(valid, jnp.exp(st - mx), 0.0)
    den = jnp.sum(e, axis=0, keepdims=True)
    p = e / jnp.where(den > 0.0, den, 1.0)
    ocmp_ref[...] = _dot_tn(p.astype(BF), vc_ref[...].astype(BF))[0:NSA_HEADS]

    grp = (lax.broadcasted_iota(I32, (LANES, LANES), 0) // NSA_REP
           == lax.broadcasted_iota(I32, (LANES, LANES), 1)).astype(BF)
    a, b, c = _split3(p)
    impc = _dot(a, grp) + _dot(b, grp) + _dot(c, grp)
    pair = (lax.broadcasted_iota(I32, (ns_pad, nrow), 1) // (BLK_SEL // BLK_CMP)
            == lax.broadcasted_iota(I32, (ns_pad, nrow), 0)).astype(BF)
    a, b, c = _split3(impc)
    v = _dot(pair, a) + _dot(pair, b) + _dot(pair, c)
    ns = (nblk * BLK_CMP + 1 + BLK_SEL - 1) // BLK_SEL
    blk = lax.broadcasted_iota(I32, (ns_pad, LANES), 0)
    forced = (blk == 0) | (blk == q_pos // BLK_SEL)
    future = blk * BLK_SEL > q_pos
    v = jnp.where(forced, FORCE_SCORE, jnp.where(future, -1.0, v))
    v = jnp.where(blk < ns, v, -1.0)
    key = _order_key(v)
    imp_scr[...] = key

    def body(j, cnt):
        return cnt + _beats(imp_scr[pl.ds(j, 1), :], key, blk > j)

    cnt = lax.fori_loop(0, ns_pad, body, jnp.zeros((ns_pad, LANES), I32), unroll=8)
    blk_f = blk.astype(F32)
    rows = [jnp.sum(jnp.where(cnt == r, blk_f, 0.0), axis=0, keepdims=True) for r in range(TOP_N)]
    idx_ref[...] = jnp.concatenate(rows, axis=0).astype(I32)


def _sel_decode(nq, kc, vc, *, q_pos, nblk):
    nb = nq.shape[0]
    nrow = kc.shape[1]
    ns_pad = ((nrow // 2 + 7) // 8) * 8
    return pl.pallas_call(
        functools.partial(_sel_dec_kernel, q_pos=q_pos, nblk=nblk, ns_pad=ns_pad),
        out_shape=[jax.ShapeDtypeStruct((nb, NSA_HEADS, LANES), F32), jax.ShapeDtypeStruct((nb, TOP_N, LANES), I32)],
        grid=(nb,),
        in_specs=[pl.BlockSpec((None, 1, D_NSA), lambda bi: (bi, 0, 0)),
                  pl.BlockSpec((None, nrow, LANES), lambda bi: (bi, 0, 0)),
                  pl.BlockSpec((None, nrow, LANES), lambda bi: (bi, 0, 0))],
        out_specs=[pl.BlockSpec((None, NSA_HEADS, LANES), lambda bi: (bi, 0, 0)),
                   pl.BlockSpec((None, TOP_N, LANES), lambda bi: (bi, 0, 0))],
        scratch_shapes=[pltpu.VMEM((ns_pad, LANES), F32)],
        compiler_params=_cparams(("parallel",)),
        name="sel_decode",
    )(nq, kc, vc)


def _nsa_dec_kernel(pt_ref, idx_ref, q_ref, gate_ref, ocmp_ref, skn_ref, svn_ref, wkn_ref, wvn_ref,
                    wkc_ref, wvc_ref, wk_ref, wv_ref, ck_hbm, cv_hbm, o_ref, wko_ref, wvo_ref, kbuf, vbuf, sem,
                    *, layer, ncache, per):
    nsel = NSA_KV * TOP_N
    bi = pl.program_id(0)
    pg = kbuf.shape[2]

    def page_copies(j):
        blk = jnp.minimum(idx_ref[bi, j], ncache - 1)
        page = pt_ref[bi, blk // per]
        return (pltpu.make_async_copy(ck_hbm.at[layer, page], kbuf.at[j], sem.at[0, j]),
                pltpu.make_async_copy(cv_hbm.at[layer, page], vbuf.at[j], sem.at[1, j]))

    for j in range(nsel):
        for cp in page_copies(j):
            cp.start()

    qs = _stack_heads_q_row(q_ref[...])
    qf = qs.astype(F32)
    row_g = lax.broadcasted_iota(I32, (DEC_ROWS, LANES), 0) // NSA_REP

    def attend(s, vt, s_new, v_new):
        mx = jnp.maximum(s_new, jnp.max(s, axis=1, keepdims=True))
        p_new = jnp.exp(s_new - mx)
        p = jnp.exp(s - mx)
        den = p_new + jnp.sum(p, axis=1, keepdims=True)
        return (p_new * v_new + _dot_nt(p.astype(BF), vt)) / den

    bf_row = lambda r: r[...].astype(BF).astype(F32)
    wk = wk_ref[...]
    wv = wv_ref[...]
    s_wn = jnp.sum(qf * bf_row(wkn_ref), axis=1, keepdims=True)
    o_win = attend(_dot(qs, wk.astype(BF)), wv.astype(BF), s_wn, bf_row(wvn_ref))

    wb = wk.shape[1]
    last = lax.broadcasted_iota(I32, (D_KV, wb), 1) == wb - 1
    wko_ref[...] = jnp.where(last, wkc_ref[...], pltpu.roll(wk, wb - 1, 1))
    wvo_ref[...] = jnp.where(last, wvc_ref[...], pltpu.roll(wv, wb - 1, 1))

    for j in range(nsel):
        for cp in page_copies(j):
            cp.wait()

    s_new = jnp.sum(qf * bf_row(skn_ref), axis=1, keepdims=True)
    nkeys = TOP_N * pg
    col = lax.broadcasted_iota(I32, (DEC_ROWS, nkeys), 1)
    col_slot = col // pg
    col_blk = (col % pg) // BLK_SEL
    o_slc = jnp.zeros((DEC_ROWS, LANES), F32)
    for g in range(NSA_KV):
        pen = jnp.full((DEC_ROWS, nkeys), NEG, F32)
        for r in range(TOP_N):
            blk = idx_ref[bi, g * TOP_N + r]
            want = jnp.where(blk < ncache, blk % per, -1)
            pen = jnp.where((col_slot == r) & (col_blk == want), 0.0, pen)
        kcat = jnp.concatenate([kbuf[g * TOP_N + r] for r in range(TOP_N)], axis=1).astype(BF)
        vcat = jnp.concatenate([vbuf[g * TOP_N + r] for r in range(TOP_N)], axis=1).astype(BF)
        o_g = attend(_dot(qs, kcat) + pen, vcat, s_new, bf_row(svn_ref))
        o_slc = jnp.where(row_g == g, o_g, o_slc)

    gates = jnp.broadcast_to(gate_ref[...], (DEC_ROWS, LANES))
    out = (_gate_expand(gates, 0)[0:1] * _unstack_heads_row(ocmp_ref[...])
           + _gate_expand(gates, 1)[0:1] * _unstack_heads_row(o_slc)
           + _gate_expand(gates, 2)[0:1] * _unstack_heads_row(o_win))
    o_ref[...] = out


def _nsa_decode(layer, page_table, idx, nq, gates, ocmp, sk_new, sv_new, wk_new, wv_new, win_k, win_v,
                cache_k, cache_v):
    nb, npg = page_table.shape
    pg = cache_k.shape[3]
    per = pg // BLK_SEL
    ncache = npg * per
    wb = win_k.shape[3]
    nsel = NSA_KV * TOP_N

    row = lambda w: pl.BlockSpec((None, 1, w), lambda bi, pt, ix: (bi, 0, 0))
    colspec = pl.BlockSpec((None, D_KV, 1), lambda bi, pt, ix: (bi, 0, 0))
    win = pl.BlockSpec((None, None, D_KV, wb), lambda bi, pt, ix: (layer, bi, 0, 0))
    wout = pl.BlockSpec((None, D_KV, wb), lambda bi, pt, ix: (bi, 0, 0))
    hbm = pl.BlockSpec(memory_space=pl.ANY)
    col = lambda a: a.reshape(nb, D_KV, 1)
    return pl.pallas_call(
        functools.partial(_nsa_dec_kernel, layer=layer, ncache=ncache, per=per),
        out_shape=[jax.ShapeDtypeStruct((nb, 1, D_NSA), F32), jax.ShapeDtypeStruct((nb, D_KV, wb), F32),
                   jax.ShapeDtypeStruct((nb, D_KV, wb), F32)],
        grid_spec=pltpu.PrefetchScalarGridSpec(
            num_scalar_prefetch=2, grid=(nb,),
            in_specs=[row(D_NSA), row(LANES), pl.BlockSpec((None, NSA_HEADS, LANES), lambda bi, pt, ix: (bi, 0, 0)),
                      row(D_KV), row(D_KV), row(D_KV), row(D_KV), colspec, colspec, win, win, hbm, hbm],
            out_specs=[row(D_NSA), wout, wout],
            scratch_shapes=[pltpu.VMEM((nsel, D_KV, pg), F32), pltpu.VMEM((nsel, D_KV, pg), F32),
                            pltpu.SemaphoreType.DMA((2, nsel))]),
        compiler_params=_cparams(("arbitrary",)),
        name="nsa_decode",
    )(page_table, idx, nq, gates, ocmp, sk_new, sv_new, wk_new, wv_new, col(wk_new), col(wv_new),
      win_k, win_v, cache_k, cache_v)


def _ret_dec_kernel(q_ref, k_ref, kcol_ref, v_ref, g_ref, gcol_ref, grow_ref, st_ref, o_ref, sto_ref):
    rows = DEC_ROWS
    st = st_ref[...]
    q = jnp.broadcast_to(q_ref[...], (rows, D_RET))
    v = jnp.broadcast_to(v_ref[...], (rows, D_RET))
    lane_h = lax.broadcasted_iota(I32, (rows, D_RET), 1) // HEAD_DIM
    row = lax.broadcasted_iota(I32, (rows, D_RET), 0)
    own = lane_h == row
    qs = jnp.where(own, q, 0.0).astype(BF)
    spread = (lax.broadcasted_iota(I32, (HEAD_DIM, D_RET), 1) % HEAD_DIM
              == lax.broadcasted_iota(I32, (HEAD_DIM, D_RET), 0)).astype(BF)
    shi, slo = _split2(st)
    cross = _dot(qs, shi) + _dot(qs, slo)
    o_cross = jnp.sum(jnp.where(own, _dot_x2(cross, spread), 0.0), axis=0, keepdims=True) * grow_ref[...]
    seg = _seg_mean_mat(D_RET)
    qk = q * jnp.broadcast_to(k_ref[...], (rows, D_RET))
    o = o_cross + (_dot_x2(qk, seg) * float(HEAD_DIM)) * v

    v4 = _dot_nt(jnp.where(own, v, 0.0).astype(BF), spread)
    pick = (lax.broadcasted_iota(I32, (D_RET, rows), 0) // HEAD_DIM
            == lax.broadcasted_iota(I32, (D_RET, rows), 1)).astype(BF)
    vexp = _dot(pick, v4.astype(BF))
    sto_ref[...] = st * gcol_ref[...] + kcol_ref[...] * vexp

    mu = _dot_x2(o, seg)
    d = o - mu
    var = _dot_x2(d * d, seg)
    y = d * lax.rsqrt(var + EPS)
    o_ref[...] = y[0:1] * jax.nn.silu(g_ref[...])


def _ret_decode(layer, q, k, v, g, state):
    nb = q.shape[0]
    log_gamma = np.log1p(-np.exp2(-5.0 - np.arange(RET_HEADS, dtype=np.float64)))
    gam = np.repeat(np.exp(log_gamma), HEAD_DIM)
    gcol = jnp.asarray(gam[:, None], F32)
    grow = jnp.asarray(gam[None, :], F32)
    kcol = k.reshape(nb, D_RET, 1)
    row = pl.BlockSpec((None, 1, D_RET), lambda bi: (bi, 0, 0))
    return pl.pallas_call(
        _ret_dec_kernel,
        out_shape=[jax.ShapeDtypeStruct((nb, 1, D_RET), F32), jax.ShapeDtypeStruct((nb, D_RET, HEAD_DIM), F32)],
        grid=(nb,),
        in_specs=[row, row, pl.BlockSpec((None, D_RET, 1), lambda bi: (bi, 0, 0)), row, row,
                  _const_spec(gcol.shape), _const_spec(grow.shape),
                  pl.BlockSpec((None, None, D_RET, HEAD_DIM), lambda bi: (layer, bi, 0, 0))],
        out_specs=[row, pl.BlockSpec((None, D_RET, HEAD_DIM), lambda bi: (bi, 0, 0))],
        compiler_params=_cparams(("parallel",)),
        name="ret_decode",
    )(q, k, kcol, v, g, gcol, grow, state)


def _swap_perm(rot_dim):
    half = rot_dim // 2
    p = np.arange(HEAD_DIM)
    p[:half] = np.arange(half, rot_dim)
    p[half:rot_dim] = np.arange(half)
    return p


def _rope_tables(pos, rot_dim, theta, width):
    half = rot_dim // 2
    inv = jnp.exp(-math.log(theta) * jnp.arange(half, dtype=F32) / half)
    ang = pos.astype(F32)[:, None] * inv[None, :]
    cos, sin = jnp.cos(ang), jnp.sin(ang)
    ones = jnp.ones((pos.shape[0], HEAD_DIM - rot_dim), F32)
    c = jnp.concatenate([cos, cos, ones], axis=1)
    s = jnp.concatenate([-sin, sin, 0.0 * ones], axis=1)
    reps = width // HEAD_DIM
    return jnp.tile(c, (1, reps)), jnp.tile(s, (1, reps))


def _layer_params(l, norm1_g, w_in, nsa_q_norm, nsa_k_norm, cmp_pos_k, cmp_pos_v, cmp_w_k, cmp_w_v, w_out,
                  norm2_g, w_up, w_down):
    w = w_in[l]
    d = w.shape[0]
    wm = jnp.pad(w, ((0, 0), (0, _N_IN_PAD - _N_IN))).astype(BF)
    gains = jnp.concatenate([nsa_q_norm[l][None], nsa_k_norm[l]], axis=0)
    gn = jnp.tile(gains, (1, LANES // HEAD_DIM))
    gs = jnp.tile(gains[:, _swap_perm(ROT_DIM)], (1, LANES // HEAD_DIM))
    eye2 = jnp.eye(NSA_KV, dtype=F32)
    return dict(
        g1=norm1_g[l][None], wm=wm, gn=gn, gs=gs,
        pk=jnp.tile(cmp_pos_k[l], (1, NSA_KV)), pv=jnp.tile(cmp_pos_v[l], (1, NSA_KV)),
        pkt=jnp.tile(cmp_pos_k[l].T, (NSA_KV, CMP_SPAN // BLK_CMP)),
        pvt=jnp.tile(cmp_pos_v[l].T, (NSA_KV, CMP_SPAN // BLK_CMP)),
        phik=jnp.kron(eye2, cmp_w_k[l]), phiv=jnp.kron(eye2, cmp_w_v[l]),
        wo=w_out[l].astype(BF), g2=norm2_g[l][None], wu=w_up[l].astype(BF), wd=w_down[l].astype(BF))


def _prompt_layer(xp2d, prm, tabs, *, b, t, tm):
    (sbk_f, sbv_f, ck_f, cv_f, sk_f, sv_f, wk_f, wv_f,
     sbq_b, sbk_b, sbv_b, rq_b, rk_b, rv_b, rg_f, nq_b, sk_b, sv_b, wk_b, wv_b, ng_f) = _project(
        xp2d, prm['g1'], prm['wm'], tabs, prm['gn'], prm['gs'], tm=tm, tab_blocks=t // tm, seqs=b)
    o_sb = _sb_prompt(sbq_b, sbk_b, sbv_b, b=b, t=t)
    o_ret, ret_st = _ret_prompt(rq_b, rk_b, rv_b, rg_f, b=b, t=t)
    kc, vc = _compress_prompt(ck_f, cv_f, prm['pkt'], prm['pvt'], prm['phik'], prm['phiv'])
    o_nsa = _nsa_prompt(nq_b, ng_f, kc, vc, sk_b, sv_b, wk_b, wv_b, b=b, t=t)
    y = _out_mlp(xp2d, o_sb, o_ret, o_nsa, prm['wo'], prm['g2'], prm['wu'], prm['wd'], tm=tm)
    keep = min(WINDOW, t)
    r4 = lambda a, h: jnp.transpose(a.reshape(b, h, HEAD_DIM, a.shape[-1]), (0, 3, 1, 2))
    st = ret_st.reshape(b, RET_HEADS, HEAD_DIM, RET_HEADS, HEAD_DIM)
    st = jnp.stack([st[:, h, :, h, :] for h in range(RET_HEADS)], axis=1)
    caches = dict(
        p_sb_k=r4(sbk_f, SB_HEADS), p_sb_v=r4(sbv_f, SB_HEADS),
        p_cmp_k=r4(ck_f, NSA_KV), p_cmp_v=r4(cv_f, NSA_KV),
        p_slc_k=r4(sk_f, NSA_KV), p_slc_v=r4(sv_f, NSA_KV),
        p_win_k=r4(wk_f[:, :, t - keep:], NSA_KV), p_win_v=r4(wv_f[:, :, t - keep:], NSA_KV),
        p_ret=st)
    return y, caches


def _sample_layer(l, xs2d, prm, tabs, caches, states, page_table, *, past_len):
    nb = xs2d.shape[0]
    (sbk_f, sbv_f, ck_f, cv_f, sk_f, sv_f, wk_f, wv_f,
     sbq_b, sbk_b, sbv_b, rq_b, rk_b, rv_b, rg_f, nq_b, sk_b, sv_b, wk_b, wv_b, ng_f) = _project(
        xs2d, prm['g1'], prm['wm'], tabs, prm['gn'], prm['gs'], tm=nb, tab_blocks=1)
    row = lambda a: a.astype(F32).reshape(nb, 1, a.shape[-1])
    c_sb_k, c_sb_v, c_cmp_k, c_cmp_v, c_slc_k, c_slc_v = caches
    win_k, win_v, st_ret = states
    pg = c_cmp_k.shape[3]
    nblk = page_table.shape[1] * pg // BLK_CMP
    o_sb = _sb_decode(l, page_table, row(sbq_b), c_sb_k, c_sb_v)
    o_ret, ret_new = _ret_decode(l, row(rq_b), row(rk_b), row(rv_b), row(rg_f), st_ret)
    kc, vc = _cmp_decode(l, page_table, row(ck_f), row(cv_f), c_cmp_k, c_cmp_v,
                         prm['pk'], prm['pv'], prm['pkt'], prm['pvt'], prm['phik'], prm['phiv'])
    ocmp, idx = _sel_decode(row(nq_b), kc, vc, q_pos=past_len, nblk=nblk)
    idx = jnp.transpose(idx[:, :, :NSA_KV], (0, 2, 1)).reshape(nb, NSA_KV * TOP_N)
    o_nsa, win_k_new, win_v_new = _nsa_decode(
        l, page_table, idx, row(nq_b), row(ng_f), ocmp, row(sk_f), row(sv_f), row(wk_f), row(wv_f),
        win_k, win_v, c_slc_k, c_slc_v)
    y = _out_mlp(xs2d, o_sb.reshape(nb, D_SB), o_ret.reshape(nb, D_RET), o_nsa.reshape(nb, D_NSA),
                 prm['wo'], prm['g2'], prm['wu'], prm['wd'], tm=nb)
    r4 = lambda a, h: a.reshape(nb, 1, h, HEAD_DIM)
    wb = win_k.shape[3]
    win4 = lambda a: jnp.transpose(a.reshape(nb, NSA_KV, HEAD_DIM, wb), (0, 3, 1, 2))
    out = dict(
        s_sb_k=r4(sbk_f, SB_HEADS), s_sb_v=r4(sbv_f, SB_HEADS),
        s_cmp_k=r4(ck_f, NSA_KV), s_cmp_v=r4(cv_f, NSA_KV),
        s_slc_k=r4(sk_f, NSA_KV), s_slc_v=r4(sv_f, NSA_KV),
        s_win_k=win4(win_k_new), s_win_v=win4(win_v_new),
        s_ret=ret_new.reshape(nb, RET_HEADS, HEAD_DIM, HEAD_DIM))
    return y, out


def kernel(x_prompt, x_sample, cache_sb_k, cache_sb_v, cache_cmp_k, cache_cmp_v, cache_slc_k, cache_slc_v,
           state_win_k, state_win_v, state_ret, page_table, norm1_g, w_in, nsa_q_norm, nsa_k_norm, cmp_pos_k,
           cmp_pos_v, cmp_w_k, cmp_w_v, w_out, norm2_g, w_up, w_down):
    b, t, d = x_prompt.shape
    depth = w_in.shape[0]
    tm = min(512, t)
    pos_p = jnp.arange(t, dtype=I32)
    tabs_p = (*_rope_tables(pos_p, HEAD_DIM, RET_THETA, D_RET), *_rope_tables(pos_p, ROT_DIM, ROPE_THETA, LANES))
    xp = x_prompt.reshape(b * t, d)

    nb, n_new, _ = x_sample.shape
    assert n_new == 1, "sample group kernels handle one new token per sample"
    pg = cache_sb_k.shape[2]
    past_len = page_table.shape[1] * pg
    wb = state_win_k.shape[2]
    assert wb <= WINDOW and wb <= past_len
    pos_s = jnp.full((nb,), past_len, dtype=I32)
    tabs_s = (*_rope_tables(pos_s, HEAD_DIM, RET_THETA, D_RET), *_rope_tables(pos_s, ROT_DIM, ROPE_THETA, LANES))
    xs = x_sample.reshape(nb, d)
    caches = tuple(_pages_t(c) for c in (cache_sb_k, cache_sb_v, cache_cmp_k, cache_cmp_v, cache_slc_k, cache_slc_v))
    states = (_pages_t(state_win_k), _pages_t(state_win_v),
              state_ret.reshape(depth, nb, RET_HEADS * HEAD_DIM, HEAD_DIM))

    new = {}
    for l in range(depth):
        prm = _layer_params(l, norm1_g, w_in, nsa_q_norm, nsa_k_norm, cmp_pos_k, cmp_pos_v, cmp_w_k, cmp_w_v,
                            w_out, norm2_g, w_up, w_down)
        xp, p_new = _prompt_layer(xp, prm, tabs_p, b=b, t=t, tm=tm)
        xs, s_new = _sample_layer(l, xs, prm, tabs_s, caches, states, page_table, past_len=past_len)
        for name, val in {**p_new, **s_new}.items():
            new.setdefault(name, []).append(val)
    st = lambda name: jnp.stack(new[name])
    return (xp.reshape(b, t, d), xs.reshape(nb, 1, d)) + tuple(st(nm) for nm in (
        'p_sb_k', 'p_sb_v', 'p_cmp_k', 'p_cmp_v', 'p_slc_k', 'p_slc_v', 'p_win_k', 'p_win_v', 'p_ret',
        's_sb_k', 's_sb_v', 's_cmp_k', 's_cmp_v', 's_slc_k', 's_slc_v', 's_win_k', 's_win_v', 's_ret'))
```
